```python
import functools
import jax, jax.numpy as jnp
from jax import lax
import numpy as np

D_MODEL = 1024
BATCH = 16
SEQ = 4096
DEPTH = 1
DEC_BATCH = 128
DEC_SEQ = 1
PAST_LEN = 8192
PAGE_SIZE = 128

ATT_GROUPS = ((128, 1), (512, 4), (2048, 16))
N_GROUPS = 3
ATT_HEADS = 8
ATT_HEAD_DIM = 64
ATT_WIDTH = N_GROUPS * ATT_HEADS * ATT_HEAD_DIM
ATT_OUT = ATT_HEADS * ATT_HEAD_DIM
HG_HEADS = 8
HG_DK = 128
HG_DV = D_MODEL // HG_HEADS
HG_CHUNK = 64
PEER_HEADS = 8
PEER_KEYS = 128
PEER_EXPERTS = PEER_KEYS * PEER_KEYS
PEER_TOPK = 16
PEER_QDIM = 256
PEER_HALF = PEER_QDIM // 2
PEER_BLOCK = 256
EPS = 1e-6
IN_SPLITS = (ATT_WIDTH, ATT_WIDTH, ATT_WIDTH, HG_HEADS * HG_DK, HG_HEADS * HG_DK, HG_HEADS * HG_DV, HG_HEADS * HG_DV, D_MODEL, D_MODEL)
IN_WIDTH = sum(IN_SPLITS)
IN_OFFSETS = tuple(int(o) for o in np.cumsum(IN_SPLITS)[:-1])

kernel_name = 'hybrid_dilswa_hgrn2_peer_step'


def rms_norm(x, w):
    xf = x.astype(jnp.float32)
    y = xf * lax.rsqrt(jnp.mean(xf * xf, axis=-1, keepdims=True) + EPS)
    return (y * w.astype(jnp.float32)).astype(x.dtype)


def dilated_window_prompt(q, k, v, window, dilation):
    B, S, H, E = q.shape
    span = window // dilation
    blk = span
    unit = dilation * blk
    s_pad = -(-S // unit) * unit
    nb = s_pad // unit

    def to_blocks(t):
        t = jnp.pad(t, ((0, 0), (0, s_pad - S), (0, 0), (0, 0)))
        t = t.reshape(B, nb * blk, dilation, H, E).transpose(0, 2, 1, 3, 4)
        return t.reshape(B, dilation, nb, blk, H, E)

    def with_prev(t):
        prev = jnp.pad(t, ((0, 0), (0, 0), (1, 0), (0, 0), (0, 0), (0, 0)))[:, :, :-1]
        return jnp.concatenate([prev, t], axis=3)

    qb = to_blocks(q)
    kk = with_prev(to_blocks(k))
    vv = with_prev(to_blocks(v))
    s = jnp.einsum('brnqhe,brnkhe->brnqhk', qb, kk, preferred_element_type=jnp.float32)
    qi = jnp.arange(blk)[:, None]
    ki = jnp.arange(2 * blk)[None, :]
    delta = qi + blk - ki
    band = (delta >= 0) & (delta <= span)
    valid = band[None] & ((jnp.arange(nb)[:, None, None] > 0) | (ki[None] >= blk))
    s = jnp.where(valid[None, None, :, :, None, :], s, -jnp.inf)
    m = jnp.max(s, axis=-1, keepdims=True)
    p = jnp.exp(s - m)
    l = jnp.sum(p, axis=-1)
    o = jnp.einsum('brnqhk,brnkhe->brnqhe', p, vv, preferred_element_type=jnp.float32) / l[..., None]
    lse = m[..., 0] + jnp.log(l)
    o = o.reshape(B, dilation, nb * blk, H, E).transpose(0, 2, 1, 3, 4).reshape(B, s_pad, H, E)[:, :S]
    lse = lse.reshape(B, dilation, nb * blk, H).transpose(0, 2, 1, 3).reshape(B, s_pad, H)[:, :S]
    return o, lse


def dilated_window_decode(cache_kv, q, k, v, window, dilation):
    L = cache_kv.shape[1]
    T = q.shape[1]
    keys = jnp.concatenate([cache_kv[:, :, 0].astype(k.dtype), k], axis=1)
    vals = jnp.concatenate([cache_kv[:, :, 1].astype(v.dtype), v], axis=1)
    n_taps = window // dilation + 1
    idx = L + jnp.arange(T)[:, None] - dilation * jnp.arange(n_taps)[None, :]
    valid = idx >= 0
    idx = jnp.maximum(idx, 0)
    kg = keys[:, idx]
    vg = vals[:, idx]
    s = jnp.einsum('bthe,btmhe->bthm', q, kg, preferred_element_type=jnp.float32)
    s = jnp.where(valid[None, :, None, :], s, -jnp.inf)
    m = jnp.max(s, axis=-1, keepdims=True)
    p = jnp.exp(s - m)
    l = jnp.sum(p, axis=-1)
    o = jnp.einsum('bthm,btmhe->bthe', p, vg, preferred_element_type=jnp.float32) / l[..., None]
    return o, m[..., 0] + jnp.log(l)


def merge_groups(outs, lses):
    w = jax.nn.softmax(jnp.stack(lses, 0), axis=0)
    o = jnp.einsum('gbth,gbthe->bthe', w, jnp.stack(outs, 0))
    return o.reshape(o.shape[0], o.shape[1], ATT_OUT)


def attend_prompt(qa, ka, va):
    T = qa.shape[1]
    outs, lses, rows = [], [], []
    for g, (win, dil) in enumerate(ATT_GROUPS):
        o, lse = dilated_window_prompt(qa[:, :, g], ka[:, :, g], va[:, :, g], win, dil)
        outs.append(o)
        lses.append(lse)
        rows.append(jnp.stack([ka[:, :, g], va[:, :, g]], axis=2)[:, max(T - win, 0):])
    return merge_groups(outs, lses), rows


def attend_sample(caches, qa, ka, va):
    outs, lses, rows = [], [], []
    for g, (win, dil) in enumerate(ATT_GROUPS):
        o, lse = dilated_window_decode(caches[g], qa[:, :, g], ka[:, :, g], va[:, :, g], win, dil)
        outs.append(o)
        lses.append(lse)
        rows.append(jnp.stack([ka[:, :, g], va[:, :, g]], axis=2))
    return merge_groups(outs, lses), rows


def hgrn2_chunk(state, q, k, v, logf):
    C = q.shape[1]
    b = jnp.cumsum(logf, axis=1)
    causal = jnp.tril(jnp.ones((C, C), bool))
    diff = b[:, :, None] - b[:, None, :]
    decay = jnp.exp(jnp.where(causal[None, :, :, None, None], diff, -jnp.inf))
    attn = jnp.einsum('bthk,bshk,btshk->bhts', q, k, decay)
    o = jnp.einsum('bthk,bhkv->bthv', q * jnp.exp(b), state) + jnp.einsum('bhts,bshv->bthv', attn, v)
    b_last = b[:, -1]
    new_state = jnp.exp(b_last)[..., None] * state + jnp.einsum('bshk,bshv->bhkv', k * jnp.exp(b_last[:, None] - b), v)
    return new_state, o


def hgrn2_prompt(q, k, v, logf):
    B, S, H, _ = q.shape
    nc = S // HG_CHUNK

    def chunks(t):
        return t.reshape(B, nc, HG_CHUNK, *t.shape[2:]).swapaxes(0, 1)

    s0 = jnp.zeros((B, H, HG_DK, HG_DV), jnp.float32)
    s_fin, o = lax.scan(lambda st, inp: hgrn2_chunk(st, *inp), s0, (chunks(q), chunks(k), chunks(v), chunks(logf)))
    return o.swapaxes(0, 1).reshape(B, S, H, HG_DV), s_fin


def hgrn2_step(state, q, k, v, logf):
    new_state, o = hgrn2_chunk(state.astype(jnp.float32), q, k, v, logf)
    return o, new_state


def peer_ffn(x, w_q, subkeys, u_tab, v_tab):
    N, D = x.shape
    nb = -(-N // PEER_BLOCK)
    xp = jnp.pad(x, ((0, nb * PEER_BLOCK - N), (0, 0))).reshape(nb, PEER_BLOCK, D)

    def block(xb):
        q = (xb @ w_q).reshape(PEER_BLOCK, PEER_HEADS, 2, PEER_HALF)
        s = jnp.einsum('nhpe,pke->nhpk', q, subkeys, preferred_element_type=jnp.float32)
        s1, i1 = lax.top_k(s[:, :, 0], PEER_TOPK)
        s2, i2 = lax.top_k(s[:, :, 1], PEER_TOPK)
        cand = (s1[..., :, None] + s2[..., None, :]).reshape(PEER_BLOCK, PEER_HEADS, PEER_TOPK * PEER_TOPK)
        cidx = (i1[..., :, None] * PEER_KEYS + i2[..., None, :]).reshape(PEER_BLOCK, PEER_HEADS, PEER_TOPK * PEER_TOPK)
        top, pos = lax.top_k(cand, PEER_TOPK)
        idx = jnp.take_along_axis(cidx, pos, axis=-1)
        g = jax.nn.softmax(top, axis=-1)
        act = jax.nn.gelu(jnp.einsum('nd,nhkd->nhk', xb, u_tab[idx], preferred_element_type=jnp.float32))
        return jnp.einsum('nhk,nhkd->nd', (g * act).astype(xb.dtype), v_tab[idx])

    return lax.map(block, xp).reshape(nb * PEER_BLOCK, D)[:N]


def run_layer(x, c, attend, recur, w_ada, b_ada, norm1_w, norm2_w, w_in, q_norm_w, k_norm_w, lb, hg_norm_w, w_br_a, w_br_b, w_o, w_peer_q, peer_subkeys, peer_u, peer_v):
    B, T, D = x.shape
    mods = jax.nn.silu(c) @ w_ada + b_ada
    sh1, sc1, g1, sh2, sc2, g2 = [m[:, None, :] for m in jnp.split(mods, 6, axis=-1)]
    h = rms_norm(x, norm1_w) * (1 + sc1) + sh1
    qa, ka, va, qh, fh, ih, gh, ga, gb = jnp.split(h @ w_in, IN_OFFSETS, axis=-1)
    att_shape = (B, T, N_GROUPS, ATT_HEADS, ATT_HEAD_DIM)
    qa = rms_norm(qa.reshape(att_shape), q_norm_w) * (ATT_HEAD_DIM ** -0.5)
    ka = rms_norm(ka.reshape(att_shape), k_norm_w)
    va = va.reshape(att_shape)
    att, kv_rows = attend(qa, ka, va)
    att = att.astype(x.dtype)
    f = lb + (1 - lb) * jax.nn.sigmoid(fh.astype(jnp.float32))
    hg_k_shape = (B, T, HG_HEADS, HG_DK)
    q_hg = jax.nn.silu(qh.astype(jnp.float32)).reshape(hg_k_shape)
    k_hg = (1 - f).reshape(hg_k_shape)
    logf = jnp.log(f).reshape(hg_k_shape)
    v_hg = ih.astype(jnp.float32).reshape(B, T, HG_HEADS, HG_DV)
    o_hg, hg_state = recur(q_hg, k_hg, v_hg, logf)
    o_hg = rms_norm(o_hg, hg_norm_w) * jax.nn.silu(gh.astype(jnp.float32).reshape(B, T, HG_HEADS, HG_DV))
    hg = o_hg.reshape(B, T, HG_HEADS * HG_DV).astype(x.dtype)
    y = jax.nn.sigmoid(ga) * (att @ w_br_a) + jax.nn.sigmoid(gb) * (hg @ w_br_b)
    x = x + g1 * (y @ w_o)
    h2 = rms_norm(x, norm2_w) * (1 + sc2) + sh2
    x = x + g2 * peer_ffn(h2.reshape(B * T, D), w_peer_q, peer_subkeys, peer_u, peer_v).reshape(B, T, D)
    return x, kv_rows, hg_state


def setup_inputs(seed: int = 0) -> dict:
    key = jax.random.key(seed)
    ks = iter(jax.random.split(key, 32))

    def nrm(shape, scale=1.0):
        return jax.random.normal(next(ks), shape, jnp.float32) * scale

    def gain(shape):
        return 1.0 + 0.02 * jax.random.normal(next(ks), shape, jnp.float32)

    kv_tail = (2, ATT_HEADS, ATT_HEAD_DIM)
    return {
        'x_prompt': nrm((BATCH, SEQ, D_MODEL)),
        'x_sample': nrm((DEC_BATCH, DEC_SEQ, D_MODEL)),
        'cache_kv_w128': nrm((DEPTH, DEC_BATCH, min(ATT_GROUPS[0][0], PAST_LEN)) + kv_tail),
        'cache_kv_w512': nrm((DEPTH, DEC_BATCH, min(ATT_GROUPS[1][0], PAST_LEN)) + kv_tail),
        'cache_kv_w2048': nrm((DEPTH, DEC_BATCH, min(ATT_GROUPS[2][0], PAST_LEN)) + kv_tail),
        'state_hgrn': nrm((DEPTH, DEC_BATCH, HG_HEADS, HG_DK, HG_DV), 0.5),
        'c_prompt': nrm((BATCH, D_MODEL)),
        'c_sample': nrm((DEC_BATCH, D_MODEL)),
        'w_ada': nrm((DEPTH, D_MODEL, 6 * D_MODEL), 0.5 * D_MODEL ** -0.5),
        'b_ada': nrm((DEPTH, 6 * D_MODEL), 0.02),
        'norm1_w': gain((DEPTH, D_MODEL)),
        'norm2_w': gain((DEPTH, D_MODEL)),
        'w_in': nrm((DEPTH, D_MODEL, IN_WIDTH), D_MODEL ** -0.5),
        'q_norm_w': gain((DEPTH, ATT_HEAD_DIM)),
        'k_norm_w': gain((DEPTH, ATT_HEAD_DIM)),
        'hg_lb_logits': nrm((DEPTH + 1, HG_HEADS * HG_DK)),
        'hg_norm_w': gain((DEPTH, HG_DV)),
        'w_br_a': nrm((DEPTH, ATT_OUT, D_MODEL), ATT_OUT ** -0.5),
        'w_br_b': nrm((DEPTH, HG_HEADS * HG_DV, D_MODEL), (HG_HEADS * HG_DV) ** -0.5),
        'w_o': nrm((DEPTH, D_MODEL, D_MODEL), D_MODEL ** -0.5),
        'w_peer_q': nrm((DEPTH, D_MODEL, PEER_HEADS * PEER_QDIM), D_MODEL ** -0.5),
        'peer_subkeys': nrm((DEPTH, 2, PEER_KEYS, PEER_HALF), PEER_HALF ** -0.5),
        'peer_u': nrm((DEPTH, PEER_EXPERTS, D_MODEL), D_MODEL ** -0.5),
        'peer_v': nrm((DEPTH, PEER_EXPERTS, D_MODEL), PEER_HEADS ** -0.5),
    }


def reference(x_prompt, x_sample, cache_kv_w128, cache_kv_w512, cache_kv_w2048, state_hgrn, c_prompt, c_sample, w_ada, b_ada, norm1_w, norm2_w, w_in, q_norm_w, k_norm_w, hg_lb_logits, hg_norm_w, w_br_a, w_br_b, w_o, w_peer_q, peer_subkeys, peer_u, peer_v):
    lower_bounds = jnp.cumsum(jax.nn.softmax(hg_lb_logits.astype(jnp.float32), axis=0), axis=0)
    y_prompt, y_sample = x_prompt, x_sample
    kv_p = [[] for _ in range(N_GROUPS)]
    kv_s = [[] for _ in range(N_GROUPS)]
    hg_p, hg_s = [], []
    for layer in range(DEPTH):
        weights = (w_ada[layer], b_ada[layer], norm1_w[layer], norm2_w[layer], w_in[layer], q_norm_w[layer], k_norm_w[layer], lower_bounds[layer], hg_norm_w[layer], w_br_a[layer], w_br_b[layer], w_o[layer], w_peer_q[layer], peer_subkeys[layer], peer_u[layer], peer_v[layer])
        y_prompt, rows_p, st_p = run_layer(y_prompt, c_prompt, attend_prompt, hgrn2_prompt, *weights)
        caches = (cache_kv_w128[layer], cache_kv_w512[layer], cache_kv_w2048[layer])
        y_sample, rows_s, st_s = run_layer(y_sample, c_sample, functools.partial(attend_sample, caches), functools.partial(hgrn2_step, state_hgrn[layer]), *weights)
        for g in range(N_GROUPS):
            kv_p[g].append(rows_p[g])
            kv_s[g].append(rows_s[g])
        hg_p.append(st_p)
        hg_s.append(st_s)
    kv128_p = jnp.stack(kv_p[0])
    kv512_p = jnp.stack(kv_p[1])
    kv2048_p = jnp.stack(kv_p[2])
    hgrn_p = jnp.stack(hg_p)
    kv128_s = jnp.stack(kv_s[0])
    kv512_s = jnp.stack(kv_s[1])
    kv2048_s = jnp.stack(kv_s[2])
    hgrn_s = jnp.stack(hg_s)
    return (y_prompt, y_sample, kv128_p, kv512_p, kv2048_p, hgrn_p, kv128_s, kv512_s, kv2048_s, hgrn_s)
```

```python
import functools

import jax
import jax.numpy as jnp
from jax import lax
from jax.experimental import pallas as pl
from jax.experimental.pallas import tpu as pltpu

F32 = jnp.float32
BF16 = jnp.bfloat16

D_MODEL = 1024
ATT_GROUPS = ((128, 1), (512, 4), (2048, 16))
N_GROUPS = 3
ATT_HEADS = 8
ATT_HEAD_DIM = 64
ATT_OUT = ATT_HEADS * ATT_HEAD_DIM
SPAN = 128
HG_HEADS = 8
HG_DK = 128
HG_DV = 128
PEER_HEADS = 8
PEER_KEYS = 128
PEER_TOPK = 16
PEER_HALF = 128
EPS = 1e-6
IN_WIDTH = 10752
COL = 512
NCOL = IN_WIDTH // COL
CB_Q, CB_K, CB_V = 0, 3, 6
CB_QH, CB_FH, CB_IH, CB_GH, CB_GA, CB_GB = 9, 11, 13, 15, 17, 19
HG_CHUNK = 16
W_PITCH = 136
VMEM_LIMIT = 56 * 1024 * 1024

NEG_INF = float("-inf")


def _silu(x):
    return x * jax.nn.sigmoid(x)


def _gelu_tanh(x):
    return 0.5 * x * (1.0 + jnp.tanh(0.7978845608028654 * (x + 0.044715 * (x * x * x))))


def _nt_dot(a, b):
    return lax.dot_general(a, b, (((1,), (1,)), ((), ())), preferred_element_type=F32)


def _params(*sem):
    return pltpu.CompilerParams(dimension_semantics=sem, vmem_limit_bytes=VMEM_LIMIT)


def _mods_kernel(c_ref, w_ref, b_ref, o_ref):
    s = _silu(c_ref[...])
    o_ref[...] = jnp.dot(s.astype(BF16), w_ref[...], preferred_element_type=F32) + b_ref[...]


def _mods(c, w_ada_bf, b_ada):
    n = c.shape[0]
    return pl.pallas_call(
        _mods_kernel,
        grid=(6,),
        in_specs=[pl.BlockSpec((n, D_MODEL), lambda j: (0, 0)),
                  pl.BlockSpec((D_MODEL, D_MODEL), lambda j: (0, j)),
                  pl.BlockSpec((1, D_MODEL), lambda j: (0, j))],
        out_specs=pl.BlockSpec((n, D_MODEL), lambda j: (0, j)),
        out_shape=jax.ShapeDtypeStruct((n, 6 * D_MODEL), F32),
        compiler_params=_params("arbitrary"),
        name="mods",
    )(c, w_ada_bf, b_ada)


def _mod_spec(per_row, tm, rows_per_batch, k):
    if per_row:
        return pl.BlockSpec((tm, D_MODEL), lambda i, *_: (i, k))
    tiles = rows_per_batch // tm
    return pl.BlockSpec((None, 1, D_MODEL), lambda i, *_: (i // tiles, 0, k))


def _inproj_kernel(x_ref, sc_ref, sh_ref, n1_ref, w_ref, qn_ref, kn_ref, seg_ref, o_ref, h_scr):
    j = pl.program_id(1)

    @pl.when(j == 0)
    def _():
        x = x_ref[...]
        ms = jnp.mean(x * x, axis=-1, keepdims=True)
        xn = x * lax.rsqrt(ms + EPS) * n1_ref[...]
        h_scr[...] = (xn * (1.0 + sc_ref[...]) + sh_ref[...]).astype(BF16)

    acc = jnp.dot(h_scr[...], w_ref[...], preferred_element_type=F32)

    def head_norm(w_row, scale):
        ss = jnp.dot((acc * acc).astype(BF16), seg_ref[...], preferred_element_type=F32)
        return acc * lax.rsqrt(ss * (1.0 / ATT_HEAD_DIM) + EPS) * w_row * scale

    @pl.when(j < CB_K)
    def _():
        o_ref[...] = head_norm(qn_ref[...], ATT_HEAD_DIM ** -0.5)

    @pl.when((j >= CB_K) & (j < CB_V))
    def _():
        o_ref[...] = head_norm(kn_ref[...], 1.0)

    @pl.when(j >= CB_V)
    def _():
        o_ref[...] = acc


def _inproj(x2, mods, per_row, rows_per_batch, tm, n1, w_in_bf, qn, kn, seg):
    n = x2.shape[0]
    const = lambda i, j: (0, 0)
    return pl.pallas_call(
        _inproj_kernel,
        grid=(n // tm, NCOL),
        in_specs=[pl.BlockSpec((tm, D_MODEL), lambda i, j: (i, 0)),
                  _mod_spec(per_row, tm, rows_per_batch, 1),
                  _mod_spec(per_row, tm, rows_per_batch, 0),
                  pl.BlockSpec((1, D_MODEL), const),
                  pl.BlockSpec((D_MODEL, COL), lambda i, j: (0, j)),
                  pl.BlockSpec((1, COL), const),
                  pl.BlockSpec((1, COL), const),
                  pl.BlockSpec((COL, COL), const)],
        out_specs=pl.BlockSpec((tm, COL), lambda i, j: (i, j)),
        out_shape=jax.ShapeDtypeStruct((n, IN_WIDTH), F32),
        scratch_shapes=[pltpu.VMEM((tm, D_MODEL), BF16)],
        compiler_params=_params("arbitrary", "arbitrary"),
        name="inproj",
    )(x2, mods, mods, n1, w_in_bf, qn, kn, seg)


def _attn_kernel(q_ref, kp_ref, kc_ref, vp_ref, vc_ref, o_ref, lse_ref):
    jb = pl.program_id(2)
    q = q_ref[...].astype(BF16)
    k = jnp.concatenate([kp_ref[...], kc_ref[...]], axis=0).astype(BF16)
    v = jnp.concatenate([vp_ref[...], vc_ref[...]], axis=0).astype(BF16)
    qi = lax.broadcasted_iota(jnp.int32, (SPAN, 2 * SPAN), 0)
    ki = lax.broadcasted_iota(jnp.int32, (SPAN, 2 * SPAN), 1)
    delta = qi + SPAN - ki
    valid = (delta >= 0) & (delta <= SPAN) & ((ki >= SPAN) | (jb > 0))
    outs, lses = [], []
    for h in range(ATT_HEADS):
        sl = slice(h * ATT_HEAD_DIM, (h + 1) * ATT_HEAD_DIM)
        s = jnp.where(valid, _nt_dot(q[:, sl], k[:, sl]), NEG_INF)
        m = jnp.max(s, axis=-1, keepdims=True)
        p = jnp.exp(s - m)
        l = jnp.sum(p, axis=-1, keepdims=True)
        o = jnp.dot(p.astype(BF16), v[:, sl], preferred_element_type=F32) / l
        outs.append(o)
        lses.append(jnp.broadcast_to(m + jnp.log(l), (SPAN, ATT_HEAD_DIM)))
    o_ref[...] = jnp.concatenate(outs, axis=-1)
    lse_ref[...] = jnp.concatenate(lses, axis=-1)


def _attn_group(proj3, g, dil):
    b, s, _ = proj3.shape
    sj = s // dil
    nb = sj // SPAN
    pv = proj3.reshape(b, sj, dil * IN_WIDTH)

    def spec(cb, prev):
        if prev:
            return pl.BlockSpec((None, SPAN, COL), lambda bi, r, jb: (bi, jnp.maximum(jb - 1, 0), r * NCOL + cb + g))
        return pl.BlockSpec((None, SPAN, COL), lambda bi, r, jb: (bi, jb, r * NCOL + cb + g))

    out_spec = pl.BlockSpec((None, SPAN, COL), lambda bi, r, jb: (bi, jb, r))
    o, lse = pl.pallas_call(
        _attn_kernel,
        grid=(b, dil, nb),
        in_specs=[spec(CB_Q, False), spec(CB_K, True), spec(CB_K, False), spec(CB_V, True), spec(CB_V, False)],
        out_specs=[out_spec, out_spec],
        out_shape=[jax.ShapeDtypeStruct((b, sj, dil * ATT_OUT), F32)] * 2,
        compiler_params=_params("arbitrary", "arbitrary", "arbitrary"),
        name=f"attn_g{g}",
    )(pv, pv, pv, pv, pv)
    return o.reshape(b * s, ATT_OUT), lse.reshape(b * s, ATT_OUT)


def _lower_bound(logits):
    mx = jnp.max(logits, axis=0, keepdims=True)
    e = jnp.exp(logits - mx)
    return e[0:1] / jnp.sum(e, axis=0, keepdims=True)


def _hgrn_kernel(qh_ref, fh_ref, ih_ref, gh_ref, lbl_ref, hn_ref, tri_ref, o_ref, st_ref,
                 st_scr, q_scr, k_scr, b_scr, *, tb):
    t = pl.program_id(2)
    nh = COL // HG_DK
    c_rows = HG_CHUNK

    @pl.when(t == 0)
    def _():
        st_scr[...] = jnp.zeros_like(st_scr)

    lb = _lower_bound(lbl_ref[...])
    f = lb + (1.0 - lb) * jax.nn.sigmoid(fh_ref[...])
    k_scr[...] = 1.0 - f
    q_scr[...] = _silu(qh_ref[...])
    logf = jnp.log(f)
    for r in range(tb // 128):
        rows = slice(r * 128, (r + 1) * 128)
        b_scr[rows, :] = jnp.dot(tri_ref[...], logf[rows, :], precision=lax.Precision.HIGHEST,
                                 preferred_element_type=F32)
    rowid = lax.broadcasted_iota(jnp.int32, (c_rows, 1), 0)

    def chunk(c, carry):
        r0 = pl.multiple_of(c * c_rows, c_rows)
        rows = pl.ds(r0, c_rows)
        for h in range(nh):
            cols = slice(h * HG_DK, (h + 1) * HG_DK)
            b = b_scr[rows, cols]
            qc = q_scr[rows, cols]
            kc = k_scr[rows, cols]
            vc = ih_ref[rows, cols]
            st = st_scr[h]
            o = _nt_dot((qc * jnp.exp(b)).astype(BF16), st.astype(BF16))
            for s in range(c_rows):
                e = jnp.exp(jnp.minimum(b - b[s:s + 1], 0.0))
                a = jnp.sum(qc * kc[s:s + 1] * e, axis=-1, keepdims=True)
                o = o + jnp.where(rowid >= s, a, 0.0) * vc[s:s + 1]
            bl = b[c_rows - 1:c_rows]
            kt = kc * jnp.exp(bl - b)
            upd = lax.dot_general(vc.astype(BF16), kt.astype(BF16), (((0,), (0,)), ((), ())),
                                  preferred_element_type=F32)
            st_scr[h] = st * jnp.exp(bl) + upd
            ms = jnp.mean(o * o, axis=-1, keepdims=True)
            on = o * lax.rsqrt(ms + EPS) * hn_ref[...]
            o_ref[rows, cols] = on * _silu(gh_ref[rows, cols])
        return carry

    lax.fori_loop(0, tb // c_rows, chunk, 0)

    @pl.when(t == pl.num_programs(2) - 1)
    def _():
        for h in range(nh):
            st_ref[h] = st_scr[h].T


def _hgrn_prompt(proj3, lb_logits, hn, tri, tb):
    b, s, _ = proj3.shape
    nhb = HG_HEADS * HG_DK // COL
    nh = COL // HG_DK

    def spec(cb):
        return pl.BlockSpec((None, tb, COL), lambda bi, hb, t: (bi, t, cb + hb))

    o, st = pl.pallas_call(
        functools.partial(_hgrn_kernel, tb=tb),
        grid=(b, nhb, s // tb),
        in_specs=[spec(CB_QH), spec(CB_FH), spec(CB_IH), spec(CB_GH),
                  pl.BlockSpec((lb_logits.shape[0], COL), lambda bi, hb, t: (0, hb)),
                  pl.BlockSpec((1, HG_DV), lambda bi, hb, t: (0, 0)),
                  pl.BlockSpec((128, 128), lambda bi, hb, t: (0, 0))],
        out_specs=[pl.BlockSpec((None, tb, COL), lambda bi, hb, t: (bi, t, hb)),
                   pl.BlockSpec((None, nh, HG_DK, HG_DV), lambda bi, hb, t: (bi, hb, 0, 0))],
        out_shape=[jax.ShapeDtypeStruct((b, s, HG_HEADS * HG_DV), F32),
                   jax.ShapeDtypeStruct((b, HG_HEADS, HG_DK, HG_DV), F32)],
        scratch_shapes=[pltpu.VMEM((nh, HG_DV, HG_DK), F32),
                        pltpu.VMEM((tb, COL), F32), pltpu.VMEM((tb, COL), F32), pltpu.VMEM((tb, COL), F32)],
        compiler_params=_params("arbitrary", "arbitrary", "arbitrary"),
        name="hgrn_prompt",
    )(proj3, proj3, proj3, proj3, lb_logits, hn, tri)
    return o.reshape(b * s, HG_HEADS * HG_DV), st


def _mix_kernel(*refs, n_att):
    n_refs = 1 if n_att == 1 else 2 * n_att
    att_refs = refs[:n_refs]
    (hg_ref, ga0, ga1, gb0, gb1, x_ref, g1_ref, sc2_ref, sh2_ref, n2_ref,
     wa_ref, wb_ref, wo_ref, x1_ref, h2_ref) = refs[n_refs:]
    if n_att == 1:
        att = att_refs[0][...]
    else:
        lses = [att_refs[2 * g + 1][...] for g in range(n_att)]
        mx = functools.reduce(jnp.maximum, lses)
        es = [jnp.exp(l - mx) for l in lses]
        den = functools.reduce(lambda a, b: a + b, es)
        num = functools.reduce(lambda a, b: a + b, [es[g] * att_refs[2 * g][...] for g in range(n_att)])
        att = num / den
    ga = jnp.concatenate([ga0[...], ga1[...]], axis=-1)
    gb = jnp.concatenate([gb0[...], gb1[...]], axis=-1)
    ya = jnp.dot(att.astype(BF16), wa_ref[...], preferred_element_type=F32)
    yb = jnp.dot(hg_ref[...].astype(BF16), wb_ref[...], preferred_element_type=F32)
    y = jax.nn.sigmoid(ga) * ya + jax.nn.sigmoid(gb) * yb
    x1 = x_ref[...] + g1_ref[...] * jnp.dot(y.astype(BF16), wo_ref[...], preferred_element_type=F32)
    x1_ref[...] = x1
    ms = jnp.mean(x1 * x1, axis=-1, keepdims=True)
    xn = x1 * lax.rsqrt(ms + EPS) * n2_ref[...]
    h2_ref[...] = (xn * (1.0 + sc2_ref[...]) + sh2_ref[...]).astype(BF16)


def _mix(att_list, hg, proj2, x2, mods, per_row, rows_per_batch, tm, n2, wa, wb, wo):
    n = x2.shape[0]
    row = lambda w: pl.BlockSpec((tm, w), lambda i: (i, 0))
    colblk = lambda cb: pl.BlockSpec((tm, COL), lambda i: (i, cb))
    full = lambda a: pl.BlockSpec(a.shape, lambda i: (0, 0))
    in_specs = ([row(ATT_OUT)] * len(att_list)
                + [row(D_MODEL), colblk(CB_GA), colblk(CB_GA + 1), colblk(CB_GB), colblk(CB_GB + 1), row(D_MODEL),
                   _mod_spec(per_row, tm, rows_per_batch, 2), _mod_spec(per_row, tm, rows_per_batch, 4),
                   _mod_spec(per_row, tm, rows_per_batch, 3), full(n2), full(wa), full(wb), full(wo)])
    n_att = 1 if len(att_list) == 1 else len(att_list) // 2
    return pl.pallas_call(
        functools.partial(_mix_kernel, n_att=n_att),
        grid=(n // tm,),
        in_specs=in_specs,
        out_specs=[row(D_MODEL), row(D_MODEL)],
        out_shape=[jax.ShapeDtypeStruct((n, D_MODEL), F32), jax.ShapeDtypeStruct((n, D_MODEL), BF16)],
        compiler_params=_params("arbitrary"),
        name="mix",
    )(*att_list, hg, proj2, proj2, proj2, proj2, x2, mods, mods, mods, n2, wa, wb, wo)


ROUTE_ROWS = 32


def _route_kernel(h2_ref, wq_ref, sk_ref, a_ref, b_ref, g_ref, s_scr):
    tm = h2_ref.shape[0]
    rb = min(ROUTE_ROWS, tm)
    q = jnp.dot(h2_ref[...], wq_ref[...], preferred_element_type=F32)
    for hp in range(2 * PEER_HEADS):
        cols = slice(hp * PEER_HALF, (hp + 1) * PEER_HALF)
        s_scr[:, cols] = _nt_dot(q[:, cols].astype(BF16), sk_ref[hp % 2])
    lane = lax.broadcasted_iota(jnp.int32, (rb, 128), 1)
    p_lo = lane >> 4
    q_of = lane & 15

    def top16(s):
        vals = jnp.zeros((rb, 128), F32)
        idxs = jnp.zeros((rb, 128), jnp.int32)
        for r in range(PEER_TOPK):
            m = jnp.max(s, axis=-1, keepdims=True)
            am = jnp.min(jnp.where(s == m, lane, 128), axis=-1, keepdims=True)
            vals = jnp.where(lane == r, m, vals)
            idxs = jnp.where(lane == r, am, idxs)
            s = jnp.where(lane == am, NEG_INF, s)
        return vals, idxs

    def block(rblk, carry):
        rows = pl.ds(pl.multiple_of(rblk * rb, rb), rb)
        a_out = jnp.zeros((rb, 128), jnp.int32)
        b_out = jnp.zeros((rb, 128), jnp.int32)
        g_out = jnp.zeros((rb, 128), F32)
        for h in range(PEER_HEADS):
            s1, i1 = top16(s_scr[rows, (2 * h) * PEER_HALF:(2 * h + 1) * PEER_HALF])
            s2, i2 = top16(s_scr[rows, (2 * h + 1) * PEER_HALF:(2 * h + 2) * PEER_HALF])
            s2e = jnp.take_along_axis(s2, q_of, axis=1)
            c_lo = jnp.take_along_axis(s1, p_lo, axis=1) + s2e
            c_hi = jnp.take_along_axis(s1, p_lo + 8, axis=1) + s2e
            tv = jnp.full((rb, 128), NEG_INF, F32)
            tp = jnp.zeros((rb, 128), jnp.int32)
            m0 = None
            for r in range(PEER_TOPK):
                m = jnp.maximum(jnp.max(c_lo, axis=-1, keepdims=True), jnp.max(c_hi, axis=-1, keepdims=True))
                am = jnp.minimum(jnp.min(jnp.where(c_lo == m, lane, 256), axis=-1, keepdims=True),
                                 jnp.min(jnp.where(c_hi == m, lane + 128, 256), axis=-1, keepdims=True))
                if r == 0:
                    m0 = m
                slot = lane == (h * PEER_TOPK + r)
                tv = jnp.where(slot, m, tv)
                tp = jnp.where(slot, am, tp)
                c_lo = jnp.where(lane == am, NEG_INF, c_lo)
                c_hi = jnp.where(lane + 128 == am, NEG_INF, c_hi)
            mine = (lane >= h * PEER_TOPK) & (lane < (h + 1) * PEER_TOPK)
            e = jnp.exp(tv - m0)
            g = e / jnp.sum(e, axis=-1, keepdims=True)
            a_sel = jnp.take_along_axis(i1, tp >> 4, axis=1)
            b_sel = jnp.take_along_axis(i2, tp & 15, axis=1)
            a_out = jnp.where(mine, a_sel, a_out)
            b_out = jnp.where(mine, b_sel, b_out)
            g_out = jnp.where(mine, g, g_out)
        a_ref[rows, :] = a_out
        b_ref[rows, :] = b_out
        g_ref[rows, :] = g_out
        return carry

    lax.fori_loop(0, tm // rb, block, 0)


def _route(h2, wq_bf, sk_bf, tm):
    n = h2.shape[0]
    row = pl.BlockSpec((tm, 128), lambda i: (i, 0))
    return pl.pallas_call(
        _route_kernel,
        grid=(n // tm,),
        in_specs=[pl.BlockSpec((tm, D_MODEL), lambda i: (i, 0)),
                  pl.BlockSpec(wq_bf.shape, lambda i: (0, 0)),
                  pl.BlockSpec(sk_bf.shape, lambda i: (0, 0, 0))],
        out_specs=[row, row, row],
        out_shape=[jax.ShapeDtypeStruct((n, 128), jnp.int32), jax.ShapeDtypeStruct((n, 128), jnp.int32),
                   jax.ShapeDtypeStruct((n, 128), F32)],
        scratch_shapes=[pltpu.VMEM((tm, 2 * PEER_HEADS * PEER_HALF), F32)],
        compiler_params=_params("arbitrary"),
        name="peer_route",
    )(h2, wq_bf, sk_bf)


def _peer_u_kernel(h2_ref, u_ref, a_ref, b_ref, g_ref, w_ref, act_scr, *, ac):
    c = pl.program_id(1)

    @pl.when(c == 0)
    def _():
        act_scr[...] = jnp.zeros_like(act_scr)

    hc = _nt_dot(h2_ref[...], u_ref[...])
    a_idx = a_ref[...]
    b_idx = b_ref[...]
    act = act_scr[...]
    for i in range(ac):
        gathered = jnp.take_along_axis(hc[:, i * 128:(i + 1) * 128], b_idx, axis=1)
        act = jnp.where(a_idx == c * ac + i, gathered, act)
    act_scr[...] = act

    @pl.when(c == pl.num_programs(1) - 1)
    def _():
        w_ref[...] = g_ref[...] * _gelu_tanh(act)


def _peer_u(h2, u_bf, a_idx, b_idx, gate, tm, ac):
    n = h2.shape[0]
    row = pl.BlockSpec((tm, 128), lambda i, c: (i, 0))
    return pl.pallas_call(
        functools.partial(_peer_u_kernel, ac=ac),
        grid=(n // tm, PEER_KEYS // ac),
        in_specs=[pl.BlockSpec((tm, D_MODEL), lambda i, c: (i, 0)),
                  pl.BlockSpec((ac * 128, D_MODEL), lambda i, c: (c, 0)),
                  row, row, row],
        out_specs=row,
        out_shape=jax.ShapeDtypeStruct((n, 128), F32),
        scratch_shapes=[pltpu.VMEM((tm, 128), F32)],
        compiler_params=_params("arbitrary", "arbitrary"),
        name="peer_u",
    )(h2, u_bf, a_idx, b_idx, gate)


def _peer_v_kernel(a_ref, b_ref, w_ref, v_ref, x1_ref, g2_ref, o_ref, w3_scr, acc_scr, *, ac, tm):
    c = pl.program_id(1)

    @pl.when(c == 0)
    def _():
        sub = lax.broadcasted_iota(jnp.int32, (128, 128), 0)

        def build(n, carry):
            ar = a_ref[pl.ds(n, 1), :]
            br = b_ref[pl.ds(n, 1), :]
            wr = w_ref[pl.ds(n, 1), :]
            at = jnp.where(sub == ar, 1.0, 0.0).astype(BF16)
            rt = jnp.where(sub == br, wr, 0.0).astype(BF16)
            w3_scr[pl.ds(pl.multiple_of(n * W_PITCH, 8), 128), :] = _nt_dot(at, rt)
            return carry

        lax.fori_loop(0, tm, build, 0)
        acc_scr[...] = jnp.zeros_like(acc_scr)

    acc = acc_scr[...]
    for i in range(ac):
        lhs = w3_scr[pl.ds(c * ac + i, tm, stride=W_PITCH), :]
        acc = acc + jnp.dot(lhs.astype(BF16), v_ref[i * 128:(i + 1) * 128, :], preferred_element_type=F32)
    acc_scr[...] = acc

    @pl.when(c == pl.num_programs(1) - 1)
    def _():
        o_ref[...] = x1_ref[...] + g2_ref[...] * acc


def _peer_v(a_idx, b_idx, wts, v_bf, x1, mods, per_row, rows_per_batch, tm, ac):
    n = x1.shape[0]
    row = pl.BlockSpec((tm, 128), lambda i, c: (i, 0))
    wide = pl.BlockSpec((tm, D_MODEL), lambda i, c: (i, 0))
    return pl.pallas_call(
        functools.partial(_peer_v_kernel, ac=ac, tm=tm),
        grid=(n // tm, PEER_KEYS // ac),
        in_specs=[row, row, row,
                  pl.BlockSpec((ac * 128, D_MODEL), lambda i, c: (c, 0)),
                  wide, _mod_spec(per_row, tm, rows_per_batch, 5)],
        out_specs=wide,
        out_shape=jax.ShapeDtypeStruct((n, D_MODEL), F32),
        scratch_shapes=[pltpu.VMEM((tm * W_PITCH, 128), F32), pltpu.VMEM((tm, D_MODEL), F32)],
        compiler_params=_params("arbitrary", "arbitrary"),
        name="peer_v",
    )(a_idx, b_idx, wts, v_bf, x1, mods)


def _decode_attn_kernel(q_ref, k_ref, v_ref, c0_ref, c1_ref, c2_ref, seg_ref, segt_ref, o_ref, *, bt):
    caches = (c0_ref, c1_ref, c2_ref)
    hi = lax.Precision.HIGHEST
    for i in range(bt):
        ms, ls, accs = [], [], []
        for g in range(N_GROUPS):
            cols = slice(g * ATT_OUT, (g + 1) * ATT_OUT)
            q = q_ref[i:i + 1, cols]
            kn = k_ref[i:i + 1, cols]
            vn = v_ref[i:i + 1, cols]
            kt = caches[g][i, :, 0:ATT_OUT]
            vt = caches[g][i, :, ATT_OUT:2 * ATT_OUT]
            s = jnp.dot(kt * q, seg_ref[...], precision=hi, preferred_element_type=F32)
            s0 = jnp.dot(kn * q, seg_ref[...], precision=hi, preferred_element_type=F32)
            m = jnp.maximum(jnp.max(s, axis=0, keepdims=True), s0)
            p = jnp.exp(s - m)
            p0 = jnp.exp(s0 - m)
            l = jnp.sum(p, axis=0, keepdims=True) + p0
            pe = jnp.dot(p, segt_ref[...], precision=hi, preferred_element_type=F32)
            p0e = jnp.dot(p0, segt_ref[...], precision=hi, preferred_element_type=F32)
            acc = jnp.sum(pe * vt, axis=0, keepdims=True) + p0e * vn
            ms.append(jnp.dot(m, segt_ref[...], precision=hi, preferred_element_type=F32))
            ls.append(jnp.dot(l, segt_ref[...], precision=hi, preferred_element_type=F32))
            accs.append(acc)
        lses = [ms[g] + jnp.log(ls[g]) for g in range(N_GROUPS)]
        mx = functools.reduce(jnp.maximum, lses)
        es = [jnp.exp(x - mx) for x in lses]
        den = es[0] + es[1] + es[2]
        num = es[0] * (accs[0] / ls[0]) + es[1] * (accs[1] / ls[1]) + es[2] * (accs[2] / ls[2])
        o_ref[i:i + 1, :] = num / den


def _decode_attn(proj_s, caches, seg8, seg8t, bt):
    n = proj_s.shape[0]
    views, specs = [], []
    for (win, dil), cache in zip(ATT_GROUPS, caches):
        length = cache.shape[1]
        views.append(cache.reshape(n, length // dil, dil * 2 * ATT_OUT))
        specs.append(pl.BlockSpec((bt, length // dil, 2 * ATT_OUT), lambda i: (i, 0, 0)))
    blk = lambda cb: pl.BlockSpec((bt, N_GROUPS * ATT_OUT), lambda i: (i, cb))
    return pl.pallas_call(
        functools.partial(_decode_attn_kernel, bt=bt),
        grid=(n // bt,),
        in_specs=[blk(0), blk(1), blk(2)] + specs
                 + [pl.BlockSpec(seg8.shape, lambda i: (0, 0)), pl.BlockSpec(seg8t.shape, lambda i: (0, 0))],
        out_specs=pl.BlockSpec((bt, ATT_OUT), lambda i: (i, 0)),
        out_shape=jax.ShapeDtypeStruct((n, ATT_OUT), F32),
        compiler_params=_params("arbitrary"),
        name="decode_attn",
    )(proj_s, proj_s, proj_s, *views, seg8, seg8t)


def _decode_hgrn_kernel(q0, q1, f0, f1, i0, i1, g0, g1, lbl_ref, hn_ref, st_ref, o_ref, sto_ref, *, bt):
    nh = COL // HG_DK
    lb = _lower_bound(lbl_ref[...])
    for h in range(HG_HEADS):
        qr, fr, ir, gr = ((q0, f0, i0, g0), (q1, f1, i1, g1))[h // nh]
        cols = slice((h % nh) * HG_DK, (h % nh + 1) * HG_DK)
        lbh = lb[:, h * HG_DK:(h + 1) * HG_DK]
        f = lbh + (1.0 - lbh) * jax.nn.sigmoid(fr[:, cols])
        ft = f.T
        kt = 1.0 - ft
        qt = _silu(qr[:, cols]).T
        for i in range(bt):
            v = ir[i:i + 1, cols]
            s_new = ft[:, i:i + 1] * st_ref[i, h] + kt[:, i:i + 1] * v
            sto_ref[i, h] = s_new
            o = jnp.sum(qt[:, i:i + 1] * s_new, axis=0, keepdims=True)
            ms = jnp.mean(o * o, axis=-1, keepdims=True)
            o_ref[i:i + 1, h * HG_DV:(h + 1) * HG_DV] = (o * lax.rsqrt(ms + EPS) * hn_ref[...]
                                                         * _silu(gr[i:i + 1, cols]))


def _decode_hgrn(proj_s, state, lb_logits, hn, bt):
    n = proj_s.shape[0]
    st_spec = pl.BlockSpec((bt, HG_HEADS, HG_DK, HG_DV), lambda i: (i, 0, 0, 0))
    half = lambda cb: pl.BlockSpec((bt, COL), lambda i: (i, cb))
    return pl.pallas_call(
        functools.partial(_decode_hgrn_kernel, bt=bt),
        grid=(n // bt,),
        in_specs=[half(CB_QH), half(CB_QH + 1), half(CB_FH), half(CB_FH + 1), half(CB_IH), half(CB_IH + 1),
                  half(CB_GH), half(CB_GH + 1),
                  pl.BlockSpec(lb_logits.shape, lambda i: (0, 0)),
                  pl.BlockSpec((1, HG_DV), lambda i: (0, 0)),
                  st_spec],
        out_specs=[pl.BlockSpec((bt, HG_HEADS * HG_DV), lambda i: (i, 0)), st_spec],
        out_shape=[jax.ShapeDtypeStruct((n, HG_HEADS * HG_DV), F32),
                   jax.ShapeDtypeStruct(state.shape, F32)],
        compiler_params=_params("arbitrary"),
        name="decode_hgrn",
    )(proj_s, proj_s, proj_s, proj_s, proj_s, proj_s, proj_s, proj_s, lb_logits, hn, state)


def _block_diag_ones(n, seg):
    i = jnp.arange(n)
    return (i[:, None] // seg == i[None, :] // seg)


def _peer(h2, x1, mods, per_row, rows_per_batch, tm, wq_bf, sk_bf, u_bf, v_bf, ac):
    a_idx, b_idx, gate = _route(h2, wq_bf, sk_bf, tm)
    wts = _peer_u(h2, u_bf, a_idx, b_idx, gate, tm, ac)
    return _peer_v(a_idx, b_idx, wts, v_bf, x1, mods, per_row, rows_per_batch, tm, ac)


def _kv_rows(proj3, g, rows):
    k = proj3[:, -rows:, (CB_K + g) * COL:(CB_K + g + 1) * COL]
    v = proj3[:, -rows:, (CB_V + g) * COL:(CB_V + g + 1) * COL]
    b = proj3.shape[0]
    kv = jnp.stack([k, v], axis=2)
    return kv.reshape(1, b, rows, 2, ATT_HEADS, ATT_HEAD_DIM)


def kernel(x_prompt, x_sample, cache_kv_w128, cache_kv_w512, cache_kv_w2048, state_hgrn, c_prompt, c_sample, w_ada, b_ada, norm1_w, norm2_w, w_in, q_norm_w, k_norm_w, hg_lb_logits, hg_norm_w, w_br_a, w_br_b, w_o, w_peer_q, peer_subkeys, peer_u, peer_v):
    bsz, seq, _ = x_prompt.shape
    dec, dec_t, _ = x_sample.shape
    assert w_ada.shape[0] == 1 and dec_t == 1 and seq % (ATT_GROUPS[-1][1] * SPAN) == 0
    for (win, dil), cache in zip(ATT_GROUPS, (cache_kv_w128, cache_kv_w512, cache_kv_w2048)):
        assert win == dil * SPAN and cache.shape[2] == win

    w_ada_bf = w_ada[0].astype(BF16)
    w_in_bf = w_in[0].astype(BF16)
    wa, wb, wo = w_br_a[0].astype(BF16), w_br_b[0].astype(BF16), w_o[0].astype(BF16)
    wq_bf = w_peer_q[0].astype(BF16)
    sk_bf = peer_subkeys[0].astype(BF16)
    u_bf = peer_u[0].astype(BF16)
    v_bf = peer_v[0].astype(BF16)
    n1 = norm1_w[0].reshape(1, D_MODEL)
    n2 = norm2_w[0].reshape(1, D_MODEL)
    qn = jnp.tile(q_norm_w[0], ATT_HEADS).reshape(1, COL)
    kn = jnp.tile(k_norm_w[0], ATT_HEADS).reshape(1, COL)
    hn = hg_norm_w[0].reshape(1, HG_DV)
    seg = _block_diag_ones(COL, ATT_HEAD_DIM).astype(BF16)
    tri = (_block_diag_ones(128, HG_CHUNK) & (jnp.arange(128)[:, None] >= jnp.arange(128)[None, :])).astype(F32)
    head_of_col = jnp.arange(ATT_OUT)[:, None] // ATT_HEAD_DIM
    seg8 = (head_of_col == jnp.arange(128)[None, :]).astype(F32)
    seg8t = seg8.T

    mods = _mods(jnp.concatenate([c_prompt, c_sample], axis=0), w_ada_bf, b_ada)
    mods_p = mods[:bsz].reshape(bsz, 1, 6 * D_MODEL)
    mods_s = mods[bsz:]

    n_p = bsz * seq
    tm_p = 512
    ac = 8
    xp2 = x_prompt.reshape(n_p, D_MODEL)
    proj_p = _inproj(xp2, mods_p, False, seq, 1024, n1, w_in_bf, qn, kn, seg)
    proj_p3 = proj_p.reshape(bsz, seq, IN_WIDTH)
    att_list = []
    for g, (win, dil) in enumerate(ATT_GROUPS):
        att_list.extend(_attn_group(proj_p3, g, dil))
    hg_p, st_p = _hgrn_prompt(proj_p3, hg_lb_logits, hn, tri, 256)
    x1_p, h2_p = _mix(att_list, hg_p, proj_p, xp2, mods_p, False, seq, tm_p, n2, wa, wb, wo)
    y_p = _peer(h2_p, x1_p, mods_p, False, seq, 256, wq_bf, sk_bf, u_bf, v_bf, ac)

    xs2 = x_sample.reshape(dec, D_MODEL)
    proj_s = _inproj(xs2, mods_s, True, 1, dec, n1, w_in_bf, qn, kn, seg)
    att_s = _decode_attn(proj_s, (cache_kv_w128[0], cache_kv_w512[0], cache_kv_w2048[0]), seg8, seg8t, 8)
    hg_s, st_s = _decode_hgrn(proj_s, state_hgrn[0], hg_lb_logits, hn, 8)
    x1_s, h2_s = _mix([att_s], hg_s, proj_s, xs2, mods_s, True, 1, dec, n2, wa, wb, wo)
    y_s = _peer(h2_s, x1_s, mods_s, True, 1, dec, wq_bf, sk_bf, u_bf, v_bf, ac)

    proj_s3 = proj_s.reshape(dec, 1, IN_WIDTH)
    return (y_p.reshape(bsz, seq, D_MODEL), y_s.reshape(dec, 1, D_MODEL),
            _kv_rows(proj_p3, 0, min(ATT_GROUPS[0][0], seq)), _kv_rows(proj_p3, 1, min(ATT_GROUPS[1][0], seq)),
            _kv_rows(proj_p3, 2, min(ATT_GROUPS[2][0], seq)), st_p[None],
            _kv_rows(proj_s3, 0, 1), _kv_rows(proj_s3, 1, 1), _kv_rows(proj_s3, 2, 1), st_s[None])
```

```python
import functools

import jax
import jax.numpy as jnp
from jax import lax
from jax.experimental import pallas as pl
from jax.experimental.pallas import tpu as pltpu

F32 = jnp.float32
BF16 = jnp.bfloat16

D_MODEL = 1024
ATT_GROUPS = ((128, 1), (512, 4), (2048, 16))
N_GROUPS = 3
ATT_HEADS = 8
ATT_HEAD_DIM = 64
ATT_OUT = ATT_HEADS * ATT_HEAD_DIM
SPAN = 128
HG_HEADS = 8
HG_DK = 128
HG_DV = 128
PEER_HEADS = 8
PEER_KEYS = 128
PEER_TOPK = 16
PEER_HALF = 128
EPS = 1e-6
IN_WIDTH = 10752
COL = 512
NCOL = IN_WIDTH // COL
CB_Q, CB_K, CB_V = 0, 3, 6
CB_QH, CB_FH, CB_IH, CB_GH, CB_GA, CB_GB = 9, 11, 13, 15, 17, 19
HG_CHUNK = 16
W_PITCH = 136
VMEM_LIMIT = 56 * 1024 * 1024

NEG_INF = float("-inf")


def _silu(x):
    return x * jax.nn.sigmoid(x)


def _gelu_tanh(x):
    return 0.5 * x * (1.0 + jnp.tanh(0.7978845608028654 * (x + 0.044715 * (x * x * x))))


def _nt_dot(a, b):
    return lax.dot_general(a, b, (((1,), (1,)), ((), ())), preferred_element_type=F32)


def _params(*sem):
    return pltpu.CompilerParams(dimension_semantics=sem, vmem_limit_bytes=VMEM_LIMIT)


def _mods_kernel(c_ref, w_ref, b_ref, o_ref):
    s = _silu(c_ref[...])
    o_ref[...] = jnp.dot(s.astype(BF16), w_ref[...], preferred_element_type=F32) + b_ref[...]


def _mods(c, w_ada_bf, b_ada):
    n = c.shape[0]
    return pl.pallas_call(
        _mods_kernel,
        grid=(6,),
        in_specs=[pl.BlockSpec((n, D_MODEL), lambda j: (0, 0)),
                  pl.BlockSpec((D_MODEL, D_MODEL), lambda j: (0, j)),
                  pl.BlockSpec((1, D_MODEL), lambda j: (0, j))],
        out_specs=pl.BlockSpec((n, D_MODEL), lambda j: (0, j)),
        out_shape=jax.ShapeDtypeStruct((n, 6 * D_MODEL), F32),
        compiler_params=_params("arbitrary"),
        name="mods",
    )(c, w_ada_bf, b_ada)


def _mod_spec(per_row, tm, rows_per_batch, k):
    if per_row:
        return pl.BlockSpec((tm, D_MODEL), lambda i, *_: (i, k))
    tiles = rows_per_batch // tm
    return pl.BlockSpec((None, 1, D_MODEL), lambda i, *_: (i // tiles, 0, k))


def _inproj_kernel(x_ref, sc_ref, sh_ref, n1_ref, w_ref, qn_ref, kn_ref, seg_ref, o_ref, h_scr):
    j = pl.program_id(1)

    @pl.when(j == 0)
    def _():
        x = x_ref[...]
        ms = jnp.mean(x * x, axis=-1, keepdims=True)
        xn = x * lax.rsqrt(ms + EPS) * n1_ref[...]
        h_scr[...] = (xn * (1.0 + sc_ref[...]) + sh_ref[...]).astype(BF16)

    acc = jnp.dot(h_scr[...], w_ref[...], preferred_element_type=F32)

    def head_norm(w_row, scale):
        ss = jnp.dot((acc * acc).astype(BF16), seg_ref[...], preferred_element_type=F32)
        return acc * lax.rsqrt(ss * (1.0 / ATT_HEAD_DIM) + EPS) * w_row * scale

    @pl.when(j < CB_K)
    def _():
        o_ref[...] = head_norm(qn_ref[...], ATT_HEAD_DIM ** -0.5)

    @pl.when((j >= CB_K) & (j < CB_V))
    def _():
        o_ref[...] = head_norm(kn_ref[...], 1.0)

    @pl.when(j >= CB_V)
    def _():
        o_ref[...] = acc


def _inproj(x2, mods, per_row, rows_per_batch, tm, n1, w_in_bf, qn, kn, seg):
    n = x2.shape[0]
    const = lambda i, j: (0, 0)
    return pl.pallas_call(
        _inproj_kernel,
        grid=(n // tm, NCOL),
        in_specs=[pl.BlockSpec((tm, D_MODEL), lambda i, j: (i, 0)),
                  _mod_spec(per_row, tm, rows_per_batch, 1),
                  _mod_spec(per_row, tm, rows_per_batch, 0),
                  pl.BlockSpec((1, D_MODEL), const),
                  pl.BlockSpec((D_MODEL, COL), lambda i, j: (0, j)),
                  pl.BlockSpec((1, COL), const),
                  pl.BlockSpec((1, COL), const),
                  pl.BlockSpec((COL, COL), const)],
        out_specs=pl.BlockSpec((tm, COL), lambda i, j: (i, j)),
        out_shape=jax.ShapeDtypeStruct((n, IN_WIDTH), F32),
        scratch_shapes=[pltpu.VMEM((tm, D_MODEL), BF16)],
        compiler_params=_params("arbitrary", "arbitrary"),
        name="inproj",
    )(x2, mods, mods, n1, w_in_bf, qn, kn, seg)


def _attn_kernel(*refs, ns, dil):
    q_ref = refs[0]
    k_refs = refs[1:1 + 2 * ns]
    v_refs = refs[1 + 2 * ns:1 + 4 * ns]
    o_ref, lse_ref = refs[1 + 4 * ns:]
    qrows = SPAN // ns
    nk = 2 * SPAN
    ph = pl.program_id(1) // dil
    jb = pl.program_id(2)
    q = q_ref[...].astype(BF16)
    k = jnp.concatenate([r[...] for r in k_refs], axis=0).astype(BF16)
    v = jnp.concatenate([r[...] for r in v_refs], axis=0).astype(BF16)
    qi = lax.broadcasted_iota(jnp.int32, (qrows, nk), 0)
    kr = lax.broadcasted_iota(jnp.int32, (qrows, nk), 1)
    kph = kr // (2 * qrows)
    kin = kr % (2 * qrows)
    cur = kin >= qrows
    mk = ns * (kin % qrows) + kph + jnp.where(cur, 0, -SPAN)
    delta = ns * qi + ph - mk
    valid = (delta >= 0) & (delta <= SPAN) & (cur | (jb > 0))
    outs, lses = [], []
    for h in range(ATT_HEADS):
        sl = slice(h * ATT_HEAD_DIM, (h + 1) * ATT_HEAD_DIM)
        s = jnp.where(valid, _nt_dot(q[:, sl], k[:, sl]), NEG_INF)
        m = jnp.max(s, axis=-1, keepdims=True)
        p = jnp.exp(s - m)
        l = jnp.sum(p, axis=-1, keepdims=True)
        o = jnp.dot(p.astype(BF16), v[:, sl], preferred_element_type=F32) / l
        outs.append(o)
        lses.append(jnp.broadcast_to(m + jnp.log(l), (qrows, ATT_HEAD_DIM)))
    o_ref[...] = jnp.concatenate(outs, axis=-1)
    lse_ref[...] = jnp.concatenate(lses, axis=-1)


def _attn_group(proj3, g, dil):
    b, s, _ = proj3.shape
    if dil == 1:
        ns, nstream, qrows = 1, 1, SPAN
        pv = proj3
        blk = (None, qrows, COL)
        imap = lambda cb, stream_of, prev: (
            lambda bi, i, jb: (bi, jnp.maximum(jb - 1, 0) if prev else jb, cb + g))
        out_shape = (b, s, ATT_OUT)
        omap = lambda bi, i, jb: (bi, jb, 0)
    else:
        nstream = max(dil, 8)
        ns = nstream // dil
        qrows = SPAN // ns
        pv = proj3.reshape(b, s // nstream, nstream, IN_WIDTH)
        blk = (None, qrows, None, COL)
        imap = lambda cb, stream_of, prev: (
            lambda bi, i, jb: (bi, jnp.maximum(jb - 1, 0) if prev else jb, stream_of(i), cb + g))
        out_shape = (b, s // nstream, nstream, ATT_OUT)
        omap = lambda bi, i, jb: (bi, jb, i, 0)
    nb = s // (nstream * qrows)

    def kv_specs(cb):
        specs = []
        for kph in range(ns):
            stream_of = lambda i, kph=kph: i % dil + dil * kph
            specs += [pl.BlockSpec(blk, imap(cb, stream_of, True)), pl.BlockSpec(blk, imap(cb, stream_of, False))]
        return specs

    in_specs = [pl.BlockSpec(blk, imap(CB_Q, lambda i: i, False))] + kv_specs(CB_K) + kv_specs(CB_V)
    out_spec = pl.BlockSpec(blk, omap)
    o, lse = pl.pallas_call(
        functools.partial(_attn_kernel, ns=ns, dil=dil),
        grid=(b, nstream, nb),
        in_specs=in_specs,
        out_specs=[out_spec, out_spec],
        out_shape=[jax.ShapeDtypeStruct(out_shape, F32)] * 2,
        compiler_params=_params("arbitrary", "arbitrary", "arbitrary"),
        name=f"attn_g{g}",
    )(*([pv] * len(in_specs)))
    return o.reshape(b * s, ATT_OUT), lse.reshape(b * s, ATT_OUT)


ATT_TILE = ATT_GROUPS[-1][1] * SPAN
PAIR = 2 * ATT_HEAD_DIM


def _class_rows(ref, start, n, dil):
    if dil == 1:
        return ref[pl.ds(start, n), :]
    return ref[pl.ds(start, n, stride=dil), :]


def _attn_fused_kernel(*refs):
    in_refs, (o_ref, og_scr, lg_scr) = refs[:5 * N_GROUPS], refs[5 * N_GROUPS:]
    tile = pl.program_id(1)
    lane = lax.broadcasted_iota(jnp.int32, (SPAN, PAIR), 1)
    lo = lane < ATT_HEAD_DIM
    qi = lax.broadcasted_iota(jnp.int32, (SPAN, 2 * SPAN), 0)
    kr = lax.broadcasted_iota(jnp.int32, (SPAN, 2 * SPAN), 1)
    delta = qi + SPAN - kr
    band = (delta >= 0) & (delta <= SPAN)
    cur_keys = kr >= SPAN

    def block(g, dil, r, jq, q_ref, k_ref, kp_ref, v_ref, vp_ref):
        q = _class_rows(q_ref, r + dil * SPAN * jq, SPAN, dil).astype(BF16)
        if jq == 0:
            k = jnp.concatenate([_class_rows(kp_ref, r, SPAN, dil), _class_rows(k_ref, r, SPAN, dil)], axis=0)
            v = jnp.concatenate([_class_rows(vp_ref, r, SPAN, dil), _class_rows(v_ref, r, SPAN, dil)], axis=0)
            valid = band & (cur_keys | (tile > 0))
        else:
            k = _class_rows(k_ref, r + dil * SPAN * (jq - 1), 2 * SPAN, dil)
            v = _class_rows(v_ref, r + dil * SPAN * (jq - 1), 2 * SPAN, dil)
            valid = band
        k = k.astype(BF16)
        v = v.astype(BF16)
        o_pair, lse_pair = None, None
        for first in (True, False):
            mine = lo if first else jnp.logical_not(lo)
            qh = jnp.where(mine, q, jnp.zeros_like(q))
            s = jnp.where(valid, _nt_dot(qh, k), NEG_INF)
            m = jnp.max(s, axis=-1, keepdims=True)
            p = jnp.exp(s - m)
            l = jnp.sum(p, axis=-1, keepdims=True)
            oh = jnp.dot(p.astype(BF16), v, preferred_element_type=F32) / l
            lse = m + jnp.log(l)
            o_pair = oh if first else jnp.where(lo, o_pair, oh)
            lse_pair = jnp.broadcast_to(lse, (SPAN, PAIR)) if first else jnp.where(lo, lse_pair, lse)
        start = r + dil * SPAN * jq
        if dil == 1:
            og_scr[g, pl.ds(start, SPAN), :] = o_pair
            lg_scr[g, pl.ds(start, SPAN), :] = lse_pair
        else:
            og_scr[g, pl.ds(start, SPAN, stride=dil), :] = o_pair
            lg_scr[g, pl.ds(start, SPAN, stride=dil), :] = lse_pair

    for g, (win, dil) in enumerate(ATT_GROUPS):
        grefs = in_refs[5 * g:5 * g + 5]
        nblk = ATT_TILE // (dil * SPAN)
        if dil == 1:
            for jq in range(nblk):
                block(g, dil, 0, jq, *grefs)
        else:
            def residue(r, carry, g=g, dil=dil, nblk=nblk, grefs=grefs):
                for jq in range(nblk):
                    block(g, dil, r, jq, *grefs)
                return carry
            lax.fori_loop(0, dil, residue, 0)

    lses = [lg_scr[g] for g in range(N_GROUPS)]
    mx = functools.reduce(jnp.maximum, lses)
    es = [jnp.exp(x - mx) for x in lses]
    num = es[0] * og_scr[0] + es[1] * og_scr[1] + es[2] * og_scr[2]
    o_ref[...] = (num / (es[0] + es[1] + es[2])).astype(o_ref.dtype)


def _attn_prompt(proj3):
    b, s, _ = proj3.shape
    npair = ATT_OUT // PAIR
    in_specs = []
    for g, (win, dil) in enumerate(ATT_GROUPS):
        prev_rows = dil * SPAN
        per_tile = ATT_TILE // prev_rows
        for cb in (CB_Q, CB_K, CB_V):
            col = lambda hp, cb=cb, g=g: (cb + g) * (COL // PAIR) + hp
            in_specs.append(pl.BlockSpec((None, ATT_TILE, PAIR), lambda bi, t, hp, col=col: (bi, t, col(hp))))
            if cb != CB_Q:
                in_specs.append(pl.BlockSpec(
                    (None, prev_rows, PAIR),
                    lambda bi, t, hp, col=col, per_tile=per_tile: (bi, jnp.maximum(t * per_tile - 1, 0), col(hp))))
    att = pl.pallas_call(
        _attn_fused_kernel,
        grid=(b, s // ATT_TILE, npair),
        in_specs=in_specs,
        out_specs=pl.BlockSpec((None, ATT_TILE, PAIR), lambda bi, t, hp: (bi, t, hp)),
        out_shape=jax.ShapeDtypeStruct((b, s, ATT_OUT), BF16),
        scratch_shapes=[pltpu.VMEM((N_GROUPS, ATT_TILE, PAIR), F32), pltpu.VMEM((N_GROUPS, ATT_TILE, PAIR), F32)],
        compiler_params=_params("arbitrary", "arbitrary", "arbitrary"),
        name="attn_prompt",
    )(*([proj3] * len(in_specs)))
    return att.reshape(b * s, ATT_OUT)


def _lower_bound(logits):
    mx = jnp.max(logits, axis=0, keepdims=True)
    e = jnp.exp(logits - mx)
    return e[0:1] / jnp.sum(e, axis=0, keepdims=True)


def _hgrn_kernel(qh_ref, fh_ref, ih_ref, gh_ref, lbl_ref, hn_ref, tri_ref, o_ref, st_ref,
                 st_scr, q_scr, k_scr, b_scr, *, tb):
    t = pl.program_id(2)
    nh = COL // HG_DK
    c_rows = HG_CHUNK

    @pl.when(t == 0)
    def _():
        st_scr[...] = jnp.zeros_like(st_scr)

    lb = _lower_bound(lbl_ref[...])
    f = lb + (1.0 - lb) * jax.nn.sigmoid(fh_ref[...])
    k_scr[...] = 1.0 - f
    q_scr[...] = _silu(qh_ref[...])
    logf = jnp.log(f)
    for r in range(tb // 128):
        rows = slice(r * 128, (r + 1) * 128)
        b_scr[rows, :] = jnp.dot(tri_ref[...], logf[rows, :], precision=lax.Precision.HIGHEST,
                                 preferred_element_type=F32)
    rowid = lax.broadcasted_iota(jnp.int32, (c_rows, 1), 0)

    def chunk(c, carry):
        r0 = pl.multiple_of(c * c_rows, c_rows)
        rows = pl.ds(r0, c_rows)
        for h in range(nh):
            cols = slice(h * HG_DK, (h + 1) * HG_DK)
            b = b_scr[rows, cols]
            qc = q_scr[rows, cols]
            kc = k_scr[rows, cols]
            vc = ih_ref[rows, cols]
            st = st_scr[h]
            o = _nt_dot((qc * jnp.exp(b)).astype(BF16), st.astype(BF16))
            for s in range(c_rows):
                e = jnp.exp(jnp.minimum(b - b[s:s + 1], 0.0))
                a = jnp.sum(qc * kc[s:s + 1] * e, axis=-1, keepdims=True)
                o = o + jnp.where(rowid >= s, a, 0.0) * vc[s:s + 1]
            bl = b[c_rows - 1:c_rows]
            kt = kc * jnp.exp(bl - b)
            upd = lax.dot_general(vc.astype(BF16), kt.astype(BF16), (((0,), (0,)), ((), ())),
                                  preferred_element_type=F32)
            st_scr[h] = st * jnp.exp(bl) + upd
            ms = jnp.mean(o * o, axis=-1, keepdims=True)
            on = o * lax.rsqrt(ms + EPS) * hn_ref[...]
            o_ref[rows, cols] = on * _silu(gh_ref[rows, cols])
        return carry

    lax.fori_loop(0, tb // c_rows, chunk, 0)

    @pl.when(t == pl.num_programs(2) - 1)
    def _():
        for h in range(nh):
            st_ref[h] = st_scr[h].T


def _hgrn_prompt(proj3, lb_logits, hn, tri, tb):
    b, s, _ = proj3.shape
    nhb = HG_HEADS * HG_DK // COL
    nh = COL // HG_DK

    def spec(cb):
        return pl.BlockSpec((None, tb, COL), lambda bi, hb, t: (bi, t, cb + hb))

    o, st = pl.pallas_call(
        functools.partial(_hgrn_kernel, tb=tb),
        grid=(b, nhb, s // tb),
        in_specs=[spec(CB_QH), spec(CB_FH), spec(CB_IH), spec(CB_GH),
                  pl.BlockSpec((lb_logits.shape[0], COL), lambda bi, hb, t: (0, hb)),
                  pl.BlockSpec((1, HG_DV), lambda bi, hb, t: (0, 0)),
                  pl.BlockSpec((128, 128), lambda bi, hb, t: (0, 0))],
        out_specs=[pl.BlockSpec((None, tb, COL), lambda bi, hb, t: (bi, t, hb)),
                   pl.BlockSpec((None, nh, HG_DK, HG_DV), lambda bi, hb, t: (bi, hb, 0, 0))],
        out_shape=[jax.ShapeDtypeStruct((b, s, HG_HEADS * HG_DV), F32),
                   jax.ShapeDtypeStruct((b, HG_HEADS, HG_DK, HG_DV), F32)],
        scratch_shapes=[pltpu.VMEM((nh, HG_DV, HG_DK), F32),
                        pltpu.VMEM((tb, COL), F32), pltpu.VMEM((tb, COL), F32), pltpu.VMEM((tb, COL), F32)],
        compiler_params=_params("arbitrary", "arbitrary", "arbitrary"),
        name="hgrn_prompt",
    )(proj3, proj3, proj3, proj3, lb_logits, hn, tri)
    return o.reshape(b * s, HG_HEADS * HG_DV), st


def _mix_kernel(*refs, n_att):
    n_refs = 1 if n_att == 1 else 2 * n_att
    att_refs = refs[:n_refs]
    (hg_ref, ga0, ga1, gb0, gb1, x_ref, g1_ref, sc2_ref, sh2_ref, n2_ref,
     wa_ref, wb_ref, wo_ref, x1_ref, h2_ref) = refs[n_refs:]
    if n_att == 1:
        att = att_refs[0][...]
    else:
        lses = [att_refs[2 * g + 1][...] for g in range(n_att)]
        mx = functools.reduce(jnp.maximum, lses)
        es = [jnp.exp(l - mx) for l in lses]
        den = functools.reduce(lambda a, b: a + b, es)
        num = functools.reduce(lambda a, b: a + b, [es[g] * att_refs[2 * g][...] for g in range(n_att)])
        att = num / den
    ga = jnp.concatenate([ga0[...], ga1[...]], axis=-1)
    gb = jnp.concatenate([gb0[...], gb1[...]], axis=-1)
    ya = jnp.dot(att.astype(BF16), wa_ref[...], preferred_element_type=F32)
    yb = jnp.dot(hg_ref[...].astype(BF16), wb_ref[...], preferred_element_type=F32)
    y = jax.nn.sigmoid(ga) * ya + jax.nn.sigmoid(gb) * yb
    x1 = x_ref[...] + g1_ref[...] * jnp.dot(y.astype(BF16), wo_ref[...], preferred_element_type=F32)
    x1_ref[...] = x1
    ms = jnp.mean(x1 * x1, axis=-1, keepdims=True)
    xn = x1 * lax.rsqrt(ms + EPS) * n2_ref[...]
    h2_ref[...] = (xn * (1.0 + sc2_ref[...]) + sh2_ref[...]).astype(BF16)


def _mix(att_list, hg, proj2, x2, mods, per_row, rows_per_batch, tm, n2, wa, wb, wo):
    n = x2.shape[0]
    row = lambda w: pl.BlockSpec((tm, w), lambda i: (i, 0))
    colblk = lambda cb: pl.BlockSpec((tm, COL), lambda i: (i, cb))
    full = lambda a: pl.BlockSpec(a.shape, lambda i: (0, 0))
    in_specs = ([row(ATT_OUT)] * len(att_list)
                + [row(D_MODEL), colblk(CB_GA), colblk(CB_GA + 1), colblk(CB_GB), colblk(CB_GB + 1), row(D_MODEL),
                   _mod_spec(per_row, tm, rows_per_batch, 2), _mod_spec(per_row, tm, rows_per_batch, 4),
                   _mod_spec(per_row, tm, rows_per_batch, 3), full(n2), full(wa), full(wb), full(wo)])
    n_att = 1 if len(att_list) == 1 else len(att_list) // 2
    return pl.pallas_call(
        functools.partial(_mix_kernel, n_att=n_att),
        grid=(n // tm,),
        in_specs=in_specs,
        out_specs=[row(D_MODEL), row(D_MODEL)],
        out_shape=[jax.ShapeDtypeStruct((n, D_MODEL), F32), jax.ShapeDtypeStruct((n, D_MODEL), BF16)],
        compiler_params=_params("arbitrary"),
        name="mix",
    )(*att_list, hg, proj2, proj2, proj2, proj2, x2, mods, mods, mods, n2, wa, wb, wo)


TOK_LANES = 128
CAND_ROWS = ((0, 16), (1, 16), (2, 8), (3, 8), (4, 8), (5, 8), (6, 8), (7, 8))


def _top16_rows(s, ridx, sentinel):
    vals, idxs = [], []
    for _ in range(PEER_TOPK):
        m = jnp.max(s, axis=0, keepdims=True)
        am = jnp.min(jnp.where(s == m, ridx, sentinel), axis=0, keepdims=True)
        vals.append(m)
        idxs.append(am)
        s = jnp.where(ridx == am, NEG_INF, s)
    return jnp.concatenate(vals, axis=0), jnp.concatenate(idxs, axis=0)


def _route_kernel(h2_ref, wq_ref, sk_ref, a_ref, b_ref, g_ref, q_scr, at_scr, bt_scr, gt_scr):
    tm = h2_ref.shape[0]
    tt = TOK_LANES
    q = jnp.dot(h2_ref[...], wq_ref[...], preferred_element_type=F32)
    for hp in range(2 * PEER_HEADS):
        q_scr[hp] = q[:, hp * PEER_HALF:(hp + 1) * PEER_HALF].astype(BF16)
    kidx = lax.broadcasted_iota(jnp.int32, (PEER_KEYS, tt), 0)
    sub16 = lax.broadcasted_iota(jnp.int32, (PEER_TOPK, tt), 0)
    sub8 = lax.broadcasted_iota(jnp.int32, (8, tt), 0)
    cflat = jnp.concatenate([p * PEER_TOPK + (sub16 if nq == 16 else sub8) for p, nq in CAND_ROWS]
                            + [(sub8 + 8) * PEER_TOPK], axis=0)

    def group(gi, carry):
        rows = pl.ds(pl.multiple_of(gi * tt, tt), tt)

        def head(h, carry2):
            v1, i1 = _top16_rows(_nt_dot(sk_ref[0], q_scr[2 * h, rows, :]), kidx, PEER_KEYS)
            v2, i2 = _top16_rows(_nt_dot(sk_ref[1], q_scr[2 * h + 1, rows, :]), kidx, PEER_KEYS)
            cand = jnp.concatenate([v1[p:p + 1] + v2[0:nq] for p, nq in CAND_ROWS] + [v1[8:16] + v2[0:1]], axis=0)
            tv, tp = _top16_rows(cand, cflat, PEER_TOPK * PEER_TOPK)
            e = jnp.exp(tv - tv[0:1])
            g = e / jnp.sum(e, axis=0, keepdims=True)
            pr = tp >> 4
            qr = tp & 15
            a_sel = jnp.zeros((PEER_TOPK, tt), jnp.int32)
            b_sel = jnp.zeros((PEER_TOPK, tt), jnp.int32)
            for p in range(PEER_TOPK):
                a_sel = jnp.where(pr == p, i1[p:p + 1], a_sel)
                b_sel = jnp.where(qr == p, i2[p:p + 1], b_sel)
            slot = pl.ds(pl.multiple_of(h * PEER_TOPK, PEER_TOPK), PEER_TOPK)
            at_scr[slot, :] = a_sel
            bt_scr[slot, :] = b_sel
            gt_scr[slot, :] = g
            return carry2

        lax.fori_loop(0, PEER_HEADS, head, 0)
        a_ref[rows, :] = at_scr[...].T
        b_ref[rows, :] = bt_scr[...].T
        g_ref[rows, :] = gt_scr[...].T
        return carry

    lax.fori_loop(0, tm // tt, group, 0)


def _route(h2, wq_bf, sk_bf, tm):
    n = h2.shape[0]
    assert tm % TOK_LANES == 0 and n % tm == 0
    row = pl.BlockSpec((tm, 128), lambda i: (i, 0))
    slots = PEER_HEADS * PEER_TOPK
    return pl.pallas_call(
        _route_kernel,
        grid=(n // tm,),
        in_specs=[pl.BlockSpec((tm, D_MODEL), lambda i: (i, 0)),
                  pl.BlockSpec(wq_bf.shape, lambda i: (0, 0)),
                  pl.BlockSpec(sk_bf.shape, lambda i: (0, 0, 0))],
        out_specs=[row, row, row],
        out_shape=[jax.ShapeDtypeStruct((n, 128), jnp.int32), jax.ShapeDtypeStruct((n, 128), jnp.int32),
                   jax.ShapeDtypeStruct((n, 128), F32)],
        scratch_shapes=[pltpu.VMEM((2 * PEER_HEADS, tm, PEER_HALF), BF16),
                        pltpu.VMEM((slots, TOK_LANES), jnp.int32), pltpu.VMEM((slots, TOK_LANES), jnp.int32),
                        pltpu.VMEM((slots, TOK_LANES), F32)],
        compiler_params=_params("arbitrary"),
        name="peer_route",
    )(h2, wq_bf, sk_bf)


def _peer_u_kernel(h2_ref, u_ref, a_ref, b_ref, g_ref, w_ref, act_scr, *, ac):
    c = pl.program_id(1)

    @pl.when(c == 0)
    def _():
        act_scr[...] = jnp.zeros_like(act_scr)

    hc = _nt_dot(h2_ref[...], u_ref[...])
    a_idx = a_ref[...]
    b_idx = b_ref[...]
    act = act_scr[...]
    for i in range(ac):
        gathered = jnp.take_along_axis(hc[:, i * 128:(i + 1) * 128], b_idx, axis=1)
        act = jnp.where(a_idx == c * ac + i, gathered, act)
    act_scr[...] = act

    @pl.when(c == pl.num_programs(1) - 1)
    def _():
        w_ref[...] = g_ref[...] * _gelu_tanh(act)


def _peer_u(h2, u_bf, a_idx, b_idx, gate, tm, ac):
    n = h2.shape[0]
    row = pl.BlockSpec((tm, 128), lambda i, c: (i, 0))
    return pl.pallas_call(
        functools.partial(_peer_u_kernel, ac=ac),
        grid=(n // tm, PEER_KEYS // ac),
        in_specs=[pl.BlockSpec((tm, D_MODEL), lambda i, c: (i, 0)),
                  pl.BlockSpec((ac * 128, D_MODEL), lambda i, c: (c, 0)),
                  row, row, row],
        out_specs=row,
        out_shape=jax.ShapeDtypeStruct((n, 128), F32),
        scratch_shapes=[pltpu.VMEM((tm, 128), F32)],
        compiler_params=_params("arbitrary", "arbitrary"),
        name="peer_u",
    )(h2, u_bf, a_idx, b_idx, gate)


def _peer_v_kernel(a_ref, b_ref, w_ref, v_ref, x1_ref, g2_ref, o_ref, w3_scr, acc_scr, *, ac, tm):
    c = pl.program_id(1)

    @pl.when(c == 0)
    def _():
        sub = lax.broadcasted_iota(jnp.int32, (128, 128), 0)

        def build(n, carry):
            ar = a_ref[pl.ds(n, 1), :]
            br = b_ref[pl.ds(n, 1), :]
            wr = w_ref[pl.ds(n, 1), :]
            at = jnp.where(sub == ar, 1.0, 0.0).astype(BF16)
            rt = jnp.where(sub == br, wr, 0.0).astype(BF16)
            w3_scr[pl.ds(pl.multiple_of(n * W_PITCH, 8), 128), :] = _nt_dot(at, rt)
            return carry

        lax.fori_loop(0, tm, build, 0, unroll=8)

    part = None
    for i in range(0, ac, 2):
        lhs = jnp.concatenate([w3_scr[pl.ds(c * ac + i, tm, stride=W_PITCH), :],
                               w3_scr[pl.ds(c * ac + i + 1, tm, stride=W_PITCH), :]], axis=-1)
        d = jnp.dot(lhs.astype(BF16), v_ref[i * 128:(i + 2) * 128, :], preferred_element_type=F32)
        part = d if part is None else part + d

    @pl.when(c == 0)
    def _():
        acc_scr[...] = part

    @pl.when(c > 0)
    def _():
        acc_scr[...] += part

    @pl.when(c == pl.num_programs(1) - 1)
    def _():
        o_ref[...] = x1_ref[...] + g2_ref[...] * acc_scr[...]


def _peer_v(a_idx, b_idx, wts, v_bf, x1, mods, per_row, rows_per_batch, tm, ac):
    n = x1.shape[0]
    row = pl.BlockSpec((tm, 128), lambda i, c: (i, 0))
    wide = pl.BlockSpec((tm, D_MODEL), lambda i, c: (i, 0))
    return pl.pallas_call(
        functools.partial(_peer_v_kernel, ac=ac, tm=tm),
        grid=(n // tm, PEER_KEYS // ac),
        in_specs=[row, row, row,
                  pl.BlockSpec((ac * 128, D_MODEL), lambda i, c: (c, 0)),
                  wide, _mod_spec(per_row, tm, rows_per_batch, 5)],
        out_specs=wide,
        out_shape=jax.ShapeDtypeStruct((n, D_MODEL), F32),
        scratch_shapes=[pltpu.VMEM((tm * W_PITCH, 128), F32), pltpu.VMEM((tm, D_MODEL), F32)],
        compiler_params=_params("arbitrary", "arbitrary"),
        name="peer_v",
    )(a_idx, b_idx, wts, v_bf, x1, mods)


def _decode_attn_kernel(q_ref, k_ref, v_ref, c0_ref, c1_ref, c2_ref, seg_ref, segt_ref, o_ref, *, bt):
    caches = (c0_ref, c1_ref, c2_ref)
    hi = lax.Precision.HIGHEST
    for i in range(bt):
        ms, ls, accs = [], [], []
        for g in range(N_GROUPS):
            cols = slice(g * ATT_OUT, (g + 1) * ATT_OUT)
            q = q_ref[i:i + 1, cols]
            kn = k_ref[i:i + 1, cols]
            vn = v_ref[i:i + 1, cols]
            kt = caches[g][i, :, 0:ATT_OUT]
            vt = caches[g][i, :, ATT_OUT:2 * ATT_OUT]
            s = jnp.dot(kt * q, seg_ref[...], precision=hi, preferred_element_type=F32)
            s0 = jnp.dot(kn * q, seg_ref[...], precision=hi, preferred_element_type=F32)
            m = jnp.maximum(jnp.max(s, axis=0, keepdims=True), s0)
            p = jnp.exp(s - m)
            p0 = jnp.exp(s0 - m)
            l = jnp.sum(p, axis=0, keepdims=True) + p0
            pe = jnp.dot(p, segt_ref[...], precision=hi, preferred_element_type=F32)
            p0e = jnp.dot(p0, segt_ref[...], precision=hi, preferred_element_type=F32)
            acc = jnp.sum(pe * vt, axis=0, keepdims=True) + p0e * vn
            ms.append(jnp.dot(m, segt_ref[...], precision=hi, preferred_element_type=F32))
            ls.append(jnp.dot(l, segt_ref[...], precision=hi, preferred_element_type=F32))
            accs.append(acc)
        lses = [ms[g] + jnp.log(ls[g]) for g in range(N_GROUPS)]
        mx = functools.reduce(jnp.maximum, lses)
        es = [jnp.exp(x - mx) for x in lses]
        den = es[0] + es[1] + es[2]
        num = es[0] * (accs[0] / ls[0]) + es[1] * (accs[1] / ls[1]) + es[2] * (accs[2] / ls[2])
        o_ref[i:i + 1, :] = num / den


def _decode_attn(proj_s, caches, seg8, seg8t, bt):
    n = proj_s.shape[0]
    views, specs = [], []
    for (win, dil), cache in zip(ATT_GROUPS, caches):
        length = cache.shape[1]
        views.append(cache.reshape(n, length // dil, dil * 2 * ATT_OUT))
        specs.append(pl.BlockSpec((bt, length // dil, 2 * ATT_OUT), lambda i: (i, 0, 0)))
    blk = lambda cb: pl.BlockSpec((bt, N_GROUPS * ATT_OUT), lambda i: (i, cb))
    return pl.pallas_call(
        functools.partial(_decode_attn_kernel, bt=bt),
        grid=(n // bt,),
        in_specs=[blk(0), blk(1), blk(2)] + specs
                 + [pl.BlockSpec(seg8.shape, lambda i: (0, 0)), pl.BlockSpec(seg8t.shape, lambda i: (0, 0))],
        out_specs=pl.BlockSpec((bt, ATT_OUT), lambda i: (i, 0)),
        out_shape=jax.ShapeDtypeStruct((n, ATT_OUT), F32),
        compiler_params=_params("arbitrary"),
        name="decode_attn",
    )(proj_s, proj_s, proj_s, *views, seg8, seg8t)


def _decode_hgrn_kernel(q0, q1, f0, f1, i0, i1, g0, g1, lbl_ref, hn_ref, st_ref, o_ref, sto_ref, *, bt):
    nh = COL // HG_DK
    lb = _lower_bound(lbl_ref[...])
    for h in range(HG_HEADS):
        qr, fr, ir, gr = ((q0, f0, i0, g0), (q1, f1, i1, g1))[h // nh]
        cols = slice((h % nh) * HG_DK, (h % nh + 1) * HG_DK)
        lbh = lb[:, h * HG_DK:(h + 1) * HG_DK]
        f = lbh + (1.0 - lbh) * jax.nn.sigmoid(fr[:, cols])
        ft = f.T
        kt = 1.0 - ft
        qt = _silu(qr[:, cols]).T
        for i in range(bt):
            v = ir[i:i + 1, cols]
            s_new = ft[:, i:i + 1] * st_ref[i, h] + kt[:, i:i + 1] * v
            sto_ref[i, h] = s_new
            o = jnp.sum(qt[:, i:i + 1] * s_new, axis=0, keepdims=True)
            ms = jnp.mean(o * o, axis=-1, keepdims=True)
            o_ref[i:i + 1, h * HG_DV:(h + 1) * HG_DV] = (o * lax.rsqrt(ms + EPS) * hn_ref[...]
                                                         * _silu(gr[i:i + 1, cols]))


def _decode_hgrn(proj_s, state, lb_logits, hn, bt):
    n = proj_s.shape[0]
    st_spec = pl.BlockSpec((bt, HG_HEADS, HG_DK, HG_DV), lambda i: (i, 0, 0, 0))
    half = lambda cb: pl.BlockSpec((bt, COL), lambda i: (i, cb))
    return pl.pallas_call(
        functools.partial(_decode_hgrn_kernel, bt=bt),
        grid=(n // bt,),
        in_specs=[half(CB_QH), half(CB_QH + 1), half(CB_FH), half(CB_FH + 1), half(CB_IH), half(CB_IH + 1),
                  half(CB_GH), half(CB_GH + 1),
                  pl.BlockSpec(lb_logits.shape, lambda i: (0, 0)),
                  pl.BlockSpec((1, HG_DV), lambda i: (0, 0)),
                  st_spec],
        out_specs=[pl.BlockSpec((bt, HG_HEADS * HG_DV), lambda i: (i, 0)), st_spec],
        out_shape=[jax.ShapeDtypeStruct((n, HG_HEADS * HG_DV), F32),
                   jax.ShapeDtypeStruct(state.shape, F32)],
        compiler_params=_params("arbitrary"),
        name="decode_hgrn",
    )(proj_s, proj_s, proj_s, proj_s, proj_s, proj_s, proj_s, proj_s, lb_logits, hn, state)


def _block_diag_ones(n, seg):
    i = jnp.arange(n)
    return (i[:, None] // seg == i[None, :] // seg)


def _peer(h2, x1, mods, per_row, rows_per_batch, tm, wq_bf, sk_bf, u_bf, v_bf, ac):
    a_idx, b_idx, gate = _route(h2, wq_bf, sk_bf, tm)
    wts = _peer_u(h2, u_bf, a_idx, b_idx, gate, tm, ac)
    return _peer_v(a_idx, b_idx, wts, v_bf, x1, mods, per_row, rows_per_batch, tm, ac)


def _kv_rows(proj3, g, rows):
    k = proj3[:, -rows:, (CB_K + g) * COL:(CB_K + g + 1) * COL]
    v = proj3[:, -rows:, (CB_V + g) * COL:(CB_V + g + 1) * COL]
    b = proj3.shape[0]
    kv = jnp.stack([k, v], axis=2)
    return kv.reshape(1, b, rows, 2, ATT_HEADS, ATT_HEAD_DIM)


def kernel(x_prompt, x_sample, cache_kv_w128, cache_kv_w512, cache_kv_w2048, state_hgrn, c_prompt, c_sample, w_ada, b_ada, norm1_w, norm2_w, w_in, q_norm_w, k_norm_w, hg_lb_logits, hg_norm_w, w_br_a, w_br_b, w_o, w_peer_q, peer_subkeys, peer_u, peer_v):
    bsz, seq, _ = x_prompt.shape
    dec, dec_t, _ = x_sample.shape
    assert w_ada.shape[0] == 1 and dec_t == 1 and seq % (ATT_GROUPS[-1][1] * SPAN) == 0
    for (win, dil), cache in zip(ATT_GROUPS, (cache_kv_w128, cache_kv_w512, cache_kv_w2048)):
        assert win == dil * SPAN and cache.shape[2] == win

    w_ada_bf = w_ada[0].astype(BF16)
    w_in_bf = w_in[0].astype(BF16)
    wa, wb, wo = w_br_a[0].astype(BF16), w_br_b[0].astype(BF16), w_o[0].astype(BF16)
    wq_bf = w_peer_q[0].astype(BF16)
    sk_bf = peer_subkeys[0].astype(BF16)
    u_bf = peer_u[0].astype(BF16)
    v_bf = peer_v[0].astype(BF16)
    n1 = norm1_w[0].reshape(1, D_MODEL)
    n2 = norm2_w[0].reshape(1, D_MODEL)
    qn = jnp.tile(q_norm_w[0], ATT_HEADS).reshape(1, COL)
    kn = jnp.tile(k_norm_w[0], ATT_HEADS).reshape(1, COL)
    hn = hg_norm_w[0].reshape(1, HG_DV)
    seg = _block_diag_ones(COL, ATT_HEAD_DIM).astype(BF16)
    tri = (_block_diag_ones(128, HG_CHUNK) & (jnp.arange(128)[:, None] >= jnp.arange(128)[None, :])).astype(F32)
    head_of_col = jnp.arange(ATT_OUT)[:, None] // ATT_HEAD_DIM
    seg8 = (head_of_col == jnp.arange(128)[None, :]).astype(F32)
    seg8t = seg8.T

    mods = _mods(jnp.concatenate([c_prompt, c_sample], axis=0), w_ada_bf, b_ada)
    mods_p = mods[:bsz].reshape(bsz, 1, 6 * D_MODEL)
    mods_s = mods[bsz:]

    n_p = bsz * seq
    tm_p = 512
    ac = 8
    xp2 = x_prompt.reshape(n_p, D_MODEL)
    proj_p = _inproj(xp2, mods_p, False, seq, 1024, n1, w_in_bf, qn, kn, seg)
    proj_p3 = proj_p.reshape(bsz, seq, IN_WIDTH)
    att_p = _attn_prompt(proj_p3)
    hg_p, st_p = _hgrn_prompt(proj_p3, hg_lb_logits, hn, tri, 256)
    x1_p, h2_p = _mix([att_p], hg_p, proj_p, xp2, mods_p, False, seq, tm_p, n2, wa, wb, wo)
    y_p = _peer(h2_p, x1_p, mods_p, False, seq, 256, wq_bf, sk_bf, u_bf, v_bf, ac)

    xs2 = x_sample.reshape(dec, D_MODEL)
    proj_s = _inproj(xs2, mods_s, True, 1, dec, n1, w_in_bf, qn, kn, seg)
    att_s = _decode_attn(proj_s, (cache_kv_w128[0], cache_kv_w512[0], cache_kv_w2048[0]), seg8, seg8t, 8)
    hg_s, st_s = _decode_hgrn(proj_s, state_hgrn[0], hg_lb_logits, hn, 8)
    x1_s, h2_s = _mix([att_s], hg_s, proj_s, xs2, mods_s, True, 1, dec, n2, wa, wb, wo)
    y_s = _peer(h2_s, x1_s, mods_s, True, 1, dec, wq_bf, sk_bf, u_bf, v_bf, ac)

    proj_s3 = proj_s.reshape(dec, 1, IN_WIDTH)
    return (y_p.reshape(bsz, seq, D_MODEL), y_s.reshape(dec, 1, D_MODEL),
            _kv_rows(proj_p3, 0, min(ATT_GROUPS[0][0], seq)), _kv_rows(proj_p3, 1, min(ATT_GROUPS[1][0], seq)),
            _kv_rows(proj_p3, 2, min(ATT_GROUPS[2][0], seq)), st_p[None],
            _kv_rows(proj_s3, 0, 1), _kv_rows(proj_s3, 1, 1), _kv_rows(proj_s3, 2, 1), st_s[None])
```

```python
import functools

import jax
import jax.numpy as jnp
from jax import lax
from jax.experimental import pallas as pl
from jax.experimental.pallas import tpu as pltpu

F32 = jnp.float32
BF16 = jnp.bfloat16

D_MODEL = 1024
ATT_GROUPS = ((128, 1), (512, 4), (2048, 16))
N_GROUPS = 3
ATT_HEADS = 8
ATT_HEAD_DIM = 64
ATT_OUT = ATT_HEADS * ATT_HEAD_DIM
ATT_WIDTH = N_GROUPS * ATT_OUT
MXU_TILE = 256
SPAN = 128
HG_HEADS = 8
HG_DK = 128
HG_DV = 128
PEER_HEADS = 8
PEER_KEYS = 128
PEER_TOPK = 16
PEER_HALF = 128
EPS = 1e-6
IN_WIDTH = 10752
COL = 512
NCOL = IN_WIDTH // COL
CB_Q, CB_K, CB_V = 0, 3, 6
CB_QH, CB_FH, CB_IH, CB_GH, CB_GA, CB_GB = 9, 11, 13, 15, 17, 19
HG_CHUNK = 16
W_PITCH = 136
VMEM_LIMIT = 56 * 1024 * 1024

NEG_INF = float("-inf")


def _silu(x):
    return x * jax.nn.sigmoid(x)


def _gelu_tanh(x):
    return 0.5 * x * (1.0 + jnp.tanh(0.7978845608028654 * (x + 0.044715 * (x * x * x))))


def _nt_dot(a, b):
    return lax.dot_general(a, b, (((1,), (1,)), ((), ())), preferred_element_type=F32)


def _params(*sem):
    return pltpu.CompilerParams(dimension_semantics=sem, vmem_limit_bytes=VMEM_LIMIT)


def _mods_kernel(c_ref, w_ref, b_ref, o_ref):
    s = _silu(c_ref[...])
    o_ref[...] = jnp.dot(s.astype(BF16), w_ref[...], preferred_element_type=F32) + b_ref[...]


def _mods(c, w_ada_bf, b_ada):
    n = c.shape[0]
    return pl.pallas_call(
        _mods_kernel,
        grid=(6,),
        in_specs=[pl.BlockSpec((n, D_MODEL), lambda j: (0, 0)),
                  pl.BlockSpec((D_MODEL, D_MODEL), lambda j: (0, j)),
                  pl.BlockSpec((1, D_MODEL), lambda j: (0, j))],
        out_specs=pl.BlockSpec((n, D_MODEL), lambda j: (0, j)),
        out_shape=jax.ShapeDtypeStruct((n, 6 * D_MODEL), F32),
        compiler_params=_params("arbitrary"),
        name="mods",
    )(c, w_ada_bf, b_ada)


def _mod_spec(per_row, tm, rows_per_batch, k):
    if per_row:
        return pl.BlockSpec((tm, D_MODEL), lambda i, *_: (i, k))
    tiles = rows_per_batch // tm
    return pl.BlockSpec((None, 1, D_MODEL), lambda i, *_: (i // tiles, 0, k))


def _inproj_kernel(x_ref, sc_ref, sh_ref, n1_ref, w_ref, qn_ref, kn_ref, seg_ref, o_ref, h_scr):
    j = pl.program_id(1)

    @pl.when(j == 0)
    def _():
        x = x_ref[...]
        ms = jnp.mean(x * x, axis=-1, keepdims=True)
        xn = x * lax.rsqrt(ms + EPS) * n1_ref[...]
        h_scr[...] = (xn * (1.0 + sc_ref[...]) + sh_ref[...]).astype(BF16)

    acc = jnp.dot(h_scr[...], w_ref[...], preferred_element_type=F32)

    def head_norm(w_row, scale):
        sq = (acc * acc).astype(BF16)
        seg_w = seg_ref.shape[0]
        ss = jnp.concatenate([jnp.dot(sq[:, c:c + seg_w], seg_ref[...], preferred_element_type=F32)
                              for c in range(0, ATT_WIDTH, seg_w)], axis=-1)
        return acc * lax.rsqrt(ss * (1.0 / ATT_HEAD_DIM) + EPS) * w_row * scale

    @pl.when(j == 0)
    def _():
        o_ref[...] = head_norm(qn_ref[...], ATT_HEAD_DIM ** -0.5)

    @pl.when(j == 1)
    def _():
        o_ref[...] = head_norm(kn_ref[...], 1.0)

    @pl.when(j >= 2)
    def _():
        o_ref[...] = acc


def _inproj(x2, mods, per_row, rows_per_batch, tm, n1, w_in_bf, qn, kn, seg):
    n = x2.shape[0]
    const = lambda i, j: (0, 0)
    return pl.pallas_call(
        _inproj_kernel,
        grid=(n // tm, IN_WIDTH // ATT_WIDTH),
        in_specs=[pl.BlockSpec((tm, D_MODEL), lambda i, j: (i, 0)),
                  _mod_spec(per_row, tm, rows_per_batch, 1),
                  _mod_spec(per_row, tm, rows_per_batch, 0),
                  pl.BlockSpec((1, D_MODEL), const),
                  pl.BlockSpec((D_MODEL, ATT_WIDTH), lambda i, j: (0, j)),
                  pl.BlockSpec((1, ATT_WIDTH), const),
                  pl.BlockSpec((1, ATT_WIDTH), const),
                  pl.BlockSpec(seg.shape, const)],
        out_specs=pl.BlockSpec((tm, ATT_WIDTH), lambda i, j: (i, j)),
        out_shape=jax.ShapeDtypeStruct((n, IN_WIDTH), F32),
        scratch_shapes=[pltpu.VMEM((tm, D_MODEL), BF16)],
        compiler_params=_params("arbitrary", "arbitrary"),
        name="inproj",
    )(x2, mods, mods, n1, w_in_bf, qn, kn, seg)


def _attn_kernel(*refs, ns, dil):
    q_ref = refs[0]
    k_refs = refs[1:1 + 2 * ns]
    v_refs = refs[1 + 2 * ns:1 + 4 * ns]
    o_ref, lse_ref = refs[1 + 4 * ns:]
    qrows = SPAN // ns
    nk = 2 * SPAN
    ph = pl.program_id(1) // dil
    jb = pl.program_id(2)
    q = q_ref[...].astype(BF16)
    k = jnp.concatenate([r[...] for r in k_refs], axis=0).astype(BF16)
    v = jnp.concatenate([r[...] for r in v_refs], axis=0).astype(BF16)
    qi = lax.broadcasted_iota(jnp.int32, (qrows, nk), 0)
    kr = lax.broadcasted_iota(jnp.int32, (qrows, nk), 1)
    kph = kr // (2 * qrows)
    kin = kr % (2 * qrows)
    cur = kin >= qrows
    mk = ns * (kin % qrows) + kph + jnp.where(cur, 0, -SPAN)
    delta = ns * qi + ph - mk
    valid = (delta >= 0) & (delta <= SPAN) & (cur | (jb > 0))
    outs, lses = [], []
    for h in range(ATT_HEADS):
        sl = slice(h * ATT_HEAD_DIM, (h + 1) * ATT_HEAD_DIM)
        s = jnp.where(valid, _nt_dot(q[:, sl], k[:, sl]), NEG_INF)
        m = jnp.max(s, axis=-1, keepdims=True)
        p = jnp.exp(s - m)
        l = jnp.sum(p, axis=-1, keepdims=True)
        o = jnp.dot(p.astype(BF16), v[:, sl], preferred_element_type=F32) / l
        outs.append(o)
        lses.append(jnp.broadcast_to(m + jnp.log(l), (qrows, ATT_HEAD_DIM)))
    o_ref[...] = jnp.concatenate(outs, axis=-1)
    lse_ref[...] = jnp.concatenate(lses, axis=-1)


def _attn_group(proj3, g, dil):
    b, s, _ = proj3.shape
    if dil == 1:
        ns, nstream, qrows = 1, 1, SPAN
        pv = proj3
        blk = (None, qrows, COL)
        imap = lambda cb, stream_of, prev: (
            lambda bi, i, jb: (bi, jnp.maximum(jb - 1, 0) if prev else jb, cb + g))
        out_shape = (b, s, ATT_OUT)
        omap = lambda bi, i, jb: (bi, jb, 0)
    else:
        nstream = max(dil, 8)
        ns = nstream // dil
        qrows = SPAN // ns
        pv = proj3.reshape(b, s // nstream, nstream, IN_WIDTH)
        blk = (None, qrows, None, COL)
        imap = lambda cb, stream_of, prev: (
            lambda bi, i, jb: (bi, jnp.maximum(jb - 1, 0) if prev else jb, stream_of(i), cb + g))
        out_shape = (b, s // nstream, nstream, ATT_OUT)
        omap = lambda bi, i, jb: (bi, jb, i, 0)
    nb = s // (nstream * qrows)

    def kv_specs(cb):
        specs = []
        for kph in range(ns):
            stream_of = lambda i, kph=kph: i % dil + dil * kph
            specs += [pl.BlockSpec(blk, imap(cb, stream_of, True)), pl.BlockSpec(blk, imap(cb, stream_of, False))]
        return specs

    in_specs = [pl.BlockSpec(blk, imap(CB_Q, lambda i: i, False))] + kv_specs(CB_K) + kv_specs(CB_V)
    out_spec = pl.BlockSpec(blk, omap)
    o, lse = pl.pallas_call(
        functools.partial(_attn_kernel, ns=ns, dil=dil),
        grid=(b, nstream, nb),
        in_specs=in_specs,
        out_specs=[out_spec, out_spec],
        out_shape=[jax.ShapeDtypeStruct(out_shape, F32)] * 2,
        compiler_params=_params("arbitrary", "arbitrary", "arbitrary"),
        name=f"attn_g{g}",
    )(*([pv] * len(in_specs)))
    return o.reshape(b * s, ATT_OUT), lse.reshape(b * s, ATT_OUT)


ATT_TILE = ATT_GROUPS[-1][1] * SPAN
PAIR = 2 * ATT_HEAD_DIM


def _class_rows(ref, start, n, dil):
    if dil == 1:
        return ref[pl.ds(start, n), :]
    return ref[pl.ds(start, n, stride=dil), :]


def _attn_fused_kernel(*refs):
    in_refs, (o_ref, og_scr, lg_scr) = refs[:5 * N_GROUPS], refs[5 * N_GROUPS:]
    tile = pl.program_id(1)
    lane = lax.broadcasted_iota(jnp.int32, (SPAN, PAIR), 1)
    lo = lane < ATT_HEAD_DIM
    qi = lax.broadcasted_iota(jnp.int32, (SPAN, 2 * SPAN), 0)
    kr = lax.broadcasted_iota(jnp.int32, (SPAN, 2 * SPAN), 1)
    delta = qi + SPAN - kr
    band = (delta >= 0) & (delta <= SPAN)
    cur_keys = kr >= SPAN

    def block(g, dil, r, jq, q_ref, k_ref, kp_ref, v_ref, vp_ref):
        q = _class_rows(q_ref, r + dil * SPAN * jq, SPAN, dil).astype(BF16)
        if jq == 0:
            k = jnp.concatenate([_class_rows(kp_ref, r, SPAN, dil), _class_rows(k_ref, r, SPAN, dil)], axis=0)
            v = jnp.concatenate([_class_rows(vp_ref, r, SPAN, dil), _class_rows(v_ref, r, SPAN, dil)], axis=0)
            valid = band & (cur_keys | (tile > 0))
        else:
            k = _class_rows(k_ref, r + dil * SPAN * (jq - 1), 2 * SPAN, dil)
            v = _class_rows(v_ref, r + dil * SPAN * (jq - 1), 2 * SPAN, dil)
            valid = band
        k = k.astype(BF16)
        v = v.astype(BF16)
        o_pair, lse_pair = None, None
        for first in (True, False):
            mine = lo if first else jnp.logical_not(lo)
            qh = jnp.where(mine, q, jnp.zeros_like(q))
            s = jnp.where(valid, _nt_dot(qh, k), NEG_INF)
            m = jnp.max(s, axis=-1, keepdims=True)
            p = jnp.exp(s - m)
            l = jnp.sum(p, axis=-1, keepdims=True)
            oh = jnp.dot(p.astype(BF16), v, preferred_element_type=F32) / l
            lse = m + jnp.log(l)
            o_pair = oh if first else jnp.where(lo, o_pair, oh)
            lse_pair = jnp.broadcast_to(lse, (SPAN, PAIR)) if first else jnp.where(lo, lse_pair, lse)
        start = r + dil * SPAN * jq
        if dil == 1:
            og_scr[g, pl.ds(start, SPAN), :] = o_pair
            lg_scr[g, pl.ds(start, SPAN), :] = lse_pair
        else:
            og_scr[g, pl.ds(start, SPAN, stride=dil), :] = o_pair
            lg_scr[g, pl.ds(start, SPAN, stride=dil), :] = lse_pair

    for g, (win, dil) in enumerate(ATT_GROUPS):
        grefs = in_refs[5 * g:5 * g + 5]
        nblk = ATT_TILE // (dil * SPAN)
        if dil == 1:
            for jq in range(nblk):
                block(g, dil, 0, jq, *grefs)
        else:
            def residue(r, carry, g=g, dil=dil, nblk=nblk, grefs=grefs):
                for jq in range(nblk):
                    block(g, dil, r, jq, *grefs)
                return carry
            lax.fori_loop(0, dil, residue, 0)

    lses = [lg_scr[g] for g in range(N_GROUPS)]
    mx = functools.reduce(jnp.maximum, lses)
    es = [jnp.exp(x - mx) for x in lses]
    num = es[0] * og_scr[0] + es[1] * og_scr[1] + es[2] * og_scr[2]
    o_ref[...] = (num / (es[0] + es[1] + es[2])).astype(o_ref.dtype)


def _attn_prompt(proj3):
    b, s, _ = proj3.shape
    npair = ATT_OUT // PAIR
    in_specs = []
    for g, (win, dil) in enumerate(ATT_GROUPS):
        prev_rows = dil * SPAN
        per_tile = ATT_TILE // prev_rows
        for cb in (CB_Q, CB_K, CB_V):
            col = lambda hp, cb=cb, g=g: (cb + g) * (COL // PAIR) + hp
            in_specs.append(pl.BlockSpec((None, ATT_TILE, PAIR), lambda bi, t, hp, col=col: (bi, t, col(hp))))
            if cb != CB_Q:
                in_specs.append(pl.BlockSpec(
                    (None, prev_rows, PAIR),
                    lambda bi, t, hp, col=col, per_tile=per_tile: (bi, jnp.maximum(t * per_tile - 1, 0), col(hp))))
    att = pl.pallas_call(
        _attn_fused_kernel,
        grid=(b, s // ATT_TILE, npair),
        in_specs=in_specs,
        out_specs=pl.BlockSpec((None, ATT_TILE, PAIR), lambda bi, t, hp: (bi, t, hp)),
        out_shape=jax.ShapeDtypeStruct((b, s, ATT_OUT), BF16),
        scratch_shapes=[pltpu.VMEM((N_GROUPS, ATT_TILE, PAIR), F32), pltpu.VMEM((N_GROUPS, ATT_TILE, PAIR), F32)],
        compiler_params=_params("arbitrary", "arbitrary", "arbitrary"),
        name="attn_prompt",
    )(*([proj3] * len(in_specs)))
    return att.reshape(b * s, ATT_OUT)


def _lower_bound(logits):
    mx = jnp.max(logits, axis=0, keepdims=True)
    e = jnp.exp(logits - mx)
    return e[0:1] / jnp.sum(e, axis=0, keepdims=True)


def _hgrn_kernel(qh_ref, fh_ref, ih_ref, gh_ref, lbl_ref, hn_ref, tri_ref, o_ref, st_ref,
                 st_scr, q_scr, k_scr, b_scr, *, tb):
    t = pl.program_id(2)
    nh = COL // HG_DK
    c_rows = HG_CHUNK

    @pl.when(t == 0)
    def _():
        st_scr[...] = jnp.zeros_like(st_scr)

    lb = _lower_bound(lbl_ref[...])
    f = lb + (1.0 - lb) * jax.nn.sigmoid(fh_ref[...])
    k_scr[...] = 1.0 - f
    q_scr[...] = _silu(qh_ref[...])
    logf = jnp.log(f)
    hi = logf.astype(BF16)
    rest = logf - hi.astype(F32)
    mid = rest.astype(BF16)
    low = (rest - mid.astype(F32)).astype(BF16)
    for r in range(tb // 128):
        rows = slice(r * 128, (r + 1) * 128)
        pieces = jnp.concatenate([hi[rows, :], mid[rows, :], low[rows, :]], axis=0)
        b_scr[rows, :] = jnp.dot(tri_ref[...], pieces, preferred_element_type=F32)
    rowid = lax.broadcasted_iota(jnp.int32, (c_rows, 1), 0)

    def chunk(c, carry):
        r0 = pl.multiple_of(c * c_rows, c_rows)
        rows = pl.ds(r0, c_rows)
        for h in range(nh):
            cols = slice(h * HG_DK, (h + 1) * HG_DK)
            b = b_scr[rows, cols]
            qc = q_scr[rows, cols]
            kc = k_scr[rows, cols]
            vc = ih_ref[rows, cols]
            st = st_scr[h]
            o = _nt_dot((qc * jnp.exp(b)).astype(BF16), st.astype(BF16))
            for s in range(c_rows):
                e = jnp.exp(jnp.minimum(b - b[s:s + 1], 0.0))
                a = jnp.sum(qc * kc[s:s + 1] * e, axis=-1, keepdims=True)
                o = o + jnp.where(rowid >= s, a, 0.0) * vc[s:s + 1]
            bl = b[c_rows - 1:c_rows]
            kt = kc * jnp.exp(bl - b)
            upd = lax.dot_general(vc.astype(BF16), kt.astype(BF16), (((0,), (0,)), ((), ())),
                                  preferred_element_type=F32)
            st_scr[h] = st * jnp.exp(bl) + upd
            ms = jnp.mean(o * o, axis=-1, keepdims=True)
            on = o * lax.rsqrt(ms + EPS) * hn_ref[...]
            o_ref[rows, cols] = on * _silu(gh_ref[rows, cols])
        return carry

    lax.fori_loop(0, tb // c_rows, chunk, 0, unroll=2)

    @pl.when(t == pl.num_programs(2) - 1)
    def _():
        for h in range(nh):
            st_ref[h] = st_scr[h].T


def _hgrn_prompt(proj3, lb_logits, hn, tri, tb):
    b, s, _ = proj3.shape
    nhb = HG_HEADS * HG_DK // COL
    nh = COL // HG_DK

    def spec(cb):
        return pl.BlockSpec((None, tb, COL), lambda bi, hb, t: (bi, t, cb + hb))

    o, st = pl.pallas_call(
        functools.partial(_hgrn_kernel, tb=tb),
        grid=(b, nhb, s // tb),
        in_specs=[spec(CB_QH), spec(CB_FH), spec(CB_IH), spec(CB_GH),
                  pl.BlockSpec((lb_logits.shape[0], COL), lambda bi, hb, t: (0, hb)),
                  pl.BlockSpec((1, HG_DV), lambda bi, hb, t: (0, 0)),
                  pl.BlockSpec(tri.shape, lambda bi, hb, t: (0, 0))],
        out_specs=[pl.BlockSpec((None, tb, COL), lambda bi, hb, t: (bi, t, hb)),
                   pl.BlockSpec((None, nh, HG_DK, HG_DV), lambda bi, hb, t: (bi, hb, 0, 0))],
        out_shape=[jax.ShapeDtypeStruct((b, s, HG_HEADS * HG_DV), F32),
                   jax.ShapeDtypeStruct((b, HG_HEADS, HG_DK, HG_DV), F32)],
        scratch_shapes=[pltpu.VMEM((nh, HG_DV, HG_DK), F32),
                        pltpu.VMEM((tb, COL), F32), pltpu.VMEM((tb, COL), F32), pltpu.VMEM((tb, COL), F32)],
        compiler_params=_params("arbitrary", "arbitrary", "arbitrary"),
        name="hgrn_prompt",
    )(proj3, proj3, proj3, proj3, lb_logits, hn, tri)
    return o.reshape(b * s, HG_HEADS * HG_DV), st


def _mix_kernel(*refs, n_att):
    n_refs = 1 if n_att == 1 else 2 * n_att
    att_refs = refs[:n_refs]
    (hg_ref, ga0, ga1, gb0, gb1, x_ref, g1_ref, sc2_ref, sh2_ref, n2_ref,
     wa_ref, wb_ref, wo_ref, x1_ref, h2_ref) = refs[n_refs:]
    if n_att == 1:
        att = att_refs[0][...]
    else:
        lses = [att_refs[2 * g + 1][...] for g in range(n_att)]
        mx = functools.reduce(jnp.maximum, lses)
        es = [jnp.exp(l - mx) for l in lses]
        den = functools.reduce(lambda a, b: a + b, es)
        num = functools.reduce(lambda a, b: a + b, [es[g] * att_refs[2 * g][...] for g in range(n_att)])
        att = num / den
    ga = jnp.concatenate([ga0[...], ga1[...]], axis=-1)
    gb = jnp.concatenate([gb0[...], gb1[...]], axis=-1)
    ya = jnp.dot(att.astype(BF16), wa_ref[...], preferred_element_type=F32)
    yb = jnp.dot(hg_ref[...].astype(BF16), wb_ref[...], preferred_element_type=F32)
    y = jax.nn.sigmoid(ga) * ya + jax.nn.sigmoid(gb) * yb
    x1 = x_ref[...] + g1_ref[...] * jnp.dot(y.astype(BF16), wo_ref[...], preferred_element_type=F32)
    x1_ref[...] = x1
    ms = jnp.mean(x1 * x1, axis=-1, keepdims=True)
    xn = x1 * lax.rsqrt(ms + EPS) * n2_ref[...]
    h2_ref[...] = (xn * (1.0 + sc2_ref[...]) + sh2_ref[...]).astype(BF16)


def _mix(att_list, hg, proj2, x2, mods, per_row, rows_per_batch, tm, n2, wa, wb, wo):
    n = x2.shape[0]
    row = lambda w: pl.BlockSpec((tm, w), lambda i: (i, 0))
    colblk = lambda cb: pl.BlockSpec((tm, COL), lambda i: (i, cb))
    full = lambda a: pl.BlockSpec(a.shape, lambda i: (0, 0))
    in_specs = ([row(ATT_OUT)] * len(att_list)
                + [row(D_MODEL), colblk(CB_GA), colblk(CB_GA + 1), colblk(CB_GB), colblk(CB_GB + 1), row(D_MODEL),
                   _mod_spec(per_row, tm, rows_per_batch, 2), _mod_spec(per_row, tm, rows_per_batch, 4),
                   _mod_spec(per_row, tm, rows_per_batch, 3), full(n2), full(wa), full(wb), full(wo)])
    n_att = 1 if len(att_list) == 1 else len(att_list) // 2
    return pl.pallas_call(
        functools.partial(_mix_kernel, n_att=n_att),
        grid=(n // tm,),
        in_specs=in_specs,
        out_specs=[row(D_MODEL), row(D_MODEL)],
        out_shape=[jax.ShapeDtypeStruct((n, D_MODEL), F32), jax.ShapeDtypeStruct((n, D_MODEL), BF16)],
        compiler_params=_params("arbitrary"),
        name="mix",
    )(*att_list, hg, proj2, proj2, proj2, proj2, x2, mods, mods, mods, n2, wa, wb, wo)


TOK_LANES = 128
CAND_ROWS = ((0, 16), (1, 16), (2, 8), (3, 8), (4, 8), (5, 8), (6, 8), (7, 8))


def _top16_rows(s, ridx, sentinel):
    vals, idxs = [], []
    for _ in range(PEER_TOPK):
        m = jnp.max(s, axis=0, keepdims=True)
        am = jnp.min(jnp.where(s == m, ridx, sentinel), axis=0, keepdims=True)
        vals.append(m)
        idxs.append(am)
        s = jnp.where(ridx == am, NEG_INF, s)
    return jnp.concatenate(vals, axis=0), jnp.concatenate(idxs, axis=0)


def _route_kernel(h2_ref, wq_ref, sk_ref, a_ref, b_ref, g_ref, q_scr, at_scr, bt_scr, gt_scr):
    tm = h2_ref.shape[0]
    tt = TOK_LANES
    q = jnp.dot(h2_ref[...], wq_ref[...], preferred_element_type=F32)
    for hp in range(2 * PEER_HEADS):
        q_scr[hp] = q[:, hp * PEER_HALF:(hp + 1) * PEER_HALF].astype(BF16)
    kidx = lax.broadcasted_iota(jnp.int32, (PEER_KEYS, tt), 0)
    sub16 = lax.broadcasted_iota(jnp.int32, (PEER_TOPK, tt), 0)
    sub8 = lax.broadcasted_iota(jnp.int32, (8, tt), 0)
    cflat = jnp.concatenate([p * PEER_TOPK + (sub16 if nq == 16 else sub8) for p, nq in CAND_ROWS]
                            + [(sub8 + 8) * PEER_TOPK], axis=0)

    def group(gi, carry):
        rows = pl.ds(pl.multiple_of(gi * tt, tt), tt)

        def sub_key_top(h):
            v1, i1 = _top16_rows(_nt_dot(sk_ref[0], q_scr[2 * h, rows, :]), kidx, PEER_KEYS)
            v2, i2 = _top16_rows(_nt_dot(sk_ref[1], q_scr[2 * h + 1, rows, :]), kidx, PEER_KEYS)
            return v1, i1, v2, i2

        def select(h, tops):
            v1, i1, v2, i2 = tops
            cand = jnp.concatenate([v1[p:p + 1] + v2[0:nq] for p, nq in CAND_ROWS] + [v1[8:16] + v2[0:1]], axis=0)
            tv, tp = _top16_rows(cand, cflat, PEER_TOPK * PEER_TOPK)
            e = jnp.exp(tv - tv[0:1])
            g = e / jnp.sum(e, axis=0, keepdims=True)
            pr = tp >> 4
            qr = tp & 15
            a_sel = jnp.zeros((PEER_TOPK, tt), jnp.int32)
            b_sel = jnp.zeros((PEER_TOPK, tt), jnp.int32)
            for p in range(PEER_TOPK):
                a_sel = jnp.where(pr == p, i1[p:p + 1], a_sel)
                b_sel = jnp.where(qr == p, i2[p:p + 1], b_sel)
            first = h * PEER_TOPK if isinstance(h, int) else pl.multiple_of(h * PEER_TOPK, PEER_TOPK)
            slot = pl.ds(first, PEER_TOPK)
            at_scr[slot, :] = a_sel
            bt_scr[slot, :] = b_sel
            gt_scr[slot, :] = g

        def head(h, tops):
            select(h, tops)
            return sub_key_top(h + 1)

        select(PEER_HEADS - 1, lax.fori_loop(0, PEER_HEADS - 1, head, sub_key_top(0)))
        a_ref[rows, :] = at_scr[...].T
        b_ref[rows, :] = bt_scr[...].T
        g_ref[rows, :] = gt_scr[...].T
        return carry

    lax.fori_loop(0, tm // tt, group, 0)


def _route(h2, wq_bf, sk_bf, tm):
    n = h2.shape[0]
    assert tm % TOK_LANES == 0 and n % tm == 0
    row = pl.BlockSpec((tm, 128), lambda i: (i, 0))
    slots = PEER_HEADS * PEER_TOPK
    return pl.pallas_call(
        _route_kernel,
        grid=(n // tm,),
        in_specs=[pl.BlockSpec((tm, D_MODEL), lambda i: (i, 0)),
                  pl.BlockSpec(wq_bf.shape, lambda i: (0, 0)),
                  pl.BlockSpec(sk_bf.shape, lambda i: (0, 0, 0))],
        out_specs=[row, row, row],
        out_shape=[jax.ShapeDtypeStruct((n, 128), jnp.int32), jax.ShapeDtypeStruct((n, 128), jnp.int32),
                   jax.ShapeDtypeStruct((n, 128), F32)],
        scratch_shapes=[pltpu.VMEM((2 * PEER_HEADS, tm, PEER_HALF), BF16),
                        pltpu.VMEM((slots, TOK_LANES), jnp.int32), pltpu.VMEM((slots, TOK_LANES), jnp.int32),
                        pltpu.VMEM((slots, TOK_LANES), F32)],
        compiler_params=_params("arbitrary"),
        name="peer_route",
    )(h2, wq_bf, sk_bf)


def _peer_u_kernel(h2_ref, u_ref, a_ref, b_ref, g_ref, w_ref, act_scr, *, ac):
    c = pl.program_id(1)

    @pl.when(c == 0)
    def _():
        act_scr[...] = jnp.zeros_like(act_scr)

    hc = _nt_dot(h2_ref[...], u_ref[...])
    a_idx = a_ref[...]
    b_idx = b_ref[...]
    act = act_scr[...]
    for i in range(ac):
        gathered = jnp.take_along_axis(hc[:, i * 128:(i + 1) * 128], b_idx, axis=1)
        act = jnp.where(a_idx == c * ac + i, gathered, act)
    act_scr[...] = act

    @pl.when(c == pl.num_programs(1) - 1)
    def _():
        w_ref[...] = g_ref[...] * _gelu_tanh(act)


def _peer_u(h2, u_bf, a_idx, b_idx, gate, tm, ac):
    n = h2.shape[0]
    row = pl.BlockSpec((tm, 128), lambda i, c: (i, 0))
    return pl.pallas_call(
        functools.partial(_peer_u_kernel, ac=ac),
        grid=(n // tm, PEER_KEYS // ac),
        in_specs=[pl.BlockSpec((tm, D_MODEL), lambda i, c: (i, 0)),
                  pl.BlockSpec((ac * 128, D_MODEL), lambda i, c: (c, 0)),
                  row, row, row],
        out_specs=row,
        out_shape=jax.ShapeDtypeStruct((n, 128), F32),
        scratch_shapes=[pltpu.VMEM((tm, 128), F32)],
        compiler_params=_params("arbitrary", "arbitrary"),
        name="peer_u",
    )(h2, u_bf, a_idx, b_idx, gate)


def _peer_v_kernel(a_ref, b_ref, w_ref, v_ref, x1_ref, g2_ref, o_ref, w3_scr, acc_scr, *, ac, tm):
    c = pl.program_id(1)

    @pl.when(c == 0)
    def _():
        sub = lax.broadcasted_iota(jnp.int32, (128, 128), 0)

        def build(n, carry):
            ar = a_ref[pl.ds(n, 1), :]
            br = b_ref[pl.ds(n, 1), :]
            wr = w_ref[pl.ds(n, 1), :]
            at = jnp.where(sub == ar, 1.0, 0.0).astype(BF16)
            rt = jnp.where(sub == br, wr, 0.0).astype(BF16)
            w3_scr[pl.ds(pl.multiple_of(n * W_PITCH, 8), 128), :] = _nt_dot(at, rt)
            return carry

        lax.fori_loop(0, tm, build, 0, unroll=32)

    part = None
    for i in range(0, ac, 2):
        lhs = jnp.concatenate([w3_scr[pl.ds(c * ac + i, tm, stride=W_PITCH), :],
                               w3_scr[pl.ds(c * ac + i + 1, tm, stride=W_PITCH), :]], axis=-1)
        d = jnp.dot(lhs.astype(BF16), v_ref[i * 128:(i + 2) * 128, :], preferred_element_type=F32)
        part = d if part is None else part + d

    @pl.when(c == 0)
    def _():
        acc_scr[...] = part

    @pl.when(c > 0)
    def _():
        acc_scr[...] += part

    @pl.when(c == pl.num_programs(1) - 1)
    def _():
        o_ref[...] = x1_ref[...] + g2_ref[...] * acc_scr[...]


def _peer_v(a_idx, b_idx, wts, v_bf, x1, mods, per_row, rows_per_batch, tm, ac):
    n = x1.shape[0]
    row = pl.BlockSpec((tm, 128), lambda i, c: (i, 0))
    wide = pl.BlockSpec((tm, D_MODEL), lambda i, c: (i, 0))
    return pl.pallas_call(
        functools.partial(_peer_v_kernel, ac=ac, tm=tm),
        grid=(n // tm, PEER_KEYS // ac),
        in_specs=[row, row, row,
                  pl.BlockSpec((ac * 128, D_MODEL), lambda i, c: (c, 0)),
                  wide, _mod_spec(per_row, tm, rows_per_batch, 5)],
        out_specs=wide,
        out_shape=jax.ShapeDtypeStruct((n, D_MODEL), F32),
        scratch_shapes=[pltpu.VMEM((tm * W_PITCH, 128), F32), pltpu.VMEM((tm, D_MODEL), F32)],
        compiler_params=_params("arbitrary", "arbitrary"),
        name="peer_v",
    )(a_idx, b_idx, wts, v_bf, x1, mods)


def _decode_attn_kernel(q_ref, k_ref, v_ref, c0_ref, c1_ref, c2_ref, seg_ref, segt_ref, o_ref, *, bt):
    caches = (c0_ref, c1_ref, c2_ref)
    hi = lax.Precision.HIGHEST
    for i in range(bt):
        ms, ls, accs = [], [], []
        for g in range(N_GROUPS):
            cols = slice(g * ATT_OUT, (g + 1) * ATT_OUT)
            q = q_ref[i:i + 1, cols]
            kn = k_ref[i:i + 1, cols]
            vn = v_ref[i:i + 1, cols]
            kt = caches[g][i, :, 0:ATT_OUT]
            vt = caches[g][i, :, ATT_OUT:2 * ATT_OUT]
            s = jnp.dot(kt * q, seg_ref[...], precision=hi, preferred_element_type=F32)
            s0 = jnp.dot(kn * q, seg_ref[...], precision=hi, preferred_element_type=F32)
            m = jnp.maximum(jnp.max(s, axis=0, keepdims=True), s0)
            p = jnp.exp(s - m)
            p0 = jnp.exp(s0 - m)
            l = jnp.sum(p, axis=0, keepdims=True) + p0
            pe = jnp.dot(p, segt_ref[...], precision=hi, preferred_element_type=F32)
            p0e = jnp.dot(p0, segt_ref[...], precision=hi, preferred_element_type=F32)
            acc = jnp.sum(pe * vt, axis=0, keepdims=True) + p0e * vn
            ms.append(jnp.dot(m, segt_ref[...], precision=hi, preferred_element_type=F32))
            ls.append(jnp.dot(l, segt_ref[...], precision=hi, preferred_element_type=F32))
            accs.append(acc)
        lses = [ms[g] + jnp.log(ls[g]) for g in range(N_GROUPS)]
        mx = functools.reduce(jnp.maximum, lses)
        es = [jnp.exp(x - mx) for x in lses]
        den = es[0] + es[1] + es[2]
        num = es[0] * (accs[0] / ls[0]) + es[1] * (accs[1] / ls[1]) + es[2] * (accs[2] / ls[2])
        o_ref[i:i + 1, :] = num / den


def _decode_attn(proj_s, caches, seg8, seg8t, bt):
    n = proj_s.shape[0]
    views, specs = [], []
    for (win, dil), cache in zip(ATT_GROUPS, caches):
        length = cache.shape[1]
        views.append(cache.reshape(n, length // dil, dil * 2 * ATT_OUT))
        specs.append(pl.BlockSpec((bt, length // dil, 2 * ATT_OUT), lambda i: (i, 0, 0)))
    blk = lambda cb: pl.BlockSpec((bt, N_GROUPS * ATT_OUT), lambda i: (i, cb))
    return pl.pallas_call(
        functools.partial(_decode_attn_kernel, bt=bt),
        grid=(n // bt,),
        in_specs=[blk(0), blk(1), blk(2)] + specs
                 + [pl.BlockSpec(seg8.shape, lambda i: (0, 0)), pl.BlockSpec(seg8t.shape, lambda i: (0, 0))],
        out_specs=pl.BlockSpec((bt, ATT_OUT), lambda i: (i, 0)),
        out_shape=jax.ShapeDtypeStruct((n, ATT_OUT), F32),
        compiler_params=_params("arbitrary"),
        name="decode_attn",
    )(proj_s, proj_s, proj_s, *views, seg8, seg8t)


def _decode_hgrn_kernel(q0, q1, f0, f1, i0, i1, g0, g1, lbl_ref, hn_ref, st_ref, o_ref, sto_ref, *, bt):
    nh = COL // HG_DK
    lb = _lower_bound(lbl_ref[...])
    for h in range(HG_HEADS):
        qr, fr, ir, gr = ((q0, f0, i0, g0), (q1, f1, i1, g1))[h // nh]
        cols = slice((h % nh) * HG_DK, (h % nh + 1) * HG_DK)
        lbh = lb[:, h * HG_DK:(h + 1) * HG_DK]
        f = lbh + (1.0 - lbh) * jax.nn.sigmoid(fr[:, cols])
        ft = f.T
        kt = 1.0 - ft
        qt = _silu(qr[:, cols]).T
        for i in range(bt):
            v = ir[i:i + 1, cols]
            s_new = ft[:, i:i + 1] * st_ref[i, h] + kt[:, i:i + 1] * v
            sto_ref[i, h] = s_new
            o = jnp.sum(qt[:, i:i + 1] * s_new, axis=0, keepdims=True)
            ms = jnp.mean(o * o, axis=-1, keepdims=True)
            o_ref[i:i + 1, h * HG_DV:(h + 1) * HG_DV] = (o * lax.rsqrt(ms + EPS) * hn_ref[...]
                                                         * _silu(gr[i:i + 1, cols]))


def _decode_hgrn(proj_s, state, lb_logits, hn, bt):
    n = proj_s.shape[0]
    st_spec = pl.BlockSpec((bt, HG_HEADS, HG_DK, HG_DV), lambda i: (i, 0, 0, 0))
    half = lambda cb: pl.BlockSpec((bt, COL), lambda i: (i, cb))
    return pl.pallas_call(
        functools.partial(_decode_hgrn_kernel, bt=bt),
        grid=(n // bt,),
        in_specs=[half(CB_QH), half(CB_QH + 1), half(CB_FH), half(CB_FH + 1), half(CB_IH), half(CB_IH + 1),
                  half(CB_GH), half(CB_GH + 1),
                  pl.BlockSpec(lb_logits.shape, lambda i: (0, 0)),
                  pl.BlockSpec((1, HG_DV), lambda i: (0, 0)),
                  st_spec],
        out_specs=[pl.BlockSpec((bt, HG_HEADS * HG_DV), lambda i: (i, 0)), st_spec],
        out_shape=[jax.ShapeDtypeStruct((n, HG_HEADS * HG_DV), F32),
                   jax.ShapeDtypeStruct(state.shape, F32)],
        compiler_params=_params("arbitrary"),
        name="decode_hgrn",
    )(proj_s, proj_s, proj_s, proj_s, proj_s, proj_s, proj_s, proj_s, lb_logits, hn, state)


def _block_diag_ones(n, seg):
    i = jnp.arange(n)
    return (i[:, None] // seg == i[None, :] // seg)


def _peer(h2, x1, mods, per_row, rows_per_batch, wq_bf, sk_bf, u_bf, v_bf):
    n = h2.shape[0]
    tm_u, ac_u = min(512, n), 16
    tm_v, ac_v = min(256, n), 32
    a_idx, b_idx, gate = _route(h2, wq_bf, sk_bf, min(256, n))
    wts = _peer_u(h2, u_bf, a_idx, b_idx, gate, tm_u, ac_u)
    return _peer_v(a_idx, b_idx, wts, v_bf, x1, mods, per_row, rows_per_batch, tm_v, ac_v)


def _kv_rows_kernel(k_ref, v_ref, o_ref):
    for h in range(ATT_HEADS):
        cols = slice(h * ATT_HEAD_DIM, (h + 1) * ATT_HEAD_DIM)
        o_ref[:, 0, h, :] = k_ref[:, cols]
        o_ref[:, 1, h, :] = v_ref[:, cols]


def _kv_rows(proj3, g, rows):
    b, s, _ = proj3.shape
    tr = min(256, rows)
    assert rows % tr == 0 and (s - rows) % tr == 0
    first = (s - rows) // tr
    return pl.pallas_call(
        _kv_rows_kernel,
        grid=(b, rows // tr),
        in_specs=[pl.BlockSpec((None, tr, COL), lambda bi, t: (bi, first + t, CB_K + g)),
                  pl.BlockSpec((None, tr, COL), lambda bi, t: (bi, first + t, CB_V + g))],
        out_specs=pl.BlockSpec((None, tr, 2, ATT_HEADS, ATT_HEAD_DIM), lambda bi, t: (bi, t, 0, 0, 0)),
        out_shape=jax.ShapeDtypeStruct((b, rows, 2, ATT_HEADS, ATT_HEAD_DIM), F32),
        compiler_params=_params("arbitrary", "arbitrary"),
        name=f"kv_rows_g{g}",
    )(proj3, proj3)


def kernel(x_prompt, x_sample, cache_kv_w128, cache_kv_w512, cache_kv_w2048, state_hgrn, c_prompt, c_sample, w_ada, b_ada, norm1_w, norm2_w, w_in, q_norm_w, k_norm_w, hg_lb_logits, hg_norm_w, w_br_a, w_br_b, w_o, w_peer_q, peer_subkeys, peer_u, peer_v):
    bsz, seq, _ = x_prompt.shape
    dec, dec_t, _ = x_sample.shape
    assert w_ada.shape[0] == 1 and dec_t == 1 and seq % (ATT_GROUPS[-1][1] * SPAN) == 0
    for (win, dil), cache in zip(ATT_GROUPS, (cache_kv_w128, cache_kv_w512, cache_kv_w2048)):
        assert win == dil * SPAN and cache.shape[2] == win

    w_ada_bf = w_ada[0].astype(BF16)
    w_in_bf = w_in[0].astype(BF16)
    wa, wb, wo = w_br_a[0].astype(BF16), w_br_b[0].astype(BF16), w_o[0].astype(BF16)
    wq_bf = w_peer_q[0].astype(BF16)
    sk_bf = peer_subkeys[0].astype(BF16)
    u_bf = peer_u[0].astype(BF16)
    v_bf = peer_v[0].astype(BF16)
    n1 = norm1_w[0].reshape(1, D_MODEL)
    n2 = norm2_w[0].reshape(1, D_MODEL)
    qn = jnp.tile(q_norm_w[0], N_GROUPS * ATT_HEADS).reshape(1, ATT_WIDTH)
    kn = jnp.tile(k_norm_w[0], N_GROUPS * ATT_HEADS).reshape(1, ATT_WIDTH)
    hn = hg_norm_w[0].reshape(1, HG_DV)
    seg = _block_diag_ones(MXU_TILE, ATT_HEAD_DIM).astype(BF16)
    tri = _block_diag_ones(128, HG_CHUNK) & (jnp.arange(128)[:, None] >= jnp.arange(128)[None, :])
    tri = jnp.tile(tri.astype(BF16), (1, 3))
    head_of_col = jnp.arange(ATT_OUT)[:, None] // ATT_HEAD_DIM
    seg8 = (head_of_col == jnp.arange(128)[None, :]).astype(F32)
    seg8t = seg8.T

    mods = _mods(jnp.concatenate([c_prompt, c_sample], axis=0), w_ada_bf, b_ada)
    mods_p = mods[:bsz].reshape(bsz, 1, 6 * D_MODEL)
    mods_s = mods[bsz:]

    n_p = bsz * seq
    tm_p = 512
    xp2 = x_prompt.reshape(n_p, D_MODEL)
    proj_p = _inproj(xp2, mods_p, False, seq, 1024, n1, w_in_bf, qn, kn, seg)
    proj_p3 = proj_p.reshape(bsz, seq, IN_WIDTH)
    att_p = _attn_prompt(proj_p3)
    hg_p, st_p = _hgrn_prompt(proj_p3, hg_lb_logits, hn, tri, 256)
    x1_p, h2_p = _mix([att_p], hg_p, proj_p, xp2, mods_p, False, seq, tm_p, n2, wa, wb, wo)
    y_p = _peer(h2_p, x1_p, mods_p, False, seq, wq_bf, sk_bf, u_bf, v_bf)

    xs2 = x_sample.reshape(dec, D_MODEL)
    proj_s = _inproj(xs2, mods_s, True, 1, dec, n1, w_in_bf, qn, kn, seg)
    att_s = _decode_attn(proj_s, (cache_kv_w128[0], cache_kv_w512[0], cache_kv_w2048[0]), seg8, seg8t, 8)
    hg_s, st_s = _decode_hgrn(proj_s, state_hgrn[0], hg_lb_logits, hn, 8)
    x1_s, h2_s = _mix([att_s], hg_s, proj_s, xs2, mods_s, True, 1, dec, n2, wa, wb, wo)
    y_s = _peer(h2_s, x1_s, mods_s, True, 1, wq_bf, sk_bf, u_bf, v_bf)

    kv_p = [_kv_rows(proj_p3, g, min(win, seq))[None] for g, (win, _) in enumerate(ATT_GROUPS)]
    proj_s3 = proj_s.reshape(1, dec, IN_WIDTH)
    kv_s = [_kv_rows(proj_s3, g, dec).reshape(1, dec, 1, 2, ATT_HEADS, ATT_HEAD_DIM) for g in range(N_GROUPS)]
    return (y_p.reshape(bsz, seq, D_MODEL), y_s.reshape(dec, 1, D_MODEL), kv_p[0], kv_p[1], kv_p[2], st_p[None],
            kv_s[0], kv_s[1], kv_s[2], st_s[None])
```

```python
import functools

import jax
import jax.numpy as jnp
from jax import lax
from jax.experimental import pallas as pl
from jax.experimental.pallas import tpu as pltpu

F32 = jnp.float32
BF16 = jnp.bfloat16

D_MODEL = 1024
ATT_GROUPS = ((128, 1), (512, 4), (2048, 16))
N_GROUPS = 3
ATT_HEADS = 8
ATT_HEAD_DIM = 64
ATT_OUT = ATT_HEADS * ATT_HEAD_DIM
ATT_WIDTH = N_GROUPS * ATT_OUT
MXU_TILE = 256
SPAN = 128
HG_HEADS = 8
HG_DK = 128
HG_DV = 128
PEER_HEADS = 8
PEER_KEYS = 128
PEER_TOPK = 16
PEER_HALF = 128
EPS = 1e-6
IN_WIDTH = 10752
COL = 512
NCOL = IN_WIDTH // COL
CB_Q, CB_K, CB_V = 0, 3, 6
CB_QH, CB_FH, CB_IH, CB_GH, CB_GA, CB_GB = 9, 11, 13, 15, 17, 19
HG_CHUNK = 16
W_PITCH = 136
VMEM_LIMIT = 56 * 1024 * 1024

NEG_INF = float("-inf")


def _silu(x):
    return x * jax.nn.sigmoid(x)


def _gelu_tanh(x):
    return 0.5 * x * (1.0 + jnp.tanh(0.7978845608028654 * (x + 0.044715 * (x * x * x))))


def _nt_dot(a, b):
    return lax.dot_general(a, b, (((1,), (1,)), ((), ())), preferred_element_type=F32)


def _params(*sem):
    return pltpu.CompilerParams(dimension_semantics=sem, vmem_limit_bytes=VMEM_LIMIT)


def _mods_kernel(c_ref, w_ref, b_ref, o_ref):
    s = _silu(c_ref[...])
    o_ref[...] = jnp.dot(s.astype(BF16), w_ref[...], preferred_element_type=F32) + b_ref[...]


def _mods(c, w_ada_bf, b_ada):
    n = c.shape[0]
    return pl.pallas_call(
        _mods_kernel,
        grid=(6,),
        in_specs=[pl.BlockSpec((n, D_MODEL), lambda j: (0, 0)),
                  pl.BlockSpec((D_MODEL, D_MODEL), lambda j: (0, j)),
                  pl.BlockSpec((1, D_MODEL), lambda j: (0, j))],
        out_specs=pl.BlockSpec((n, D_MODEL), lambda j: (0, j)),
        out_shape=jax.ShapeDtypeStruct((n, 6 * D_MODEL), F32),
        compiler_params=_params("arbitrary"),
        name="mods",
    )(c, w_ada_bf, b_ada)


def _mod_spec(per_row, tm, rows_per_batch, k):
    if per_row:
        return pl.BlockSpec((tm, D_MODEL), lambda i, *_: (i, k))
    tiles = rows_per_batch // tm
    return pl.BlockSpec((None, 1, D_MODEL), lambda i, *_: (i // tiles, 0, k))


def _inproj_kernel(x_ref, sc_ref, sh_ref, n1_ref, w_ref, qn_ref, kn_ref, seg_ref, o_ref, h_scr):
    j = pl.program_id(1)

    @pl.when(j == 0)
    def _():
        x = x_ref[...]
        ms = jnp.mean(x * x, axis=-1, keepdims=True)
        xn = x * lax.rsqrt(ms + EPS) * n1_ref[...]
        h_scr[...] = (xn * (1.0 + sc_ref[...]) + sh_ref[...]).astype(BF16)

    acc = jnp.dot(h_scr[...], w_ref[...], preferred_element_type=F32)

    def head_norm(w_row, scale):
        sq = (acc * acc).astype(BF16)
        seg_w = seg_ref.shape[0]
        ss = jnp.concatenate([jnp.dot(sq[:, c:c + seg_w], seg_ref[...], preferred_element_type=F32)
                              for c in range(0, ATT_WIDTH, seg_w)], axis=-1)
        return acc * lax.rsqrt(ss * (1.0 / ATT_HEAD_DIM) + EPS) * w_row * scale

    @pl.when(j == 0)
    def _():
        o_ref[...] = head_norm(qn_ref[...], ATT_HEAD_DIM ** -0.5)

    @pl.when(j == 1)
    def _():
        o_ref[...] = head_norm(kn_ref[...], 1.0)

    @pl.when(j >= 2)
    def _():
        o_ref[...] = acc


def _inproj(x2, mods, per_row, rows_per_batch, tm, n1, w_in_bf, qn, kn, seg):
    n = x2.shape[0]
    const = lambda i, j: (0, 0)
    return pl.pallas_call(
        _inproj_kernel,
        grid=(n // tm, IN_WIDTH // ATT_WIDTH),
        in_specs=[pl.BlockSpec((tm, D_MODEL), lambda i, j: (i, 0)),
                  _mod_spec(per_row, tm, rows_per_batch, 1),
                  _mod_spec(per_row, tm, rows_per_batch, 0),
                  pl.BlockSpec((1, D_MODEL), const),
                  pl.BlockSpec((D_MODEL, ATT_WIDTH), lambda i, j: (0, j)),
                  pl.BlockSpec((1, ATT_WIDTH), const),
                  pl.BlockSpec((1, ATT_WIDTH), const),
                  pl.BlockSpec(seg.shape, const)],
        out_specs=pl.BlockSpec((tm, ATT_WIDTH), lambda i, j: (i, j)),
        out_shape=jax.ShapeDtypeStruct((n, IN_WIDTH), F32),
        scratch_shapes=[pltpu.VMEM((tm, D_MODEL), BF16)],
        compiler_params=_params("arbitrary", "arbitrary"),
        name="inproj",
    )(x2, mods, mods, n1, w_in_bf, qn, kn, seg)


ATT_TILE = ATT_GROUPS[-1][1] * SPAN
PAIR = 2 * ATT_HEAD_DIM


def _class_rows(ref, start, n, dil):
    if dil == 1:
        return ref[pl.ds(start, n), :]
    return ref[pl.ds(start, n, stride=dil), :]


def _attn_fused_kernel(*refs):
    in_refs, (o_ref, og_scr, lg_scr) = refs[:5 * N_GROUPS], refs[5 * N_GROUPS:]
    tile = pl.program_id(1)
    lane = lax.broadcasted_iota(jnp.int32, (SPAN, PAIR), 1)
    lo = lane < ATT_HEAD_DIM
    qi = lax.broadcasted_iota(jnp.int32, (SPAN, 2 * SPAN), 0)
    kr = lax.broadcasted_iota(jnp.int32, (SPAN, 2 * SPAN), 1)
    delta = qi + SPAN - kr
    band = (delta >= 0) & (delta <= SPAN)
    cur_keys = kr >= SPAN

    def block(g, dil, r, jq, q_ref, k_ref, kp_ref, v_ref, vp_ref):
        q = _class_rows(q_ref, r + dil * SPAN * jq, SPAN, dil).astype(BF16)
        if jq == 0:
            k = jnp.concatenate([_class_rows(kp_ref, r, SPAN, dil), _class_rows(k_ref, r, SPAN, dil)], axis=0)
            v = jnp.concatenate([_class_rows(vp_ref, r, SPAN, dil), _class_rows(v_ref, r, SPAN, dil)], axis=0)
            valid = band & (cur_keys | (tile > 0))
        else:
            k = _class_rows(k_ref, r + dil * SPAN * (jq - 1), 2 * SPAN, dil)
            v = _class_rows(v_ref, r + dil * SPAN * (jq - 1), 2 * SPAN, dil)
            valid = band
        k = k.astype(BF16)
        v = v.astype(BF16)
        o_pair, lse_pair = None, None
        for first in (True, False):
            mine = lo if first else jnp.logical_not(lo)
            qh = jnp.where(mine, q, jnp.zeros_like(q))
            s = jnp.where(valid, _nt_dot(qh, k), NEG_INF)
            m = jnp.max(s, axis=-1, keepdims=True)
            p = jnp.exp(s - m)
            l = jnp.sum(p, axis=-1, keepdims=True)
            oh = jnp.dot(p.astype(BF16), v, preferred_element_type=F32) / l
            lse = m + jnp.log(l)
            o_pair = oh if first else jnp.where(lo, o_pair, oh)
            lse_pair = jnp.broadcast_to(lse, (SPAN, PAIR)) if first else jnp.where(lo, lse_pair, lse)
        start = r + dil * SPAN * jq
        if dil == 1:
            og_scr[g, pl.ds(start, SPAN), :] = o_pair
            lg_scr[g, pl.ds(start, SPAN), :] = lse_pair
        else:
            og_scr[g, pl.ds(start, SPAN, stride=dil), :] = o_pair
            lg_scr[g, pl.ds(start, SPAN, stride=dil), :] = lse_pair

    for g, (win, dil) in enumerate(ATT_GROUPS):
        grefs = in_refs[5 * g:5 * g + 5]
        nblk = ATT_TILE // (dil * SPAN)
        if dil == 1:
            for jq in range(nblk):
                block(g, dil, 0, jq, *grefs)
        else:
            def residue(r, carry, g=g, dil=dil, nblk=nblk, grefs=grefs):
                for jq in range(nblk):
                    block(g, dil, r, jq, *grefs)
                return carry
            lax.fori_loop(0, dil, residue, 0, unroll=max(1, 2 // nblk))

    lses = [lg_scr[g] for g in range(N_GROUPS)]
    mx = functools.reduce(jnp.maximum, lses)
    es = [jnp.exp(x - mx) for x in lses]
    num = es[0] * og_scr[0] + es[1] * og_scr[1] + es[2] * og_scr[2]
    o_ref[...] = (num / (es[0] + es[1] + es[2])).astype(o_ref.dtype)


def _attn_prompt(proj3):
    b, s, _ = proj3.shape
    npair = ATT_OUT // PAIR
    in_specs = []
    for g, (win, dil) in enumerate(ATT_GROUPS):
        prev_rows = dil * SPAN
        per_tile = ATT_TILE // prev_rows
        for cb in (CB_Q, CB_K, CB_V):
            col = lambda hp, cb=cb, g=g: (cb + g) * (COL // PAIR) + hp
            in_specs.append(pl.BlockSpec((None, ATT_TILE, PAIR), lambda bi, t, hp, col=col: (bi, t, col(hp))))
            if cb != CB_Q:
                in_specs.append(pl.BlockSpec(
                    (None, prev_rows, PAIR),
                    lambda bi, t, hp, col=col, per_tile=per_tile: (bi, jnp.maximum(t * per_tile - 1, 0), col(hp))))
    att = pl.pallas_call(
        _attn_fused_kernel,
        grid=(b, s // ATT_TILE, npair),
        in_specs=in_specs,
        out_specs=pl.BlockSpec((None, ATT_TILE, PAIR), lambda bi, t, hp: (bi, t, hp)),
        out_shape=jax.ShapeDtypeStruct((b, s, ATT_OUT), BF16),
        scratch_shapes=[pltpu.VMEM((N_GROUPS, ATT_TILE, PAIR), F32), pltpu.VMEM((N_GROUPS, ATT_TILE, PAIR), F32)],
        compiler_params=_params("arbitrary", "arbitrary", "arbitrary"),
        name="attn_prompt",
    )(*([proj3] * len(in_specs)))
    return att.reshape(b * s, ATT_OUT)


def _lower_bound(logits):
    mx = jnp.max(logits, axis=0, keepdims=True)
    e = jnp.exp(logits - mx)
    return e[0:1] / jnp.sum(e, axis=0, keepdims=True)


def _hgrn_kernel(qh_ref, fh_ref, ih_ref, gh_ref, lbl_ref, hn_ref, tri_ref, o_ref, st_ref,
                 st_scr, q_scr, k_scr, b_scr, *, tb):
    t = pl.program_id(2)
    nh = COL // HG_DK
    c_rows = HG_CHUNK

    @pl.when(t == 0)
    def _():
        st_scr[...] = jnp.zeros_like(st_scr)

    lb = _lower_bound(lbl_ref[...])
    f = lb + (1.0 - lb) * jax.nn.sigmoid(fh_ref[...])
    k_scr[...] = 1.0 - f
    q_scr[...] = _silu(qh_ref[...])
    logf = jnp.log(f)
    hi = logf.astype(BF16)
    rest = logf - hi.astype(F32)
    mid = rest.astype(BF16)
    low = (rest - mid.astype(F32)).astype(BF16)
    for r in range(tb // 128):
        rows = slice(r * 128, (r + 1) * 128)
        pieces = jnp.concatenate([hi[rows, :], mid[rows, :], low[rows, :]], axis=0)
        b_scr[rows, :] = jnp.dot(tri_ref[...], pieces, preferred_element_type=F32)
    rowid8 = lax.broadcasted_iota(jnp.int32, (8, 1), 0)

    def chunk(c, carry):
        r0 = pl.multiple_of(c * c_rows, c_rows)
        rows = pl.ds(r0, c_rows)
        for h in range(nh):
            cols = slice(h * HG_DK, (h + 1) * HG_DK)
            b = b_scr[rows, cols]
            qc = q_scr[rows, cols]
            kc = k_scr[rows, cols]
            vc = ih_ref[rows, cols]
            st = st_scr[h]
            o = _nt_dot((qc * jnp.exp(b)).astype(BF16), st.astype(BF16))
            parts = []
            for g8 in range(c_rows // 8):
                tr = slice(g8 * 8, (g8 + 1) * 8)
                bg, qg, og = b[tr], qc[tr], o[tr]
                for s in range(g8 * 8 + 8):
                    d = bg - b[s:s + 1]
                    if s >= g8 * 8:
                        d = jnp.where(rowid8 >= s - g8 * 8, d, NEG_INF)
                    a = jnp.sum(qg * kc[s:s + 1] * jnp.exp(d), axis=-1, keepdims=True)
                    og = og + a * vc[s:s + 1]
                parts.append(og)
            o = jnp.concatenate(parts, axis=0)
            bl = b[c_rows - 1:c_rows]
            kt = kc * jnp.exp(bl - b)
            upd = lax.dot_general(vc.astype(BF16), kt.astype(BF16), (((0,), (0,)), ((), ())),
                                  preferred_element_type=F32)
            st_scr[h] = st * jnp.exp(bl) + upd
            ms = jnp.mean(o * o, axis=-1, keepdims=True)
            on = o * lax.rsqrt(ms + EPS) * hn_ref[...]
            o_ref[rows, cols] = on * _silu(gh_ref[rows, cols])
        return carry

    lax.fori_loop(0, tb // c_rows, chunk, 0, unroll=2)

    @pl.when(t == pl.num_programs(2) - 1)
    def _():
        for h in range(nh):
            st_ref[h] = st_scr[h].T


def _hgrn_prompt(proj3, lb_logits, hn, tri, tb):
    b, s, _ = proj3.shape
    nhb = HG_HEADS * HG_DK // COL
    nh = COL // HG_DK

    def spec(cb):
        return pl.BlockSpec((None, tb, COL), lambda bi, hb, t: (bi, t, cb + hb))

    o, st = pl.pallas_call(
        functools.partial(_hgrn_kernel, tb=tb),
        grid=(b, nhb, s // tb),
        in_specs=[spec(CB_QH), spec(CB_FH), spec(CB_IH), spec(CB_GH),
                  pl.BlockSpec((lb_logits.shape[0], COL), lambda bi, hb, t: (0, hb)),
                  pl.BlockSpec((1, HG_DV), lambda bi, hb, t: (0, 0)),
                  pl.BlockSpec(tri.shape, lambda bi, hb, t: (0, 0))],
        out_specs=[pl.BlockSpec((None, tb, COL), lambda bi, hb, t: (bi, t, hb)),
                   pl.BlockSpec((None, nh, HG_DK, HG_DV), lambda bi, hb, t: (bi, hb, 0, 0))],
        out_shape=[jax.ShapeDtypeStruct((b, s, HG_HEADS * HG_DV), F32),
                   jax.ShapeDtypeStruct((b, HG_HEADS, HG_DK, HG_DV), F32)],
        scratch_shapes=[pltpu.VMEM((nh, HG_DV, HG_DK), F32),
                        pltpu.VMEM((tb, COL), F32), pltpu.VMEM((tb, COL), F32), pltpu.VMEM((tb, COL), F32)],
        compiler_params=_params("arbitrary", "arbitrary", "arbitrary"),
        name="hgrn_prompt",
    )(proj3, proj3, proj3, proj3, lb_logits, hn, tri)
    return o.reshape(b * s, HG_HEADS * HG_DV), st


def _mix_kernel(*refs, n_att):
    n_refs = 1 if n_att == 1 else 2 * n_att
    att_refs = refs[:n_refs]
    (hg_ref, ga0, ga1, gb0, gb1, x_ref, g1_ref, sc2_ref, sh2_ref, n2_ref,
     wa_ref, wb_ref, wo_ref, x1_ref, h2_ref) = refs[n_refs:]
    if n_att == 1:
        att = att_refs[0][...]
    else:
        lses = [att_refs[2 * g + 1][...] for g in range(n_att)]
        mx = functools.reduce(jnp.maximum, lses)
        es = [jnp.exp(l - mx) for l in lses]
        den = functools.reduce(lambda a, b: a + b, es)
        num = functools.reduce(lambda a, b: a + b, [es[g] * att_refs[2 * g][...] for g in range(n_att)])
        att = num / den
    ga = jnp.concatenate([ga0[...], ga1[...]], axis=-1)
    gb = jnp.concatenate([gb0[...], gb1[...]], axis=-1)
    ya = jnp.dot(att.astype(BF16), wa_ref[...], preferred_element_type=F32)
    yb = jnp.dot(hg_ref[...].astype(BF16), wb_ref[...], preferred_element_type=F32)
    y = jax.nn.sigmoid(ga) * ya + jax.nn.sigmoid(gb) * yb
    x1 = x_ref[...] + g1_ref[...] * jnp.dot(y.astype(BF16), wo_ref[...], preferred_element_type=F32)
    x1_ref[...] = x1
    ms = jnp.mean(x1 * x1, axis=-1, keepdims=True)
    xn = x1 * lax.rsqrt(ms + EPS) * n2_ref[...]
    h2_ref[...] = (xn * (1.0 + sc2_ref[...]) + sh2_ref[...]).astype(BF16)


def _mix(att_list, hg, proj2, x2, mods, per_row, rows_per_batch, tm, n2, wa, wb, wo):
    n = x2.shape[0]
    row = lambda w: pl.BlockSpec((tm, w), lambda i: (i, 0))
    colblk = lambda cb: pl.BlockSpec((tm, COL), lambda i: (i, cb))
    full = lambda a: pl.BlockSpec(a.shape, lambda i: (0, 0))
    in_specs = ([row(ATT_OUT)] * len(att_list)
                + [row(D_MODEL), colblk(CB_GA), colblk(CB_GA + 1), colblk(CB_GB), colblk(CB_GB + 1), row(D_MODEL),
                   _mod_spec(per_row, tm, rows_per_batch, 2), _mod_spec(per_row, tm, rows_per_batch, 4),
                   _mod_spec(per_row, tm, rows_per_batch, 3), full(n2), full(wa), full(wb), full(wo)])
    n_att = 1 if len(att_list) == 1 else len(att_list) // 2
    return pl.pallas_call(
        functools.partial(_mix_kernel, n_att=n_att),
        grid=(n // tm,),
        in_specs=in_specs,
        out_specs=[row(D_MODEL), row(D_MODEL)],
        out_shape=[jax.ShapeDtypeStruct((n, D_MODEL), F32), jax.ShapeDtypeStruct((n, D_MODEL), BF16)],
        compiler_params=_params("arbitrary"),
        name="mix",
    )(*att_list, hg, proj2, proj2, proj2, proj2, x2, mods, mods, mods, n2, wa, wb, wo)


TOK_LANES = 128
CAND_ROWS = ((0, 16), (1, 16), (2, 8), (3, 8), (4, 8), (5, 8), (6, 8), (7, 8))


def _top16_rows(s, ridx, sentinel):
    vals, idxs = [], []
    for _ in range(PEER_TOPK):
        m = jnp.max(s, axis=0, keepdims=True)
        am = jnp.min(jnp.where(s == m, ridx, sentinel), axis=0, keepdims=True)
        vals.append(m)
        idxs.append(am)
        s = jnp.where(ridx == am, NEG_INF, s)
    return jnp.concatenate(vals, axis=0), jnp.concatenate(idxs, axis=0)


def _route_kernel(h2_ref, wq_ref, sk_ref, a_ref, b_ref, g_ref, q_scr, at_scr, bt_scr, gt_scr):
    tm = h2_ref.shape[0]
    tt = TOK_LANES
    q = jnp.dot(h2_ref[...], wq_ref[...], preferred_element_type=F32)
    for hp in range(2 * PEER_HEADS):
        q_scr[hp] = q[:, hp * PEER_HALF:(hp + 1) * PEER_HALF].astype(BF16)
    kidx = lax.broadcasted_iota(jnp.int32, (PEER_KEYS, tt), 0)
    sub16 = lax.broadcasted_iota(jnp.int32, (PEER_TOPK, tt), 0)
    sub8 = lax.broadcasted_iota(jnp.int32, (8, tt), 0)
    cflat = jnp.concatenate([p * PEER_TOPK + (sub16 if nq == 16 else sub8) for p, nq in CAND_ROWS]
                            + [(sub8 + 8) * PEER_TOPK], axis=0)

    def group(gi, carry):
        rows = pl.ds(pl.multiple_of(gi * tt, tt), tt)

        def sub_key_top(h):
            v1, i1 = _top16_rows(_nt_dot(sk_ref[0], q_scr[2 * h, rows, :]), kidx, PEER_KEYS)
            v2, i2 = _top16_rows(_nt_dot(sk_ref[1], q_scr[2 * h + 1, rows, :]), kidx, PEER_KEYS)
            return v1, i1, v2, i2

        def select(h, tops):
            v1, i1, v2, i2 = tops
            cand = jnp.concatenate([v1[p:p + 1] + v2[0:nq] for p, nq in CAND_ROWS] + [v1[8:16] + v2[0:1]], axis=0)
            tv, tp = _top16_rows(cand, cflat, PEER_TOPK * PEER_TOPK)
            e = jnp.exp(tv - tv[0:1])
            g = e / jnp.sum(e, axis=0, keepdims=True)
            pr = tp >> 4
            qr = tp & 15
            a_sel = jnp.zeros((PEER_TOPK, tt), jnp.int32)
            b_sel = jnp.zeros((PEER_TOPK, tt), jnp.int32)
            for p in range(PEER_TOPK):
                a_sel = jnp.where(pr == p, i1[p:p + 1], a_sel)
                b_sel = jnp.where(qr == p, i2[p:p + 1], b_sel)
            first = h * PEER_TOPK if isinstance(h, int) else pl.multiple_of(h * PEER_TOPK, PEER_TOPK)
            slot = pl.ds(first, PEER_TOPK)
            at_scr[slot, :] = a_sel
            bt_scr[slot, :] = b_sel
            gt_scr[slot, :] = g

        def head(h, tops):
            select(h, tops)
            return sub_key_top(h + 1)

        select(PEER_HEADS - 1, lax.fori_loop(0, PEER_HEADS - 1, head, sub_key_top(0)))
        a_ref[rows, :] = at_scr[...].T
        b_ref[rows, :] = bt_scr[...].T
        g_ref[rows, :] = gt_scr[...].T
        return carry

    lax.fori_loop(0, tm // tt, group, 0)


def _route(h2, wq_bf, sk_bf, tm):
    n = h2.shape[0]
    assert tm % TOK_LANES == 0 and n % tm == 0
    row = pl.BlockSpec((tm, 128), lambda i: (i, 0))
    slots = PEER_HEADS * PEER_TOPK
    return pl.pallas_call(
        _route_kernel,
        grid=(n // tm,),
        in_specs=[pl.BlockSpec((tm, D_MODEL), lambda i: (i, 0)),
                  pl.BlockSpec(wq_bf.shape, lambda i: (0, 0)),
                  pl.BlockSpec(sk_bf.shape, lambda i: (0, 0, 0))],
        out_specs=[row, row, row],
        out_shape=[jax.ShapeDtypeStruct((n, 128), jnp.int32), jax.ShapeDtypeStruct((n, 128), jnp.int32),
                   jax.ShapeDtypeStruct((n, 128), F32)],
        scratch_shapes=[pltpu.VMEM((2 * PEER_HEADS, tm, PEER_HALF), BF16),
                        pltpu.VMEM((slots, TOK_LANES), jnp.int32), pltpu.VMEM((slots, TOK_LANES), jnp.int32),
                        pltpu.VMEM((slots, TOK_LANES), F32)],
        compiler_params=_params("arbitrary"),
        name="peer_route",
    )(h2, wq_bf, sk_bf)


def _peer_u_kernel(h2_ref, u_ref, a_ref, b_ref, g_ref, w_ref, act_scr, *, ac):
    c = pl.program_id(1)

    @pl.when(c == 0)
    def _():
        act_scr[...] = jnp.zeros_like(act_scr)

    hc = _nt_dot(h2_ref[...], u_ref[...])
    a_idx = a_ref[...]
    b_idx = b_ref[...]
    act = act_scr[...]
    for i in range(ac):
        gathered = jnp.take_along_axis(hc[:, i * 128:(i + 1) * 128], b_idx, axis=1)
        act = jnp.where(a_idx == c * ac + i, gathered, act)
    act_scr[...] = act

    @pl.when(c == pl.num_programs(1) - 1)
    def _():
        w_ref[...] = g_ref[...] * _gelu_tanh(act)


def _peer_u(h2, u_bf, a_idx, b_idx, gate, tm, ac):
    n = h2.shape[0]
    row = pl.BlockSpec((tm, 128), lambda i, c: (i, 0))
    return pl.pallas_call(
        functools.partial(_peer_u_kernel, ac=ac),
        grid=(n // tm, PEER_KEYS // ac),
        in_specs=[pl.BlockSpec((tm, D_MODEL), lambda i, c: (i, 0)),
                  pl.BlockSpec((ac * 128, D_MODEL), lambda i, c: (c, 0)),
                  row, row, row],
        out_specs=row,
        out_shape=jax.ShapeDtypeStruct((n, 128), F32),
        scratch_shapes=[pltpu.VMEM((tm, 128), F32)],
        compiler_params=_params("arbitrary", "arbitrary"),
        name="peer_u",
    )(h2, u_bf, a_idx, b_idx, gate)


def _peer_v_kernel(a_ref, b_ref, w_ref, v_ref, x1_ref, g2_ref, o_ref, w3_scr, acc_scr, *, ac, tm):
    c = pl.program_id(1)

    @pl.when(c == 0)
    def _():
        sub = lax.broadcasted_iota(jnp.int32, (128, 128), 0)

        def build(n, carry):
            ar = a_ref[pl.ds(n, 1), :]
            br = b_ref[pl.ds(n, 1), :]
            wr = w_ref[pl.ds(n, 1), :]
            at = jnp.where(sub == ar, 1.0, 0.0).astype(BF16)
            rt = jnp.where(sub == br, wr, 0.0).astype(BF16)
            w3_scr[pl.ds(pl.multiple_of(n * W_PITCH, 8), 128), :] = _nt_dot(at, rt)
            return carry

        lax.fori_loop(0, tm, build, 0, unroll=32)

    part = None
    for i in range(0, ac, 2):
        lhs = jnp.concatenate([w3_scr[pl.ds(c * ac + i, tm, stride=W_PITCH), :],
                               w3_scr[pl.ds(c * ac + i + 1, tm, stride=W_PITCH), :]], axis=-1)
        d = jnp.dot(lhs.astype(BF16), v_ref[i * 128:(i + 2) * 128, :], preferred_element_type=F32)
        part = d if part is None else part + d

    @pl.when(c == 0)
    def _():
        acc_scr[...] = part

    @pl.when(c > 0)
    def _():
        acc_scr[...] += part

    @pl.when(c == pl.num_programs(1) - 1)
    def _():
        o_ref[...] = x1_ref[...] + g2_ref[...] * acc_scr[...]


def _peer_v(a_idx, b_idx, wts, v_bf, x1, mods, per_row, rows_per_batch, tm, ac):
    n = x1.shape[0]
    row = pl.BlockSpec((tm, 128), lambda i, c: (i, 0))
    wide = pl.BlockSpec((tm, D_MODEL), lambda i, c: (i, 0))
    return pl.pallas_call(
        functools.partial(_peer_v_kernel, ac=ac, tm=tm),
        grid=(n // tm, PEER_KEYS // ac),
        in_specs=[row, row, row,
                  pl.BlockSpec((ac * 128, D_MODEL), lambda i, c: (c, 0)),
                  wide, _mod_spec(per_row, tm, rows_per_batch, 5)],
        out_specs=wide,
        out_shape=jax.ShapeDtypeStruct((n, D_MODEL), F32),
        scratch_shapes=[pltpu.VMEM((tm * W_PITCH, 128), F32), pltpu.VMEM((tm, D_MODEL), F32)],
        compiler_params=_params("arbitrary", "arbitrary"),
        name="peer_v",
    )(a_idx, b_idx, wts, v_bf, x1, mods)


def _decode_attn_kernel(q_ref, k_ref, v_ref, c0_ref, c1_ref, c2_ref, o_ref, *, bt):
    caches = (c0_ref, c1_ref, c2_ref)
    for i in range(bt):
        lses, outs = [], []
        for g in range(N_GROUPS):
            q = q_ref[i, g]
            kn = k_ref[i, g]
            vn = v_ref[i, g]
            kt = caches[g][i, :, 0]
            vt = caches[g][i, :, 1]
            s = jnp.sum(kt * q[None], axis=-1, keepdims=True)
            s0 = jnp.sum(kn * q, axis=-1, keepdims=True)
            m = jnp.maximum(jnp.max(s, axis=0), s0)
            p = jnp.exp(s - m[None])
            p0 = jnp.exp(s0 - m)
            l = jnp.sum(p, axis=0) + p0
            outs.append((jnp.sum(p * vt, axis=0) + p0 * vn) / l)
            lses.append(m + jnp.log(l))
        mx = functools.reduce(jnp.maximum, lses)
        es = [jnp.exp(x - mx) for x in lses]
        o_ref[i] = (es[0] * outs[0] + es[1] * outs[1] + es[2] * outs[2]) / (es[0] + es[1] + es[2])


def _decode_attn(proj_s, caches, bt):
    n = proj_s.shape[0]
    heads4 = lambda cb: proj_s[:, cb * COL:cb * COL + ATT_WIDTH].reshape(n, N_GROUPS, ATT_HEADS, ATT_HEAD_DIM)
    views, specs = [], []
    for (win, dil), cache in zip(ATT_GROUPS, caches):
        length = cache.shape[1]
        views.append(cache.reshape(n, length // dil, dil, 2, ATT_HEADS, ATT_HEAD_DIM))
        specs.append(pl.BlockSpec((bt, length // dil, None, 2, ATT_HEADS, ATT_HEAD_DIM),
                                  lambda i: (i, 0, 0, 0, 0, 0)))
    qkv_spec = pl.BlockSpec((bt, N_GROUPS, ATT_HEADS, ATT_HEAD_DIM), lambda i: (i, 0, 0, 0))
    att = pl.pallas_call(
        functools.partial(_decode_attn_kernel, bt=bt),
        grid=(n // bt,),
        in_specs=[qkv_spec, qkv_spec, qkv_spec] + specs,
        out_specs=pl.BlockSpec((bt, ATT_HEADS, ATT_HEAD_DIM), lambda i: (i, 0, 0)),
        out_shape=jax.ShapeDtypeStruct((n, ATT_HEADS, ATT_HEAD_DIM), F32),
        compiler_params=_params("arbitrary"),
        name="decode_attn",
    )(heads4(CB_Q), heads4(CB_K), heads4(CB_V), *views)
    return att.reshape(n, ATT_OUT)


def _decode_hgrn_kernel(q0, q1, f0, f1, i0, i1, g0, g1, lbl_ref, hn_ref, st_ref, o_ref, sto_ref, *, bt):
    nh = COL // HG_DK
    lb = _lower_bound(lbl_ref[...])
    for h in range(HG_HEADS):
        qr, fr, ir, gr = ((q0, f0, i0, g0), (q1, f1, i1, g1))[h // nh]
        cols = slice((h % nh) * HG_DK, (h % nh + 1) * HG_DK)
        lbh = lb[:, h * HG_DK:(h + 1) * HG_DK]
        f = lbh + (1.0 - lbh) * jax.nn.sigmoid(fr[:, cols])
        ft = f.T
        kt = 1.0 - ft
        qt = _silu(qr[:, cols]).T
        for i in range(bt):
            v = ir[i:i + 1, cols]
            s_new = ft[:, i:i + 1] * st_ref[i, h] + kt[:, i:i + 1] * v
            sto_ref[i, h] = s_new
            o = jnp.sum(qt[:, i:i + 1] * s_new, axis=0, keepdims=True)
            ms = jnp.mean(o * o, axis=-1, keepdims=True)
            o_ref[i:i + 1, h * HG_DV:(h + 1) * HG_DV] = (o * lax.rsqrt(ms + EPS) * hn_ref[...]
                                                         * _silu(gr[i:i + 1, cols]))


def _decode_hgrn(proj_s, state, lb_logits, hn, bt):
    n = proj_s.shape[0]
    st_spec = pl.BlockSpec((bt, HG_HEADS, HG_DK, HG_DV), lambda i: (i, 0, 0, 0))
    half = lambda cb: pl.BlockSpec((bt, COL), lambda i: (i, cb))
    return pl.pallas_call(
        functools.partial(_decode_hgrn_kernel, bt=bt),
        grid=(n // bt,),
        in_specs=[half(CB_QH), half(CB_QH + 1), half(CB_FH), half(CB_FH + 1), half(CB_IH), half(CB_IH + 1),
                  half(CB_GH), half(CB_GH + 1),
                  pl.BlockSpec(lb_logits.shape, lambda i: (0, 0)),
                  pl.BlockSpec((1, HG_DV), lambda i: (0, 0)),
                  st_spec],
        out_specs=[pl.BlockSpec((bt, HG_HEADS * HG_DV), lambda i: (i, 0)), st_spec],
        out_shape=[jax.ShapeDtypeStruct((n, HG_HEADS * HG_DV), F32),
                   jax.ShapeDtypeStruct(state.shape, F32)],
        compiler_params=_params("arbitrary"),
        name="decode_hgrn",
    )(proj_s, proj_s, proj_s, proj_s, proj_s, proj_s, proj_s, proj_s, lb_logits, hn, state)


def _block_diag_ones(n, seg):
    i = jnp.arange(n)
    return (i[:, None] // seg == i[None, :] // seg)


def _peer(h2, x1, mods, per_row, rows_per_batch, wq_bf, sk_bf, u_bf, v_bf):
    n = h2.shape[0]
    tm_u, ac_u = min(512, n), 16
    tm_v, ac_v = min(256, n), 32
    a_idx, b_idx, gate = _route(h2, wq_bf, sk_bf, min(256, n))
    wts = _peer_u(h2, u_bf, a_idx, b_idx, gate, tm_u, ac_u)
    return _peer_v(a_idx, b_idx, wts, v_bf, x1, mods, per_row, rows_per_batch, tm_v, ac_v)


def _kv_rows_kernel(k_ref, v_ref, o_ref):
    for h in range(ATT_HEADS):
        cols = slice(h * ATT_HEAD_DIM, (h + 1) * ATT_HEAD_DIM)
        o_ref[:, 0, h, :] = k_ref[:, cols]
        o_ref[:, 1, h, :] = v_ref[:, cols]


def _kv_rows(proj3, g, rows):
    b, s, _ = proj3.shape
    tr = min(256, rows)
    assert rows % tr == 0 and (s - rows) % tr == 0
    first = (s - rows) // tr
    return pl.pallas_call(
        _kv_rows_kernel,
        grid=(b, rows // tr),
        in_specs=[pl.BlockSpec((None, tr, COL), lambda bi, t: (bi, first + t, CB_K + g)),
                  pl.BlockSpec((None, tr, COL), lambda bi, t: (bi, first + t, CB_V + g))],
        out_specs=pl.BlockSpec((None, tr, 2, ATT_HEADS, ATT_HEAD_DIM), lambda bi, t: (bi, t, 0, 0, 0)),
        out_shape=jax.ShapeDtypeStruct((b, rows, 2, ATT_HEADS, ATT_HEAD_DIM), F32),
        compiler_params=_params("arbitrary", "arbitrary"),
        name=f"kv_rows_g{g}",
    )(proj3, proj3)


def kernel(x_prompt, x_sample, cache_kv_w128, cache_kv_w512, cache_kv_w2048, state_hgrn, c_prompt, c_sample, w_ada, b_ada, norm1_w, norm2_w, w_in, q_norm_w, k_norm_w, hg_lb_logits, hg_norm_w, w_br_a, w_br_b, w_o, w_peer_q, peer_subkeys, peer_u, peer_v):
    bsz, seq, _ = x_prompt.shape
    dec, dec_t, _ = x_sample.shape
    assert w_ada.shape[0] == 1 and dec_t == 1 and seq % (ATT_GROUPS[-1][1] * SPAN) == 0
    for (win, dil), cache in zip(ATT_GROUPS, (cache_kv_w128, cache_kv_w512, cache_kv_w2048)):
        assert win == dil * SPAN and cache.shape[2] == win

    w_ada_bf = w_ada[0].astype(BF16)
    w_in_bf = w_in[0].astype(BF16)
    wa, wb, wo = w_br_a[0].astype(BF16), w_br_b[0].astype(BF16), w_o[0].astype(BF16)
    wq_bf = w_peer_q[0].astype(BF16)
    sk_bf = peer_subkeys[0].astype(BF16)
    u_bf = peer_u[0].astype(BF16)
    v_bf = peer_v[0].astype(BF16)
    n1 = norm1_w[0].reshape(1, D_MODEL)
    n2 = norm2_w[0].reshape(1, D_MODEL)
    qn = jnp.tile(q_norm_w[0], N_GROUPS * ATT_HEADS).reshape(1, ATT_WIDTH)
    kn = jnp.tile(k_norm_w[0], N_GROUPS * ATT_HEADS).reshape(1, ATT_WIDTH)
    hn = hg_norm_w[0].reshape(1, HG_DV)
    seg = _block_diag_ones(MXU_TILE, ATT_HEAD_DIM).astype(BF16)
    tri = _block_diag_ones(128, HG_CHUNK) & (jnp.arange(128)[:, None] >= jnp.arange(128)[None, :])
    tri = jnp.tile(tri.astype(BF16), (1, 3))

    mods = _mods(jnp.concatenate([c_prompt, c_sample], axis=0), w_ada_bf, b_ada)
    mods_p = mods[:bsz].reshape(bsz, 1, 6 * D_MODEL)
    mods_s = mods[bsz:]

    n_p = bsz * seq
    tm_p = 512
    xp2 = x_prompt.reshape(n_p, D_MODEL)
    proj_p = _inproj(xp2, mods_p, False, seq, 1024, n1, w_in_bf, qn, kn, seg)
    proj_p3 = proj_p.reshape(bsz, seq, IN_WIDTH)
    att_p = _attn_prompt(proj_p3)
    hg_p, st_p = _hgrn_prompt(proj_p3, hg_lb_logits, hn, tri, 256)
    x1_p, h2_p = _mix([att_p], hg_p, proj_p, xp2, mods_p, False, seq, tm_p, n2, wa, wb, wo)
    y_p = _peer(h2_p, x1_p, mods_p, False, seq, wq_bf, sk_bf, u_bf, v_bf)

    xs2 = x_sample.reshape(dec, D_MODEL)
    proj_s = _inproj(xs2, mods_s, True, 1, dec, n1, w_in_bf, qn, kn, seg)
    att_s = _decode_attn(proj_s, (cache_kv_w128[0], cache_kv_w512[0], cache_kv_w2048[0]), 8)
    hg_s, st_s = _decode_hgrn(proj_s, state_hgrn[0], hg_lb_logits, hn, 8)
    x1_s, h2_s = _mix([att_s], hg_s, proj_s, xs2, mods_s, True, 1, dec, n2, wa, wb, wo)
    y_s = _peer(h2_s, x1_s, mods_s, True, 1, wq_bf, sk_bf, u_bf, v_bf)

    kv_p = [_kv_rows(proj_p3, g, min(win, seq))[None] for g, (win, _) in enumerate(ATT_GROUPS)]
    proj_s3 = proj_s.reshape(1, dec, IN_WIDTH)
    kv_s = [_kv_rows(proj_s3, g, dec).reshape(1, dec, 1, 2, ATT_HEADS, ATT_HEAD_DIM) for g in range(N_GROUPS)]
    return (y_p.reshape(bsz, seq, D_MODEL), y_s.reshape(dec, 1, D_MODEL), kv_p[0], kv_p[1], kv_p[2], st_p[None],
            kv_s[0], kv_s[1], kv_s[2], st_s[None])
```

```python
import functools

import jax
import jax.numpy as jnp
from jax import lax
from jax.experimental import pallas as pl
from jax.experimental.pallas import tpu as pltpu

F32 = jnp.float32
BF16 = jnp.bfloat16

D_MODEL = 1024
ATT_GROUPS = ((128, 1), (512, 4), (2048, 16))
N_GROUPS = 3
ATT_HEADS = 8
ATT_HEAD_DIM = 64
ATT_OUT = ATT_HEADS * ATT_HEAD_DIM
ATT_WIDTH = N_GROUPS * ATT_OUT
MXU_TILE = 256
SPAN = 128
HG_HEADS = 8
HG_DK = 128
HG_DV = 128
PEER_HEADS = 8
PEER_KEYS = 128
PEER_TOPK = 16
PEER_HALF = 128
EPS = 1e-6
IN_WIDTH = 10752
COL = 512
NCOL = IN_WIDTH // COL
CB_Q, CB_K, CB_V = 0, 3, 6
CB_QH, CB_FH, CB_IH, CB_GH, CB_GA, CB_GB = 9, 11, 13, 15, 17, 19
HG_CHUNK = 16
W_PITCH = 136
VMEM_LIMIT = 56 * 1024 * 1024

NEG_INF = float("-inf")


def _silu(x):
    return x * jax.nn.sigmoid(x)


def _gelu_tanh(x):
    return 0.5 * x * (1.0 + jnp.tanh(0.7978845608028654 * (x + 0.044715 * (x * x * x))))


def _nt_dot(a, b):
    return lax.dot_general(a, b, (((1,), (1,)), ((), ())), preferred_element_type=F32)


def _params(*sem):
    return pltpu.CompilerParams(dimension_semantics=sem, vmem_limit_bytes=VMEM_LIMIT)


def _mods_kernel(c_ref, w_ref, b_ref, o_ref):
    s = _silu(c_ref[...])
    o_ref[...] = jnp.dot(s.astype(BF16), w_ref[...], preferred_element_type=F32) + b_ref[...]


def _mods(c, w_ada_bf, b_ada):
    n = c.shape[0]
    return pl.pallas_call(
        _mods_kernel,
        grid=(6,),
        in_specs=[pl.BlockSpec((n, D_MODEL), lambda j: (0, 0)),
                  pl.BlockSpec((D_MODEL, D_MODEL), lambda j: (0, j)),
                  pl.BlockSpec((1, D_MODEL), lambda j: (0, j))],
        out_specs=pl.BlockSpec((n, D_MODEL), lambda j: (0, j)),
        out_shape=jax.ShapeDtypeStruct((n, 6 * D_MODEL), F32),
        compiler_params=_params("arbitrary"),
        name="mods",
    )(c, w_ada_bf, b_ada)


def _mod_spec(per_row, tm, rows_per_batch, k):
    if per_row:
        return pl.BlockSpec((tm, D_MODEL), lambda i, *_: (i, k))
    tiles = rows_per_batch // tm
    return pl.BlockSpec((None, 1, D_MODEL), lambda i, *_: (i // tiles, 0, k))


def _inproj_kernel(x_ref, sc_ref, sh_ref, n1_ref, w_ref, qn_ref, kn_ref, seg_ref, o_ref, h_scr):
    j = pl.program_id(1)

    @pl.when(j == 0)
    def _():
        x = x_ref[...]
        ms = jnp.mean(x * x, axis=-1, keepdims=True)
        xn = x * lax.rsqrt(ms + EPS) * n1_ref[...]
        h_scr[...] = (xn * (1.0 + sc_ref[...]) + sh_ref[...]).astype(BF16)

    acc = jnp.dot(h_scr[...], w_ref[...], preferred_element_type=F32)

    def head_norm(w_row, scale):
        sq = (acc * acc).astype(BF16)
        seg_w = seg_ref.shape[0]
        ss = jnp.concatenate([jnp.dot(sq[:, c:c + seg_w], seg_ref[...], preferred_element_type=F32)
                              for c in range(0, ATT_WIDTH, seg_w)], axis=-1)
        return acc * lax.rsqrt(ss * (1.0 / ATT_HEAD_DIM) + EPS) * w_row * scale

    @pl.when(j == 0)
    def _():
        o_ref[...] = head_norm(qn_ref[...], ATT_HEAD_DIM ** -0.5)

    @pl.when(j == 1)
    def _():
        o_ref[...] = head_norm(kn_ref[...], 1.0)

    @pl.when(j >= 2)
    def _():
        o_ref[...] = acc


def _inproj(x2, mods, per_row, rows_per_batch, tm, n1, w_in_bf, qn, kn, seg):
    n = x2.shape[0]
    const = lambda i, j: (0, 0)
    return pl.pallas_call(
        _inproj_kernel,
        grid=(n // tm, IN_WIDTH // ATT_WIDTH),
        in_specs=[pl.BlockSpec((tm, D_MODEL), lambda i, j: (i, 0)),
                  _mod_spec(per_row, tm, rows_per_batch, 1),
                  _mod_spec(per_row, tm, rows_per_batch, 0),
                  pl.BlockSpec((1, D_MODEL), const),
                  pl.BlockSpec((D_MODEL, ATT_WIDTH), lambda i, j: (0, j)),
                  pl.BlockSpec((1, ATT_WIDTH), const),
                  pl.BlockSpec((1, ATT_WIDTH), const),
                  pl.BlockSpec(seg.shape, const)],
        out_specs=pl.BlockSpec((tm, ATT_WIDTH), lambda i, j: (i, j)),
        out_shape=jax.ShapeDtypeStruct((n, IN_WIDTH), F32),
        scratch_shapes=[pltpu.VMEM((tm, D_MODEL), BF16)],
        compiler_params=_params("arbitrary", "arbitrary"),
        name="inproj",
    )(x2, mods, mods, n1, w_in_bf, qn, kn, seg)


ATT_TILE = ATT_GROUPS[-1][1] * SPAN
PAIR = 2 * ATT_HEAD_DIM


def _class_rows(ref, start, n, dil):
    if dil == 1:
        return ref[pl.ds(start, n), :]
    return ref[pl.ds(start, n, stride=dil), :]


def _attn_fused_kernel(*refs):
    in_refs, (o_ref, og_scr, lg_scr) = refs[:5 * N_GROUPS], refs[5 * N_GROUPS:]
    tile = pl.program_id(1)
    lane = lax.broadcasted_iota(jnp.int32, (SPAN, PAIR), 1)
    lo = lane < ATT_HEAD_DIM
    qi = lax.broadcasted_iota(jnp.int32, (SPAN, 2 * SPAN), 0)
    kr = lax.broadcasted_iota(jnp.int32, (SPAN, 2 * SPAN), 1)
    delta = qi + SPAN - kr
    band = (delta >= 0) & (delta <= SPAN)
    cur_keys = kr >= SPAN

    def block(g, dil, r, jq, q_ref, k_ref, kp_ref, v_ref, vp_ref):
        q = _class_rows(q_ref, r + dil * SPAN * jq, SPAN, dil).astype(BF16)
        if jq == 0:
            k = jnp.concatenate([_class_rows(kp_ref, r, SPAN, dil), _class_rows(k_ref, r, SPAN, dil)], axis=0)
            v = jnp.concatenate([_class_rows(vp_ref, r, SPAN, dil), _class_rows(v_ref, r, SPAN, dil)], axis=0)
            valid = band & (cur_keys | (tile > 0))
        else:
            k = _class_rows(k_ref, r + dil * SPAN * (jq - 1), 2 * SPAN, dil)
            v = _class_rows(v_ref, r + dil * SPAN * (jq - 1), 2 * SPAN, dil)
            valid = band
        k = k.astype(BF16)
        v = v.astype(BF16)
        o_pair, lse_pair = None, None
        for first in (True, False):
            mine = lo if first else jnp.logical_not(lo)
            qh = jnp.where(mine, q, jnp.zeros_like(q))
            s = jnp.where(valid, _nt_dot(qh, k), NEG_INF)
            m = jnp.max(s, axis=-1, keepdims=True)
            p = jnp.exp(s - m)
            l = jnp.sum(p, axis=-1, keepdims=True)
            oh = jnp.dot(p.astype(BF16), v, preferred_element_type=F32) / l
            lse = m + jnp.log(l)
            o_pair = oh if first else jnp.where(lo, o_pair, oh)
            lse_pair = jnp.broadcast_to(lse, (SPAN, PAIR)) if first else jnp.where(lo, lse_pair, lse)
        start = r + dil * SPAN * jq
        if dil == 1:
            og_scr[g, pl.ds(start, SPAN), :] = o_pair
            lg_scr[g, pl.ds(start, SPAN), :] = lse_pair
        else:
            og_scr[g, pl.ds(start, SPAN, stride=dil), :] = o_pair
            lg_scr[g, pl.ds(start, SPAN, stride=dil), :] = lse_pair

    for g, (win, dil) in enumerate(ATT_GROUPS):
        grefs = in_refs[5 * g:5 * g + 5]
        nblk = ATT_TILE // (dil * SPAN)
        if dil == 1:
            for jq in range(nblk):
                block(g, dil, 0, jq, *grefs)
        else:
            def residue(r, carry, g=g, dil=dil, nblk=nblk, grefs=grefs):
                for jq in range(nblk):
                    block(g, dil, r, jq, *grefs)
                return carry
            lax.fori_loop(0, dil, residue, 0, unroll=max(1, 2 // nblk))

    lses = [lg_scr[g] for g in range(N_GROUPS)]
    mx = functools.reduce(jnp.maximum, lses)
    es = [jnp.exp(x - mx) for x in lses]
    num = es[0] * og_scr[0] + es[1] * og_scr[1] + es[2] * og_scr[2]
    o_ref[...] = (num / (es[0] + es[1] + es[2])).astype(o_ref.dtype)


def _attn_prompt(proj3):
    b, s, _ = proj3.shape
    npair = ATT_OUT // PAIR
    in_specs = []
    for g, (win, dil) in enumerate(ATT_GROUPS):
        prev_rows = dil * SPAN
        per_tile = ATT_TILE // prev_rows
        for cb in (CB_Q, CB_K, CB_V):
            col = lambda hp, cb=cb, g=g: (cb + g) * (COL // PAIR) + hp
            in_specs.append(pl.BlockSpec((None, ATT_TILE, PAIR), lambda bi, t, hp, col=col: (bi, t, col(hp))))
            if cb != CB_Q:
                in_specs.append(pl.BlockSpec(
                    (None, prev_rows, PAIR),
                    lambda bi, t, hp, col=col, per_tile=per_tile: (bi, jnp.maximum(t * per_tile - 1, 0), col(hp))))
    att = pl.pallas_call(
        _attn_fused_kernel,
        grid=(b, s // ATT_TILE, npair),
        in_specs=in_specs,
        out_specs=pl.BlockSpec((None, ATT_TILE, PAIR), lambda bi, t, hp: (bi, t, hp)),
        out_shape=jax.ShapeDtypeStruct((b, s, ATT_OUT), BF16),
        scratch_shapes=[pltpu.VMEM((N_GROUPS, ATT_TILE, PAIR), F32), pltpu.VMEM((N_GROUPS, ATT_TILE, PAIR), F32)],
        compiler_params=_params("arbitrary", "arbitrary", "arbitrary"),
        name="attn_prompt",
    )(*([proj3] * len(in_specs)))
    return att.reshape(b * s, ATT_OUT)


def _lower_bound(logits):
    mx = jnp.max(logits, axis=0, keepdims=True)
    e = jnp.exp(logits - mx)
    return e[0:1] / jnp.sum(e, axis=0, keepdims=True)


def _hgrn_kernel(qh_ref, fh_ref, ih_ref, gh_ref, lbl_ref, hn_ref, tri_ref, o_ref, st_ref,
                 st_scr, q_scr, k_scr, b_scr, *, tb):
    t = pl.program_id(2)
    nh = COL // HG_DK
    c_rows = HG_CHUNK

    @pl.when(t == 0)
    def _():
        st_scr[...] = jnp.zeros_like(st_scr)

    lb = _lower_bound(lbl_ref[...])
    f = lb + (1.0 - lb) * jax.nn.sigmoid(fh_ref[...])
    k_scr[...] = 1.0 - f
    q_scr[...] = _silu(qh_ref[...])
    logf = jnp.log(f)
    hi = logf.astype(BF16)
    rest = logf - hi.astype(F32)
    mid = rest.astype(BF16)
    low = (rest - mid.astype(F32)).astype(BF16)
    for r in range(tb // 128):
        rows = slice(r * 128, (r + 1) * 128)
        pieces = jnp.concatenate([hi[rows, :], mid[rows, :], low[rows, :]], axis=0)
        b_scr[rows, :] = jnp.dot(tri_ref[...], pieces, preferred_element_type=F32)
    rowid8 = lax.broadcasted_iota(jnp.int32, (8, 1), 0)

    def chunk(c, carry):
        r0 = pl.multiple_of(c * c_rows, c_rows)
        rows = pl.ds(r0, c_rows)
        for h in range(nh):
            cols = slice(h * HG_DK, (h + 1) * HG_DK)
            b = b_scr[rows, cols]
            qc = q_scr[rows, cols]
            kc = k_scr[rows, cols]
            vc = ih_ref[rows, cols]
            st = st_scr[h]
            o = _nt_dot((qc * jnp.exp(b)).astype(BF16), st.astype(BF16))
            parts = []
            for g8 in range(c_rows // 8):
                tr = slice(g8 * 8, (g8 + 1) * 8)
                bg, qg, og = b[tr], qc[tr], o[tr]
                for s in range(g8 * 8 + 8):
                    d = bg - b[s:s + 1]
                    if s >= g8 * 8:
                        d = jnp.where(rowid8 >= s - g8 * 8, d, NEG_INF)
                    a = jnp.sum(qg * kc[s:s + 1] * jnp.exp(d), axis=-1, keepdims=True)
                    og = og + a * vc[s:s + 1]
                parts.append(og)
            o = jnp.concatenate(parts, axis=0)
            bl = b[c_rows - 1:c_rows]
            kt = kc * jnp.exp(bl - b)
            upd = lax.dot_general(vc.astype(BF16), kt.astype(BF16), (((0,), (0,)), ((), ())),
                                  preferred_element_type=F32)
            st_scr[h] = st * jnp.exp(bl) + upd
            ms = jnp.mean(o * o, axis=-1, keepdims=True)
            on = o * lax.rsqrt(ms + EPS) * hn_ref[...]
            o_ref[rows, cols] = on * _silu(gh_ref[rows, cols])
        return carry

    lax.fori_loop(0, tb // c_rows, chunk, 0, unroll=2)

    @pl.when(t == pl.num_programs(2) - 1)
    def _():
        for h in range(nh):
            st_ref[h] = st_scr[h].T


def _hgrn_prompt(proj3, lb_logits, hn, tri, tb):
    b, s, _ = proj3.shape
    nhb = HG_HEADS * HG_DK // COL
    nh = COL // HG_DK

    def spec(cb):
        return pl.BlockSpec((None, tb, COL), lambda bi, hb, t: (bi, t, cb + hb))

    o, st = pl.pallas_call(
        functools.partial(_hgrn_kernel, tb=tb),
        grid=(b, nhb, s // tb),
        in_specs=[spec(CB_QH), spec(CB_FH), spec(CB_IH), spec(CB_GH),
                  pl.BlockSpec((lb_logits.shape[0], COL), lambda bi, hb, t: (0, hb)),
                  pl.BlockSpec((1, HG_DV), lambda bi, hb, t: (0, 0)),
                  pl.BlockSpec(tri.shape, lambda bi, hb, t: (0, 0))],
        out_specs=[pl.BlockSpec((None, tb, COL), lambda bi, hb, t: (bi, t, hb)),
                   pl.BlockSpec((None, nh, HG_DK, HG_DV), lambda bi, hb, t: (bi, hb, 0, 0))],
        out_shape=[jax.ShapeDtypeStruct((b, s, HG_HEADS * HG_DV), F32),
                   jax.ShapeDtypeStruct((b, HG_HEADS, HG_DK, HG_DV), F32)],
        scratch_shapes=[pltpu.VMEM((nh, HG_DV, HG_DK), F32),
                        pltpu.VMEM((tb, COL), F32), pltpu.VMEM((tb, COL), F32), pltpu.VMEM((tb, COL), F32)],
        compiler_params=_params("arbitrary", "arbitrary", "arbitrary"),
        name="hgrn_prompt",
    )(proj3, proj3, proj3, proj3, lb_logits, hn, tri)
    return o.reshape(b * s, HG_HEADS * HG_DV), st


def _mix_kernel(*refs, n_att):
    n_refs = 1 if n_att == 1 else 2 * n_att
    att_refs = refs[:n_refs]
    (hg_ref, ga0, ga1, gb0, gb1, x_ref, g1_ref, sc2_ref, sh2_ref, n2_ref,
     wa_ref, wb_ref, wo_ref, x1_ref, h2_ref) = refs[n_refs:]
    if n_att == 1:
        att = att_refs[0][...]
    else:
        lses = [att_refs[2 * g + 1][...] for g in range(n_att)]
        mx = functools.reduce(jnp.maximum, lses)
        es = [jnp.exp(l - mx) for l in lses]
        den = functools.reduce(lambda a, b: a + b, es)
        num = functools.reduce(lambda a, b: a + b, [es[g] * att_refs[2 * g][...] for g in range(n_att)])
        att = num / den
    ga = jnp.concatenate([ga0[...], ga1[...]], axis=-1)
    gb = jnp.concatenate([gb0[...], gb1[...]], axis=-1)
    ya = jnp.dot(att.astype(BF16), wa_ref[...], preferred_element_type=F32)
    yb = jnp.dot(hg_ref[...].astype(BF16), wb_ref[...], preferred_element_type=F32)
    y = jax.nn.sigmoid(ga) * ya + jax.nn.sigmoid(gb) * yb
    x1 = x_ref[...] + g1_ref[...] * jnp.dot(y.astype(BF16), wo_ref[...], preferred_element_type=F32)
    x1_ref[...] = x1
    ms = jnp.mean(x1 * x1, axis=-1, keepdims=True)
    xn = x1 * lax.rsqrt(ms + EPS) * n2_ref[...]
    h2_ref[...] = (xn * (1.0 + sc2_ref[...]) + sh2_ref[...]).astype(BF16)


def _mix(att_list, hg, proj2, x2, mods, per_row, rows_per_batch, tm, n2, wa, wb, wo):
    n = x2.shape[0]
    row = lambda w: pl.BlockSpec((tm, w), lambda i: (i, 0))
    colblk = lambda cb: pl.BlockSpec((tm, COL), lambda i: (i, cb))
    full = lambda a: pl.BlockSpec(a.shape, lambda i: (0, 0))
    in_specs = ([row(ATT_OUT)] * len(att_list)
                + [row(D_MODEL), colblk(CB_GA), colblk(CB_GA + 1), colblk(CB_GB), colblk(CB_GB + 1), row(D_MODEL),
                   _mod_spec(per_row, tm, rows_per_batch, 2), _mod_spec(per_row, tm, rows_per_batch, 4),
                   _mod_spec(per_row, tm, rows_per_batch, 3), full(n2), full(wa), full(wb), full(wo)])
    n_att = 1 if len(att_list) == 1 else len(att_list) // 2
    return pl.pallas_call(
        functools.partial(_mix_kernel, n_att=n_att),
        grid=(n // tm,),
        in_specs=in_specs,
        out_specs=[row(D_MODEL), row(D_MODEL)],
        out_shape=[jax.ShapeDtypeStruct((n, D_MODEL), F32), jax.ShapeDtypeStruct((n, D_MODEL), BF16)],
        compiler_params=_params("arbitrary"),
        name="mix",
    )(*att_list, hg, proj2, proj2, proj2, proj2, x2, mods, mods, mods, n2, wa, wb, wo)


TOK_LANES = 128
CAND_ROWS = ((0, 16), (1, 16), (2, 8), (3, 8), (4, 8), (5, 8), (6, 8), (7, 8))


def _top16_rows(s, ridx, sentinel):
    vals, idxs = [], []
    for _ in range(PEER_TOPK):
        m = jnp.max(s, axis=0, keepdims=True)
        am = jnp.min(jnp.where(s == m, ridx, sentinel), axis=0, keepdims=True)
        vals.append(m)
        idxs.append(am)
        s = jnp.where(ridx == am, NEG_INF, s)
    return jnp.concatenate(vals, axis=0), jnp.concatenate(idxs, axis=0)


def _route_kernel(h2_ref, wq_ref, sk_ref, a_ref, b_ref, g_ref, q_scr, at_scr, bt_scr, gt_scr):
    tm = h2_ref.shape[0]
    tt = TOK_LANES
    q = jnp.dot(h2_ref[...], wq_ref[...], preferred_element_type=F32)
    for hp in range(2 * PEER_HEADS):
        q_scr[hp] = q[:, hp * PEER_HALF:(hp + 1) * PEER_HALF].astype(BF16)
    kidx = lax.broadcasted_iota(jnp.int32, (PEER_KEYS, tt), 0)
    sub16 = lax.broadcasted_iota(jnp.int32, (PEER_TOPK, tt), 0)
    sub8 = lax.broadcasted_iota(jnp.int32, (8, tt), 0)
    cflat = jnp.concatenate([p * PEER_TOPK + (sub16 if nq == 16 else sub8) for p, nq in CAND_ROWS]
                            + [(sub8 + 8) * PEER_TOPK], axis=0)

    def group(gi, carry):
        rows = pl.ds(pl.multiple_of(gi * tt, tt), tt)

        def sub_key_top(h):
            v1, i1 = _top16_rows(_nt_dot(sk_ref[0], q_scr[2 * h, rows, :]), kidx, PEER_KEYS)
            v2, i2 = _top16_rows(_nt_dot(sk_ref[1], q_scr[2 * h + 1, rows, :]), kidx, PEER_KEYS)
            return v1, i1, v2, i2

        def select(h, tops):
            v1, i1, v2, i2 = tops
            cand = jnp.concatenate([v1[p:p + 1] + v2[0:nq] for p, nq in CAND_ROWS] + [v1[8:16] + v2[0:1]], axis=0)
            tv, tp = _top16_rows(cand, cflat, PEER_TOPK * PEER_TOPK)
            e = jnp.exp(tv - tv[0:1])
            g = e / jnp.sum(e, axis=0, keepdims=True)
            pr = tp >> 4
            qr = tp & 15
            a_sel = jnp.zeros((PEER_TOPK, tt), jnp.int32)
            b_sel = jnp.zeros((PEER_TOPK, tt), jnp.int32)
            for p in range(PEER_TOPK):
                a_sel = jnp.where(pr == p, i1[p:p + 1], a_sel)
                b_sel = jnp.where(qr == p, i2[p:p + 1], b_sel)
            first = h * PEER_TOPK if isinstance(h, int) else pl.multiple_of(h * PEER_TOPK, PEER_TOPK)
            slot = pl.ds(first, PEER_TOPK)
            at_scr[slot, :] = a_sel
            bt_scr[slot, :] = b_sel
            gt_scr[slot, :] = g

        def head(h, tops):
            select(h, tops)
            return sub_key_top(h + 1)

        select(PEER_HEADS - 1, lax.fori_loop(0, PEER_HEADS - 1, head, sub_key_top(0)))
        a_ref[rows, :] = at_scr[...].T
        b_ref[rows, :] = bt_scr[...].T
        g_ref[rows, :] = gt_scr[...].T
        return carry

    lax.fori_loop(0, tm // tt, group, 0)


def _route(h2, wq_bf, sk_bf, tm):
    n = h2.shape[0]
    assert tm % TOK_LANES == 0 and n % tm == 0
    row = pl.BlockSpec((tm, 128), lambda i: (i, 0))
    slots = PEER_HEADS * PEER_TOPK
    return pl.pallas_call(
        _route_kernel,
        grid=(n // tm,),
        in_specs=[pl.BlockSpec((tm, D_MODEL), lambda i: (i, 0)),
                  pl.BlockSpec(wq_bf.shape, lambda i: (0, 0)),
                  pl.BlockSpec(sk_bf.shape, lambda i: (0, 0, 0))],
        out_specs=[row, row, row],
        out_shape=[jax.ShapeDtypeStruct((n, 128), jnp.int32), jax.ShapeDtypeStruct((n, 128), jnp.int32),
                   jax.ShapeDtypeStruct((n, 128), F32)],
        scratch_shapes=[pltpu.VMEM((2 * PEER_HEADS, tm, PEER_HALF), BF16),
                        pltpu.VMEM((slots, TOK_LANES), jnp.int32), pltpu.VMEM((slots, TOK_LANES), jnp.int32),
                        pltpu.VMEM((slots, TOK_LANES), F32)],
        compiler_params=_params("arbitrary"),
        name="peer_route",
    )(h2, wq_bf, sk_bf)


def _peer_u_kernel(h2_ref, u_ref, a_ref, b_ref, g_ref, w_ref, act_scr, *, ac):
    c = pl.program_id(1)

    @pl.when(c == 0)
    def _():
        act_scr[...] = jnp.zeros_like(act_scr)

    hc = _nt_dot(h2_ref[...], u_ref[...])
    a_idx = a_ref[...]
    b_idx = b_ref[...]
    act = act_scr[...]
    for i in range(ac):
        gathered = jnp.take_along_axis(hc[:, i * 128:(i + 1) * 128], b_idx, axis=1)
        act = jnp.where(a_idx == c * ac + i, gathered, act)
    act_scr[...] = act

    @pl.when(c == pl.num_programs(1) - 1)
    def _():
        w_ref[...] = g_ref[...] * _gelu_tanh(act)


def _peer_u(h2, u_bf, a_idx, b_idx, gate, tm, ac):
    n = h2.shape[0]
    row = pl.BlockSpec((tm, 128), lambda i, c: (i, 0))
    return pl.pallas_call(
        functools.partial(_peer_u_kernel, ac=ac),
        grid=(n // tm, PEER_KEYS // ac),
        in_specs=[pl.BlockSpec((tm, D_MODEL), lambda i, c: (i, 0)),
                  pl.BlockSpec((ac * 128, D_MODEL), lambda i, c: (c, 0)),
                  row, row, row],
        out_specs=row,
        out_shape=jax.ShapeDtypeStruct((n, 128), F32),
        scratch_shapes=[pltpu.VMEM((tm, 128), F32)],
        compiler_params=_params("arbitrary", "arbitrary"),
        name="peer_u",
    )(h2, u_bf, a_idx, b_idx, gate)


def _peer_v_kernel(a_ref, b_ref, w_ref, v_ref, x1_ref, g2_ref, o_ref, w3_scr, acc_scr, *, ac, tm):
    c = pl.program_id(1)

    @pl.when(c == 0)
    def _():
        sub = lax.broadcasted_iota(jnp.int32, (128, 128), 0)

        def build(n, carry):
            ar = a_ref[pl.ds(n, 1), :]
            br = b_ref[pl.ds(n, 1), :]
            wr = w_ref[pl.ds(n, 1), :]
            at = jnp.where(sub == ar, 1.0, 0.0).astype(BF16)
            rt = jnp.where(sub == br, wr, 0.0).astype(BF16)
            w3_scr[pl.ds(pl.multiple_of(n * W_PITCH, 8), 128), :] = _nt_dot(at, rt)
            return carry

        lax.fori_loop(0, tm, build, 0, unroll=32)

    part = None
    for i in range(0, ac, 2):
        lhs = jnp.concatenate([w3_scr[pl.ds(c * ac + i, tm, stride=W_PITCH), :],
                               w3_scr[pl.ds(c * ac + i + 1, tm, stride=W_PITCH), :]], axis=-1)
        d = jnp.dot(lhs.astype(BF16), v_ref[i * 128:(i + 2) * 128, :], preferred_element_type=F32)
        part = d if part is None else part + d

    @pl.when(c == 0)
    def _():
        acc_scr[...] = part

    @pl.when(c > 0)
    def _():
        acc_scr[...] += part

    @pl.when(c == pl.num_programs(1) - 1)
    def _():
        o_ref[...] = x1_ref[...] + g2_ref[...] * acc_scr[...]


def _peer_v(a_idx, b_idx, wts, v_bf, x1, mods, per_row, rows_per_batch, tm, ac):
    n = x1.shape[0]
    row = pl.BlockSpec((tm, 128), lambda i, c: (i, 0))
    wide = pl.BlockSpec((tm, D_MODEL), lambda i, c: (i, 0))
    return pl.pallas_call(
        functools.partial(_peer_v_kernel, ac=ac, tm=tm),
        grid=(n // tm, PEER_KEYS // ac),
        in_specs=[row, row, row,
                  pl.BlockSpec((ac * 128, D_MODEL), lambda i, c: (c, 0)),
                  wide, _mod_spec(per_row, tm, rows_per_batch, 5)],
        out_specs=wide,
        out_shape=jax.ShapeDtypeStruct((n, D_MODEL), F32),
        scratch_shapes=[pltpu.VMEM((tm * W_PITCH, 128), F32), pltpu.VMEM((tm, D_MODEL), F32)],
        compiler_params=_params("arbitrary", "arbitrary"),
        name="peer_v",
    )(a_idx, b_idx, wts, v_bf, x1, mods)


def _decode_attn_kernel(q_ref, k_ref, v_ref, c0_ref, c1_ref, c2_ref, o_ref):
    caches = (c0_ref, c1_ref, c2_ref)
    lses, outs = [], []
    for g, (win, dil) in enumerate(ATT_GROUPS):
        length = caches[g].shape[-1]
        pos = lax.broadcasted_iota(jnp.int32, (1, length), 1)
        tap = pos % dil == 0
        o_cols, lse_cols = [], []
        for h in range(ATT_HEADS):
            q = q_ref[g][:, h:h + 1]
            kn = k_ref[g][:, h:h + 1]
            vn = v_ref[g][:, h:h + 1]
            s = jnp.sum(caches[g][0, h] * q, axis=0, keepdims=True)
            s = jnp.where(tap, s, NEG_INF)
            s0 = jnp.sum(kn * q, axis=0, keepdims=True)
            m = jnp.maximum(jnp.max(s, axis=1, keepdims=True), s0)
            p = jnp.exp(s - m)
            p0 = jnp.exp(s0 - m)
            l = jnp.sum(p, axis=1, keepdims=True) + p0
            o_cols.append((jnp.sum(caches[g][1, h] * p, axis=1, keepdims=True) + p0 * vn) / l)
            lse_cols.append(m + jnp.log(l))
        outs.append(jnp.concatenate(o_cols, axis=1))
        lses.append(jnp.concatenate(lse_cols, axis=1))
    mx = functools.reduce(jnp.maximum, lses)
    es = [jnp.exp(x - mx) for x in lses]
    o_ref[...] = (es[0] * outs[0] + es[1] * outs[1] + es[2] * outs[2]) / (es[0] + es[1] + es[2])


def _decode_attn(proj_s, caches):
    n = proj_s.shape[0]

    def cols_t(cb):
        x = proj_s[:, cb * COL:cb * COL + ATT_WIDTH].reshape(n, N_GROUPS, ATT_HEADS, ATT_HEAD_DIM)
        return x.transpose(0, 1, 3, 2)

    views, specs = [], []
    for cache in caches:
        views.append(cache.transpose(0, 2, 3, 4, 1))
        specs.append(pl.BlockSpec((None,) + views[-1].shape[1:], lambda i: (i, 0, 0, 0, 0)))
    qkv_spec = pl.BlockSpec((None, N_GROUPS, ATT_HEAD_DIM, ATT_HEADS), lambda i: (i, 0, 0, 0))
    att = pl.pallas_call(
        _decode_attn_kernel,
        grid=(n,),
        in_specs=[qkv_spec, qkv_spec, qkv_spec] + specs,
        out_specs=pl.BlockSpec((None, ATT_HEAD_DIM, ATT_HEADS), lambda i: (i, 0, 0)),
        out_shape=jax.ShapeDtypeStruct((n, ATT_HEAD_DIM, ATT_HEADS), F32),
        compiler_params=_params("arbitrary"),
        name="decode_attn",
    )(cols_t(CB_Q), cols_t(CB_K), cols_t(CB_V), *views)
    return att.transpose(0, 2, 1).reshape(n, ATT_OUT)


def _decode_hgrn_kernel(q0, q1, f0, f1, i0, i1, g0, g1, lbl_ref, hn_ref, st_ref, o_ref, sto_ref, *, bt):
    nh = COL // HG_DK
    lb = _lower_bound(lbl_ref[...])
    for h in range(HG_HEADS):
        qr, fr, ir, gr = ((q0, f0, i0, g0), (q1, f1, i1, g1))[h // nh]
        cols = slice((h % nh) * HG_DK, (h % nh + 1) * HG_DK)
        lbh = lb[:, h * HG_DK:(h + 1) * HG_DK]
        f = lbh + (1.0 - lbh) * jax.nn.sigmoid(fr[:, cols])
        ft = f.T
        kt = 1.0 - ft
        qt = _silu(qr[:, cols]).T
        for i in range(bt):
            v = ir[i:i + 1, cols]
            s_new = ft[:, i:i + 1] * st_ref[i, h] + kt[:, i:i + 1] * v
            sto_ref[i, h] = s_new
            o = jnp.sum(qt[:, i:i + 1] * s_new, axis=0, keepdims=True)
            ms = jnp.mean(o * o, axis=-1, keepdims=True)
            o_ref[i:i + 1, h * HG_DV:(h + 1) * HG_DV] = (o * lax.rsqrt(ms + EPS) * hn_ref[...]
                                                         * _silu(gr[i:i + 1, cols]))


def _decode_hgrn(proj_s, state, lb_logits, hn, bt):
    n = proj_s.shape[0]
    st_spec = pl.BlockSpec((bt, HG_HEADS, HG_DK, HG_DV), lambda i: (i, 0, 0, 0))
    half = lambda cb: pl.BlockSpec((bt, COL), lambda i: (i, cb))
    return pl.pallas_call(
        functools.partial(_decode_hgrn_kernel, bt=bt),
        grid=(n // bt,),
        in_specs=[half(CB_QH), half(CB_QH + 1), half(CB_FH), half(CB_FH + 1), half(CB_IH), half(CB_IH + 1),
                  half(CB_GH), half(CB_GH + 1),
                  pl.BlockSpec(lb_logits.shape, lambda i: (0, 0)),
                  pl.BlockSpec((1, HG_DV), lambda i: (0, 0)),
                  st_spec],
        out_specs=[pl.BlockSpec((bt, HG_HEADS * HG_DV), lambda i: (i, 0)), st_spec],
        out_shape=[jax.ShapeDtypeStruct((n, HG_HEADS * HG_DV), F32),
                   jax.ShapeDtypeStruct(state.shape, F32)],
        compiler_params=_params("arbitrary"),
        name="decode_hgrn",
    )(proj_s, proj_s, proj_s, proj_s, proj_s, proj_s, proj_s, proj_s, lb_logits, hn, state)


def _block_diag_ones(n, seg):
    i = jnp.arange(n)
    return (i[:, None] // seg == i[None, :] // seg)


def _peer(h2, x1, mods, per_row, rows_per_batch, wq_bf, sk_bf, u_bf, v_bf):
    n = h2.shape[0]
    tm_u, ac_u = min(512, n), 16
    tm_v, ac_v = min(256, n), 32
    a_idx, b_idx, gate = _route(h2, wq_bf, sk_bf, min(256, n))
    wts = _peer_u(h2, u_bf, a_idx, b_idx, gate, tm_u, ac_u)
    return _peer_v(a_idx, b_idx, wts, v_bf, x1, mods, per_row, rows_per_batch, tm_v, ac_v)


def _kv_rows_kernel(k_ref, v_ref, o_ref):
    for j, ref in enumerate((k_ref, v_ref)):
        t = ref[...].T
        for h in range(ATT_HEADS):
            o_ref[j, h] = t[h * ATT_HEAD_DIM:(h + 1) * ATT_HEAD_DIM, :]


def _kv_rows(proj3, g, rows):
    b, s, _ = proj3.shape
    tr = min(256, rows)
    assert rows % tr == 0 and (s - rows) % tr == 0
    first = (s - rows) // tr
    out = pl.pallas_call(
        _kv_rows_kernel,
        grid=(b, rows // tr),
        in_specs=[pl.BlockSpec((None, tr, COL), lambda bi, t: (bi, first + t, CB_K + g)),
                  pl.BlockSpec((None, tr, COL), lambda bi, t: (bi, first + t, CB_V + g))],
        out_specs=pl.BlockSpec((None, 2, ATT_HEADS, ATT_HEAD_DIM, tr), lambda bi, t: (bi, 0, 0, 0, t)),
        out_shape=jax.ShapeDtypeStruct((b, 2, ATT_HEADS, ATT_HEAD_DIM, rows), F32),
        compiler_params=_params("arbitrary", "arbitrary"),
        name=f"kv_rows_g{g}",
    )(proj3, proj3)
    return out.transpose(0, 4, 1, 2, 3)


def kernel(x_prompt, x_sample, cache_kv_w128, cache_kv_w512, cache_kv_w2048, state_hgrn, c_prompt, c_sample, w_ada, b_ada, norm1_w, norm2_w, w_in, q_norm_w, k_norm_w, hg_lb_logits, hg_norm_w, w_br_a, w_br_b, w_o, w_peer_q, peer_subkeys, peer_u, peer_v):
    bsz, seq, _ = x_prompt.shape
    dec, dec_t, _ = x_sample.shape
    assert w_ada.shape[0] == 1 and dec_t == 1 and seq % (ATT_GROUPS[-1][1] * SPAN) == 0
    for (win, dil), cache in zip(ATT_GROUPS, (cache_kv_w128, cache_kv_w512, cache_kv_w2048)):
        assert win == dil * SPAN and cache.shape[2] == win

    w_ada_bf = w_ada[0].astype(BF16)
    w_in_bf = w_in[0].astype(BF16)
    wa, wb, wo = w_br_a[0].astype(BF16), w_br_b[0].astype(BF16), w_o[0].astype(BF16)
    wq_bf = w_peer_q[0].astype(BF16)
    sk_bf = peer_subkeys[0].astype(BF16)
    u_bf = peer_u[0].astype(BF16)
    v_bf = peer_v[0].astype(BF16)
    n1 = norm1_w[0].reshape(1, D_MODEL)
    n2 = norm2_w[0].reshape(1, D_MODEL)
    qn = jnp.tile(q_norm_w[0], N_GROUPS * ATT_HEADS).reshape(1, ATT_WIDTH)
    kn = jnp.tile(k_norm_w[0], N_GROUPS * ATT_HEADS).reshape(1, ATT_WIDTH)
    hn = hg_norm_w[0].reshape(1, HG_DV)
    seg = _block_diag_ones(MXU_TILE, ATT_HEAD_DIM).astype(BF16)
    tri = _block_diag_ones(128, HG_CHUNK) & (jnp.arange(128)[:, None] >= jnp.arange(128)[None, :])
    tri = jnp.tile(tri.astype(BF16), (1, 3))

    mods = _mods(jnp.concatenate([c_prompt, c_sample], axis=0), w_ada_bf, b_ada)
    mods_p = mods[:bsz].reshape(bsz, 1, 6 * D_MODEL)
    mods_s = mods[bsz:]

    n_p = bsz * seq
    tm_p = 512
    xp2 = x_prompt.reshape(n_p, D_MODEL)
    proj_p = _inproj(xp2, mods_p, False, seq, 1024, n1, w_in_bf, qn, kn, seg)
    proj_p3 = proj_p.reshape(bsz, seq, IN_WIDTH)
    att_p = _attn_prompt(proj_p3)
    hg_p, st_p = _hgrn_prompt(proj_p3, hg_lb_logits, hn, tri, 256)
    x1_p, h2_p = _mix([att_p], hg_p, proj_p, xp2, mods_p, False, seq, tm_p, n2, wa, wb, wo)
    y_p = _peer(h2_p, x1_p, mods_p, False, seq, wq_bf, sk_bf, u_bf, v_bf)

    xs2 = x_sample.reshape(dec, D_MODEL)
    proj_s = _inproj(xs2, mods_s, True, 1, dec, n1, w_in_bf, qn, kn, seg)
    att_s = _decode_attn(proj_s, (cache_kv_w128[0], cache_kv_w512[0], cache_kv_w2048[0]))
    hg_s, st_s = _decode_hgrn(proj_s, state_hgrn[0], hg_lb_logits, hn, 8)
    x1_s, h2_s = _mix([att_s], hg_s, proj_s, xs2, mods_s, True, 1, dec, n2, wa, wb, wo)
    y_s = _peer(h2_s, x1_s, mods_s, True, 1, wq_bf, sk_bf, u_bf, v_bf)

    kv_p = [_kv_rows(proj_p3, g, min(win, seq))[None] for g, (win, _) in enumerate(ATT_GROUPS)]
    proj_s3 = proj_s.reshape(1, dec, IN_WIDTH)
    kv_s = [_kv_rows(proj_s3, g, dec).reshape(1, dec, 1, 2, ATT_HEADS, ATT_HEAD_DIM) for g in range(N_GROUPS)]
    return (y_p.reshape(bsz, seq, D_MODEL), y_s.reshape(dec, 1, D_MODEL), kv_p[0], kv_p[1], kv_p[2], st_p[None],
            kv_s[0], kv_s[1], kv_s[2], st_s[None])
```

```python
import functools

import jax
import jax.numpy as jnp
from jax import lax
from jax.experimental import pallas as pl
from jax.experimental.pallas import tpu as pltpu

F32 = jnp.float32
BF16 = jnp.bfloat16

D_MODEL = 1024
ATT_GROUPS = ((128, 1), (512, 4), (2048, 16))
N_GROUPS = 3
ATT_HEADS = 8
ATT_HEAD_DIM = 64
ATT_OUT = ATT_HEADS * ATT_HEAD_DIM
ATT_WIDTH = N_GROUPS * ATT_OUT
MXU_TILE = 256
SPAN = 128
HG_HEADS = 8
HG_DK = 128
HG_DV = 128
PEER_HEADS = 8
PEER_KEYS = 128
PEER_TOPK = 16
PEER_HALF = 128
EPS = 1e-6
IN_WIDTH = 10752
COL = 512
NCOL = IN_WIDTH // COL
CB_Q, CB_K, CB_V = 0, 3, 6
CB_QH, CB_FH, CB_IH, CB_GH, CB_GA, CB_GB = 9, 11, 13, 15, 17, 19
HG_CHUNK = 16
W_PITCH = 136
VMEM_LIMIT = 56 * 1024 * 1024

NEG_INF = float("-inf")


def _silu(x):
    return x * jax.nn.sigmoid(x)


def _gelu_tanh(x):
    return 0.5 * x * (1.0 + jnp.tanh(0.7978845608028654 * (x + 0.044715 * (x * x * x))))


def _nt_dot(a, b):
    return lax.dot_general(a, b, (((1,), (1,)), ((), ())), preferred_element_type=F32)


def _params(*sem):
    return pltpu.CompilerParams(dimension_semantics=sem, vmem_limit_bytes=VMEM_LIMIT)


def _mods_kernel(c_ref, w_ref, b_ref, o_ref):
    s = _silu(c_ref[...])
    o_ref[...] = jnp.dot(s.astype(BF16), w_ref[...], preferred_element_type=F32) + b_ref[...]


def _mods(c, w_ada_bf, b_ada):
    n = c.shape[0]
    return pl.pallas_call(
        _mods_kernel,
        grid=(6,),
        in_specs=[pl.BlockSpec((n, D_MODEL), lambda j: (0, 0)),
                  pl.BlockSpec((D_MODEL, D_MODEL), lambda j: (0, j)),
                  pl.BlockSpec((1, D_MODEL), lambda j: (0, j))],
        out_specs=pl.BlockSpec((n, D_MODEL), lambda j: (0, j)),
        out_shape=jax.ShapeDtypeStruct((n, 6 * D_MODEL), F32),
        compiler_params=_params("arbitrary"),
        name="mods",
    )(c, w_ada_bf, b_ada)


def _mod_spec(per_row, tm, rows_per_batch, k):
    if per_row:
        return pl.BlockSpec((tm, D_MODEL), lambda i, *_: (i, k))
    tiles = rows_per_batch // tm
    return pl.BlockSpec((None, 1, D_MODEL), lambda i, *_: (i // tiles, 0, k))


def _inproj_kernel(x_ref, sc_ref, sh_ref, n1_ref, w_ref, qn_ref, kn_ref, seg_ref, o_ref, h_scr):
    j = pl.program_id(1)

    @pl.when(j == 0)
    def _():
        x = x_ref[...]
        ms = jnp.mean(x * x, axis=-1, keepdims=True)
        xn = x * lax.rsqrt(ms + EPS) * n1_ref[...]
        h_scr[...] = (xn * (1.0 + sc_ref[...]) + sh_ref[...]).astype(BF16)

    acc = jnp.dot(h_scr[...], w_ref[...], preferred_element_type=F32)

    def head_norm(w_row, scale):
        sq = (acc * acc).astype(BF16)
        seg_w = seg_ref.shape[0]
        ss = jnp.concatenate([jnp.dot(sq[:, c:c + seg_w], seg_ref[...], preferred_element_type=F32)
                              for c in range(0, ATT_WIDTH, seg_w)], axis=-1)
        return acc * lax.rsqrt(ss * (1.0 / ATT_HEAD_DIM) + EPS) * w_row * scale

    @pl.when(j == 0)
    def _():
        o_ref[...] = head_norm(qn_ref[...], ATT_HEAD_DIM ** -0.5)

    @pl.when(j == 1)
    def _():
        o_ref[...] = head_norm(kn_ref[...], 1.0)

    @pl.when(j >= 2)
    def _():
        o_ref[...] = acc


def _inproj(x2, mods, per_row, rows_per_batch, tm, n1, w_in_bf, qn, kn, seg):
    n = x2.shape[0]
    const = lambda i, j: (0, 0)
    return pl.pallas_call(
        _inproj_kernel,
        grid=(n // tm, IN_WIDTH // ATT_WIDTH),
        in_specs=[pl.BlockSpec((tm, D_MODEL), lambda i, j: (i, 0)),
                  _mod_spec(per_row, tm, rows_per_batch, 1),
                  _mod_spec(per_row, tm, rows_per_batch, 0),
                  pl.BlockSpec((1, D_MODEL), const),
                  pl.BlockSpec((D_MODEL, ATT_WIDTH), lambda i, j: (0, j)),
                  pl.BlockSpec((1, ATT_WIDTH), const),
                  pl.BlockSpec((1, ATT_WIDTH), const),
                  pl.BlockSpec(seg.shape, const)],
        out_specs=pl.BlockSpec((tm, ATT_WIDTH), lambda i, j: (i, j)),
        out_shape=jax.ShapeDtypeStruct((n, IN_WIDTH), F32),
        scratch_shapes=[pltpu.VMEM((tm, D_MODEL), BF16)],
        compiler_params=_params("arbitrary", "arbitrary"),
        name="inproj",
    )(x2, mods, mods, n1, w_in_bf, qn, kn, seg)


ATT_TILE = ATT_GROUPS[-1][1] * SPAN
PAIR = 2 * ATT_HEAD_DIM


def _class_rows(ref, start, n, dil):
    if dil == 1:
        return ref[pl.ds(start, n), :]
    return ref[pl.ds(start, n, stride=dil), :]


def _attn_fused_kernel(*refs):
    in_refs, (o_ref, og_scr, lg_scr) = refs[:5 * N_GROUPS], refs[5 * N_GROUPS:]
    tile = pl.program_id(1)
    lane = lax.broadcasted_iota(jnp.int32, (SPAN, PAIR), 1)
    lo = lane < ATT_HEAD_DIM
    qi = lax.broadcasted_iota(jnp.int32, (SPAN, 2 * SPAN), 0)
    kr = lax.broadcasted_iota(jnp.int32, (SPAN, 2 * SPAN), 1)
    delta = qi + SPAN - kr
    band = (delta >= 0) & (delta <= SPAN)
    cur_keys = kr >= SPAN

    def block(g, dil, r, jq, q_ref, k_ref, kp_ref, v_ref, vp_ref):
        q = _class_rows(q_ref, r + dil * SPAN * jq, SPAN, dil).astype(BF16)
        if jq == 0:
            k = jnp.concatenate([_class_rows(kp_ref, r, SPAN, dil), _class_rows(k_ref, r, SPAN, dil)], axis=0)
            v = jnp.concatenate([_class_rows(vp_ref, r, SPAN, dil), _class_rows(v_ref, r, SPAN, dil)], axis=0)
            valid = band & (cur_keys | (tile > 0))
        else:
            k = _class_rows(k_ref, r + dil * SPAN * (jq - 1), 2 * SPAN, dil)
            v = _class_rows(v_ref, r + dil * SPAN * (jq - 1), 2 * SPAN, dil)
            valid = band
        k = k.astype(BF16)
        v = v.astype(BF16)
        o_pair, lse_pair = None, None
        for first in (True, False):
            mine = lo if first else jnp.logical_not(lo)
            qh = jnp.where(mine, q, jnp.zeros_like(q))
            s = jnp.where(valid, _nt_dot(qh, k), NEG_INF)
            m = jnp.max(s, axis=-1, keepdims=True)
            p = jnp.exp(s - m)
            l = jnp.sum(p, axis=-1, keepdims=True)
            oh = jnp.dot(p.astype(BF16), v, preferred_element_type=F32) / l
            lse = m + jnp.log(l)
            o_pair = oh if first else jnp.where(lo, o_pair, oh)
            lse_pair = jnp.broadcast_to(lse, (SPAN, PAIR)) if first else jnp.where(lo, lse_pair, lse)
        start = r + dil * SPAN * jq
        if dil == 1:
            og_scr[g, pl.ds(start, SPAN), :] = o_pair
            lg_scr[g, pl.ds(start, SPAN), :] = lse_pair
        else:
            og_scr[g, pl.ds(start, SPAN, stride=dil), :] = o_pair
            lg_scr[g, pl.ds(start, SPAN, stride=dil), :] = lse_pair

    for g, (win, dil) in enumerate(ATT_GROUPS):
        grefs = in_refs[5 * g:5 * g + 5]
        nblk = ATT_TILE // (dil * SPAN)
        if dil == 1:
            for jq in range(nblk):
                block(g, dil, 0, jq, *grefs)
        else:
            def residue(r, carry, g=g, dil=dil, nblk=nblk, grefs=grefs):
                for jq in range(nblk):
                    block(g, dil, r, jq, *grefs)
                return carry
            lax.fori_loop(0, dil, residue, 0, unroll=max(1, 2 // nblk))

    lses = [lg_scr[g] for g in range(N_GROUPS)]
    mx = functools.reduce(jnp.maximum, lses)
    es = [jnp.exp(x - mx) for x in lses]
    num = es[0] * og_scr[0] + es[1] * og_scr[1] + es[2] * og_scr[2]
    o_ref[...] = (num / (es[0] + es[1] + es[2])).astype(o_ref.dtype)


def _attn_prompt(proj3):
    b, s, _ = proj3.shape
    npair = ATT_OUT // PAIR
    in_specs = []
    for g, (win, dil) in enumerate(ATT_GROUPS):
        prev_rows = dil * SPAN
        per_tile = ATT_TILE // prev_rows
        for cb in (CB_Q, CB_K, CB_V):
            col = lambda hp, cb=cb, g=g: (cb + g) * (COL // PAIR) + hp
            in_specs.append(pl.BlockSpec((None, ATT_TILE, PAIR), lambda bi, t, hp, col=col: (bi, t, col(hp))))
            if cb != CB_Q:
                in_specs.append(pl.BlockSpec(
                    (None, prev_rows, PAIR),
                    lambda bi, t, hp, col=col, per_tile=per_tile: (bi, jnp.maximum(t * per_tile - 1, 0), col(hp))))
    att = pl.pallas_call(
        _attn_fused_kernel,
        grid=(b, s // ATT_TILE, npair),
        in_specs=in_specs,
        out_specs=pl.BlockSpec((None, ATT_TILE, PAIR), lambda bi, t, hp: (bi, t, hp)),
        out_shape=jax.ShapeDtypeStruct((b, s, ATT_OUT), BF16),
        scratch_shapes=[pltpu.VMEM((N_GROUPS, ATT_TILE, PAIR), F32), pltpu.VMEM((N_GROUPS, ATT_TILE, PAIR), F32)],
        compiler_params=_params("arbitrary", "arbitrary", "arbitrary"),
        name="attn_prompt",
    )(*([proj3] * len(in_specs)))
    return att.reshape(b * s, ATT_OUT)


def _lower_bound(logits):
    mx = jnp.max(logits, axis=0, keepdims=True)
    e = jnp.exp(logits - mx)
    return e[0:1] / jnp.sum(e, axis=0, keepdims=True)


def _hgrn_kernel(qh_ref, fh_ref, ih_ref, gh_ref, lbl_ref, hn_ref, tri_ref, o_ref, st_ref,
                 st_scr, q_scr, k_scr, b_scr, *, tb):
    t = pl.program_id(2)
    nh = COL // HG_DK
    c_rows = HG_CHUNK

    @pl.when(t == 0)
    def _():
        st_scr[...] = jnp.zeros_like(st_scr)

    lb = _lower_bound(lbl_ref[...])
    f = lb + (1.0 - lb) * jax.nn.sigmoid(fh_ref[...])
    k_scr[...] = 1.0 - f
    q_scr[...] = _silu(qh_ref[...])
    logf = jnp.log(f)
    hi = logf.astype(BF16)
    rest = logf - hi.astype(F32)
    mid = rest.astype(BF16)
    low = (rest - mid.astype(F32)).astype(BF16)
    for r in range(tb // 128):
        rows = slice(r * 128, (r + 1) * 128)
        pieces = jnp.concatenate([hi[rows, :], mid[rows, :], low[rows, :]], axis=0)
        b_scr[rows, :] = jnp.dot(tri_ref[...], pieces, preferred_element_type=F32)
    rowid8 = lax.broadcasted_iota(jnp.int32, (8, 1), 0)

    def chunk(c, carry):
        r0 = pl.multiple_of(c * c_rows, c_rows)
        rows = pl.ds(r0, c_rows)
        for h in range(nh):
            cols = slice(h * HG_DK, (h + 1) * HG_DK)
            b = b_scr[rows, cols]
            qc = q_scr[rows, cols]
            kc = k_scr[rows, cols]
            vc = ih_ref[rows, cols]
            st = st_scr[h]
            o_state = _nt_dot((qc * jnp.exp(b)).astype(BF16), st.astype(BF16))
            parts = []
            for g8 in range(c_rows // 8):
                tr = slice(g8 * 8, (g8 + 1) * 8)
                bg, qg = b[tr], qc[tr]
                og = o_state[tr]
                for s in range(g8 * 8 + 8):
                    d = bg - b[s:s + 1]
                    if s >= g8 * 8:
                        d = jnp.where(rowid8 >= s - g8 * 8, d, NEG_INF)
                    a = jnp.sum(qg * kc[s:s + 1] * jnp.exp(d), axis=-1, keepdims=True)
                    og = og + a * vc[s:s + 1]
                parts.append(og)
            o = jnp.concatenate(parts, axis=0)
            bl = b[c_rows - 1:c_rows]
            kt = kc * jnp.exp(bl - b)
            upd = lax.dot_general(vc.astype(BF16), kt.astype(BF16), (((0,), (0,)), ((), ())),
                                  preferred_element_type=F32)
            st_scr[h] = st * jnp.exp(bl) + upd
            ms = jnp.mean(o * o, axis=-1, keepdims=True)
            on = o * lax.rsqrt(ms + EPS) * hn_ref[...]
            o_ref[rows, cols] = on * _silu(gh_ref[rows, cols])
        return carry

    lax.fori_loop(0, tb // c_rows, chunk, 0, unroll=2)

    @pl.when(t == pl.num_programs(2) - 1)
    def _():
        for h in range(nh):
            st_ref[h] = st_scr[h].T


def _hgrn_prompt(proj3, lb_logits, hn, tri, tb):
    b, s, _ = proj3.shape
    nhb = HG_HEADS * HG_DK // COL
    nh = COL // HG_DK

    def spec(cb):
        return pl.BlockSpec((None, tb, COL), lambda bi, hb, t: (bi, t, cb + hb))

    o, st = pl.pallas_call(
        functools.partial(_hgrn_kernel, tb=tb),
        grid=(b, nhb, s // tb),
        in_specs=[spec(CB_QH), spec(CB_FH), spec(CB_IH), spec(CB_GH),
                  pl.BlockSpec((lb_logits.shape[0], COL), lambda bi, hb, t: (0, hb)),
                  pl.BlockSpec((1, HG_DV), lambda bi, hb, t: (0, 0)),
                  pl.BlockSpec(tri.shape, lambda bi, hb, t: (0, 0))],
        out_specs=[pl.BlockSpec((None, tb, COL), lambda bi, hb, t: (bi, t, hb)),
                   pl.BlockSpec((None, nh, HG_DK, HG_DV), lambda bi, hb, t: (bi, hb, 0, 0))],
        out_shape=[jax.ShapeDtypeStruct((b, s, HG_HEADS * HG_DV), F32),
                   jax.ShapeDtypeStruct((b, HG_HEADS, HG_DK, HG_DV), F32)],
        scratch_shapes=[pltpu.VMEM((nh, HG_DV, HG_DK), F32),
                        pltpu.VMEM((tb, COL), F32), pltpu.VMEM((tb, COL), F32), pltpu.VMEM((tb, COL), F32)],
        compiler_params=_params("arbitrary", "arbitrary", "arbitrary"),
        name="hgrn_prompt",
    )(proj3, proj3, proj3, proj3, lb_logits, hn, tri)
    return o.reshape(b * s, HG_HEADS * HG_DV), st


def _mix_kernel(*refs, n_att):
    n_refs = 1 if n_att == 1 else 2 * n_att
    att_refs = refs[:n_refs]
    (hg_ref, ga0, ga1, gb0, gb1, x_ref, g1_ref, sc2_ref, sh2_ref, n2_ref,
     wa_ref, wb_ref, wo_ref, x1_ref, h2_ref) = refs[n_refs:]
    if n_att == 1:
        att = att_refs[0][...]
    else:
        lses = [att_refs[2 * g + 1][...] for g in range(n_att)]
        mx = functools.reduce(jnp.maximum, lses)
        es = [jnp.exp(l - mx) for l in lses]
        den = functools.reduce(lambda a, b: a + b, es)
        num = functools.reduce(lambda a, b: a + b, [es[g] * att_refs[2 * g][...] for g in range(n_att)])
        att = num / den
    ga = jnp.concatenate([ga0[...], ga1[...]], axis=-1)
    gb = jnp.concatenate([gb0[...], gb1[...]], axis=-1)
    ya = jnp.dot(att.astype(BF16), wa_ref[...], preferred_element_type=F32)
    yb = jnp.dot(hg_ref[...].astype(BF16), wb_ref[...], preferred_element_type=F32)
    y = jax.nn.sigmoid(ga) * ya + jax.nn.sigmoid(gb) * yb
    x1 = x_ref[...] + g1_ref[...] * jnp.dot(y.astype(BF16), wo_ref[...], preferred_element_type=F32)
    x1_ref[...] = x1
    ms = jnp.mean(x1 * x1, axis=-1, keepdims=True)
    xn = x1 * lax.rsqrt(ms + EPS) * n2_ref[...]
    h2_ref[...] = (xn * (1.0 + sc2_ref[...]) + sh2_ref[...]).astype(BF16)


def _mix(att_list, hg, proj2, x2, mods, per_row, rows_per_batch, tm, n2, wa, wb, wo):
    n = x2.shape[0]
    row = lambda w: pl.BlockSpec((tm, w), lambda i: (i, 0))
    colblk = lambda cb: pl.BlockSpec((tm, COL), lambda i: (i, cb))
    full = lambda a: pl.BlockSpec(a.shape, lambda i: (0, 0))
    in_specs = ([row(ATT_OUT)] * len(att_list)
                + [row(D_MODEL), colblk(CB_GA), colblk(CB_GA + 1), colblk(CB_GB), colblk(CB_GB + 1), row(D_MODEL),
                   _mod_spec(per_row, tm, rows_per_batch, 2), _mod_spec(per_row, tm, rows_per_batch, 4),
                   _mod_spec(per_row, tm, rows_per_batch, 3), full(n2), full(wa), full(wb), full(wo)])
    n_att = 1 if len(att_list) == 1 else len(att_list) // 2
    return pl.pallas_call(
        functools.partial(_mix_kernel, n_att=n_att),
        grid=(n // tm,),
        in_specs=in_specs,
        out_specs=[row(D_MODEL), row(D_MODEL)],
        out_shape=[jax.ShapeDtypeStruct((n, D_MODEL), F32), jax.ShapeDtypeStruct((n, D_MODEL), BF16)],
        compiler_params=_params("arbitrary"),
        name="mix",
    )(*att_list, hg, proj2, proj2, proj2, proj2, x2, mods, mods, mods, n2, wa, wb, wo)


TOK_LANES = 128
CAND_ROWS = ((0, 16), (1, 16), (2, 8), (3, 8), (4, 8), (5, 8), (6, 8), (7, 8))


def _top16_rows(s, ridx, sentinel):
    slabs = [(s[r:r + 8], ridx[r:r + 8]) for r in range(0, s.shape[0], 8)]
    vals, idxs = [], []
    for _ in range(PEER_TOPK):
        level = slabs
        while len(level) > 1:
            nxt = []
            for i in range(0, len(level) - 1, 2):
                (va, ia), (vb, ib) = level[i], level[i + 1]
                keep_a = va >= vb
                nxt.append((jnp.where(keep_a, va, vb), jnp.where(keep_a, ia, ib)))
            level = nxt + ([level[-1]] if len(level) % 2 else [])
        v8, i8 = level[0]
        m = jnp.max(v8, axis=0, keepdims=True)
        am = jnp.min(jnp.where(v8 == m, i8, sentinel), axis=0, keepdims=True)
        vals.append(m)
        idxs.append(am)
        slabs = [(jnp.where(ix == am, NEG_INF, sv), ix) for sv, ix in slabs]
    return jnp.concatenate(vals, axis=0), jnp.concatenate(idxs, axis=0)


def _route_kernel(h2_ref, wq_ref, sk_ref, a_ref, b_ref, g_ref, q_scr, at_scr, bt_scr, gt_scr):
    tm = h2_ref.shape[0]
    tt = TOK_LANES
    q = jnp.dot(h2_ref[...], wq_ref[...], preferred_element_type=F32)
    for hp in range(2 * PEER_HEADS):
        q_scr[hp] = q[:, hp * PEER_HALF:(hp + 1) * PEER_HALF].astype(BF16)
    kidx = lax.broadcasted_iota(jnp.int32, (PEER_KEYS, tt), 0)
    sub16 = lax.broadcasted_iota(jnp.int32, (PEER_TOPK, tt), 0)
    sub8 = lax.broadcasted_iota(jnp.int32, (8, tt), 0)
    cflat = jnp.concatenate([p * PEER_TOPK + (sub16 if nq == 16 else sub8) for p, nq in CAND_ROWS]
                            + [(sub8 + 8) * PEER_TOPK], axis=0)

    def group(gi, carry):
        rows = pl.ds(pl.multiple_of(gi * tt, tt), tt)

        def sub_key_top(h):
            v1, i1 = _top16_rows(_nt_dot(sk_ref[0], q_scr[2 * h, rows, :]), kidx, PEER_KEYS)
            v2, i2 = _top16_rows(_nt_dot(sk_ref[1], q_scr[2 * h + 1, rows, :]), kidx, PEER_KEYS)
            return v1, i1, v2, i2

        def select(h, tops):
            v1, i1, v2, i2 = tops
            cand = jnp.concatenate([v1[p:p + 1] + v2[0:nq] for p, nq in CAND_ROWS] + [v1[8:16] + v2[0:1]], axis=0)
            tv, tp = _top16_rows(cand, cflat, PEER_TOPK * PEER_TOPK)
            e = jnp.exp(tv - tv[0:1])
            g = e / jnp.sum(e, axis=0, keepdims=True)
            pr = tp >> 4
            qr = tp & 15
            a_sel = jnp.zeros((PEER_TOPK, tt), jnp.int32)
            b_sel = jnp.zeros((PEER_TOPK, tt), jnp.int32)
            for p in range(PEER_TOPK):
                a_sel = jnp.where(pr == p, i1[p:p + 1], a_sel)
                b_sel = jnp.where(qr == p, i2[p:p + 1], b_sel)
            first = h * PEER_TOPK if isinstance(h, int) else pl.multiple_of(h * PEER_TOPK, PEER_TOPK)
            slot = pl.ds(first, PEER_TOPK)
            at_scr[slot, :] = a_sel
            bt_scr[slot, :] = b_sel
            gt_scr[slot, :] = g

        def head(h, tops):
            select(h, tops)
            return sub_key_top(h + 1)

        select(PEER_HEADS - 1, lax.fori_loop(0, PEER_HEADS - 1, head, sub_key_top(0)))
        a_ref[rows, :] = at_scr[...].T
        b_ref[rows, :] = bt_scr[...].T
        g_ref[rows, :] = gt_scr[...].T
        return carry

    lax.fori_loop(0, tm // tt, group, 0)


def _route(h2, wq_bf, sk_bf, tm):
    n = h2.shape[0]
    assert tm % TOK_LANES == 0 and n % tm == 0
    row = pl.BlockSpec((tm, 128), lambda i: (i, 0))
    slots = PEER_HEADS * PEER_TOPK
    return pl.pallas_call(
        _route_kernel,
        grid=(n // tm,),
        in_specs=[pl.BlockSpec((tm, D_MODEL), lambda i: (i, 0)),
                  pl.BlockSpec(wq_bf.shape, lambda i: (0, 0)),
                  pl.BlockSpec(sk_bf.shape, lambda i: (0, 0, 0))],
        out_specs=[row, row, row],
        out_shape=[jax.ShapeDtypeStruct((n, 128), jnp.int32), jax.ShapeDtypeStruct((n, 128), jnp.int32),
                   jax.ShapeDtypeStruct((n, 128), F32)],
        scratch_shapes=[pltpu.VMEM((2 * PEER_HEADS, tm, PEER_HALF), BF16),
                        pltpu.VMEM((slots, TOK_LANES), jnp.int32), pltpu.VMEM((slots, TOK_LANES), jnp.int32),
                        pltpu.VMEM((slots, TOK_LANES), F32)],
        compiler_params=_params("arbitrary"),
        name="peer_route",
    )(h2, wq_bf, sk_bf)


def _peer_u_kernel(h2_ref, u_ref, a_ref, b_ref, g_ref, w_ref, act_scr, *, ac):
    c = pl.program_id(1)

    @pl.when(c == 0)
    def _():
        act_scr[...] = jnp.zeros_like(act_scr)

    hc = _nt_dot(h2_ref[...], u_ref[...])
    a_idx = a_ref[...]
    b_idx = b_ref[...]
    act = act_scr[...]
    for i in range(ac):
        gathered = jnp.take_along_axis(hc[:, i * 128:(i + 1) * 128], b_idx, axis=1)
        act = jnp.where(a_idx == c * ac + i, gathered, act)
    act_scr[...] = act

    @pl.when(c == pl.num_programs(1) - 1)
    def _():
        w_ref[...] = g_ref[...] * _gelu_tanh(act)


def _peer_u(h2, u_bf, a_idx, b_idx, gate, tm, ac):
    n = h2.shape[0]
    row = pl.BlockSpec((tm, 128), lambda i, c: (i, 0))
    return pl.pallas_call(
        functools.partial(_peer_u_kernel, ac=ac),
        grid=(n // tm, PEER_KEYS // ac),
        in_specs=[pl.BlockSpec((tm, D_MODEL), lambda i, c: (i, 0)),
                  pl.BlockSpec((ac * 128, D_MODEL), lambda i, c: (c, 0)),
                  row, row, row],
        out_specs=row,
        out_shape=jax.ShapeDtypeStruct((n, 128), F32),
        scratch_shapes=[pltpu.VMEM((tm, 128), F32)],
        compiler_params=_params("arbitrary", "arbitrary"),
        name="peer_u",
    )(h2, u_bf, a_idx, b_idx, gate)


def _peer_v_kernel(a_ref, b_ref, w_ref, v_ref, x1_ref, g2_ref, o_ref, w3_scr, acc_scr, *, ac, tm):
    c = pl.program_id(1)

    @pl.when(c == 0)
    def _():
        sub = lax.broadcasted_iota(jnp.int32, (128, 128), 0)

        def build(n, carry):
            ar = a_ref[pl.ds(n, 1), :]
            br = b_ref[pl.ds(n, 1), :]
            wr = w_ref[pl.ds(n, 1), :]
            at = jnp.where(sub == ar, 1.0, 0.0).astype(BF16)
            rt = jnp.where(sub == br, wr, 0.0).astype(BF16)
            w3_scr[pl.ds(pl.multiple_of(n * W_PITCH, 8), 128), :] = _nt_dot(at, rt)
            return carry

        lax.fori_loop(0, tm, build, 0, unroll=32)

    part = None
    for i in range(0, ac, 2):
        lhs = jnp.concatenate([w3_scr[pl.ds(c * ac + i, tm, stride=W_PITCH), :],
                               w3_scr[pl.ds(c * ac + i + 1, tm, stride=W_PITCH), :]], axis=-1)
        d = jnp.dot(lhs.astype(BF16), v_ref[i * 128:(i + 2) * 128, :], preferred_element_type=F32)
        part = d if part is None else part + d

    @pl.when(c == 0)
    def _():
        acc_scr[...] = part

    @pl.when(c > 0)
    def _():
        acc_scr[...] += part

    @pl.when(c == pl.num_programs(1) - 1)
    def _():
        o_ref[...] = x1_ref[...] + g2_ref[...] * acc_scr[...]


def _peer_v(a_idx, b_idx, wts, v_bf, x1, mods, per_row, rows_per_batch, tm, ac):
    n = x1.shape[0]
    row = pl.BlockSpec((tm, 128), lambda i, c: (i, 0))
    wide = pl.BlockSpec((tm, D_MODEL), lambda i, c: (i, 0))
    return pl.pallas_call(
        functools.partial(_peer_v_kernel, ac=ac, tm=tm),
        grid=(n // tm, PEER_KEYS // ac),
        in_specs=[row, row, row,
                  pl.BlockSpec((ac * 128, D_MODEL), lambda i, c: (c, 0)),
                  wide, _mod_spec(per_row, tm, rows_per_batch, 5)],
        out_specs=wide,
        out_shape=jax.ShapeDtypeStruct((n, D_MODEL), F32),
        scratch_shapes=[pltpu.VMEM((tm * W_PITCH, 128), F32), pltpu.VMEM((tm, D_MODEL), F32)],
        compiler_params=_params("arbitrary", "arbitrary"),
        name="peer_v",
    )(a_idx, b_idx, wts, v_bf, x1, mods)


def _decode_attn_kernel(q_ref, k_ref, v_ref, c0_ref, c1_ref, c2_ref, o_ref):
    caches = (c0_ref, c1_ref, c2_ref)
    lses, outs = [], []
    for g, (win, dil) in enumerate(ATT_GROUPS):
        length = caches[g].shape[-1]
        pos = lax.broadcasted_iota(jnp.int32, (1, length), 1)
        tap = pos % dil == 0
        o_cols, lse_cols = [], []
        for h in range(ATT_HEADS):
            q = q_ref[g][:, h:h + 1]
            kn = k_ref[g][:, h:h + 1]
            vn = v_ref[g][:, h:h + 1]
            s = jnp.sum(caches[g][0, h] * q, axis=0, keepdims=True)
            s = jnp.where(tap, s, NEG_INF)
            s0 = jnp.sum(kn * q, axis=0, keepdims=True)
            m = jnp.maximum(jnp.max(s, axis=1, keepdims=True), s0)
            p = jnp.exp(s - m)
            p0 = jnp.exp(s0 - m)
            l = jnp.sum(p, axis=1, keepdims=True) + p0
            o_cols.append((jnp.sum(caches[g][1, h] * p, axis=1, keepdims=True) + p0 * vn) / l)
            lse_cols.append(m + jnp.log(l))
        outs.append(jnp.concatenate(o_cols, axis=1))
        lses.append(jnp.concatenate(lse_cols, axis=1))
    mx = functools.reduce(jnp.maximum, lses)
    es = [jnp.exp(x - mx) for x in lses]
    o_ref[...] = (es[0] * outs[0] + es[1] * outs[1] + es[2] * outs[2]) / (es[0] + es[1] + es[2])


def _decode_attn(proj_s, caches):
    n = proj_s.shape[0]

    def cols_t(cb):
        x = proj_s[:, cb * COL:cb * COL + ATT_WIDTH].reshape(n, N_GROUPS, ATT_HEADS, ATT_HEAD_DIM)
        return x.transpose(0, 1, 3, 2)

    views, specs = [], []
    for cache in caches:
        views.append(cache.transpose(0, 2, 3, 4, 1))
        specs.append(pl.BlockSpec((None,) + views[-1].shape[1:], lambda i: (i, 0, 0, 0, 0)))
    qkv_spec = pl.BlockSpec((None, N_GROUPS, ATT_HEAD_DIM, ATT_HEADS), lambda i: (i, 0, 0, 0))
    att = pl.pallas_call(
        _decode_attn_kernel,
        grid=(n,),
        in_specs=[qkv_spec, qkv_spec, qkv_spec] + specs,
        out_specs=pl.BlockSpec((None, ATT_HEAD_DIM, ATT_HEADS), lambda i: (i, 0, 0)),
        out_shape=jax.ShapeDtypeStruct((n, ATT_HEAD_DIM, ATT_HEADS), F32),
        compiler_params=_params("arbitrary"),
        name="decode_attn",
    )(cols_t(CB_Q), cols_t(CB_K), cols_t(CB_V), *views)
    return att.transpose(0, 2, 1).reshape(n, ATT_OUT)


def _decode_hgrn_kernel(q0, q1, f0, f1, i0, i1, g0, g1, lbl_ref, hn_ref, st_ref, o_ref, sto_ref, *, bt):
    nh = COL // HG_DK
    lb = _lower_bound(lbl_ref[...])
    for h in range(HG_HEADS):
        qr, fr, ir, gr = ((q0, f0, i0, g0), (q1, f1, i1, g1))[h // nh]
        cols = slice((h % nh) * HG_DK, (h % nh + 1) * HG_DK)
        lbh = lb[:, h * HG_DK:(h + 1) * HG_DK]
        f = lbh + (1.0 - lbh) * jax.nn.sigmoid(fr[:, cols])
        ft = f.T
        kt = 1.0 - ft
        qt = _silu(qr[:, cols]).T
        for i in range(bt):
            v = ir[i:i + 1, cols]
            s_new = ft[:, i:i + 1] * st_ref[i, h] + kt[:, i:i + 1] * v
            sto_ref[i, h] = s_new
            o = jnp.sum(qt[:, i:i + 1] * s_new, axis=0, keepdims=True)
            ms = jnp.mean(o * o, axis=-1, keepdims=True)
            o_ref[i:i + 1, h * HG_DV:(h + 1) * HG_DV] = (o * lax.rsqrt(ms + EPS) * hn_ref[...]
                                                         * _silu(gr[i:i + 1, cols]))


def _decode_hgrn(proj_s, state, lb_logits, hn, bt):
    n = proj_s.shape[0]
    st_spec = pl.BlockSpec((bt, HG_HEADS, HG_DK, HG_DV), lambda i: (i, 0, 0, 0))
    half = lambda cb: pl.BlockSpec((bt, COL), lambda i: (i, cb))
    return pl.pallas_call(
        functools.partial(_decode_hgrn_kernel, bt=bt),
        grid=(n // bt,),
        in_specs=[half(CB_QH), half(CB_QH + 1), half(CB_FH), half(CB_FH + 1), half(CB_IH), half(CB_IH + 1),
                  half(CB_GH), half(CB_GH + 1),
                  pl.BlockSpec(lb_logits.shape, lambda i: (0, 0)),
                  pl.BlockSpec((1, HG_DV), lambda i: (0, 0)),
                  st_spec],
        out_specs=[pl.BlockSpec((bt, HG_HEADS * HG_DV), lambda i: (i, 0)), st_spec],
        out_shape=[jax.ShapeDtypeStruct((n, HG_HEADS * HG_DV), F32),
                   jax.ShapeDtypeStruct(state.shape, F32)],
        compiler_params=_params("arbitrary"),
        name="decode_hgrn",
    )(proj_s, proj_s, proj_s, proj_s, proj_s, proj_s, proj_s, proj_s, lb_logits, hn, state)


def _block_diag_ones(n, seg):
    i = jnp.arange(n)
    return (i[:, None] // seg == i[None, :] // seg)


def _peer(h2, x1, mods, per_row, rows_per_batch, wq_bf, sk_bf, u_bf, v_bf):
    n = h2.shape[0]
    tm_u, ac_u = min(512, n), 32
    tm_v, ac_v = min(256, n), 32
    a_idx, b_idx, gate = _route(h2, wq_bf, sk_bf, min(256, n))
    wts = _peer_u(h2, u_bf, a_idx, b_idx, gate, tm_u, ac_u)
    return _peer_v(a_idx, b_idx, wts, v_bf, x1, mods, per_row, rows_per_batch, tm_v, ac_v)


def _kv_rows_kernel(k_ref, v_ref, o_ref):
    for j, ref in enumerate((k_ref, v_ref)):
        t = ref[...].T
        for h in range(ATT_HEADS):
            o_ref[j, h] = t[h * ATT_HEAD_DIM:(h + 1) * ATT_HEAD_DIM, :]


def _kv_rows(proj3, g, rows):
    b, s, _ = proj3.shape
    tr = min(256, rows)
    assert rows % tr == 0 and (s - rows) % tr == 0
    first = (s - rows) // tr
    out = pl.pallas_call(
        _kv_rows_kernel,
        grid=(b, rows // tr),
        in_specs=[pl.BlockSpec((None, tr, COL), lambda bi, t: (bi, first + t, CB_K + g)),
                  pl.BlockSpec((None, tr, COL), lambda bi, t: (bi, first + t, CB_V + g))],
        out_specs=pl.BlockSpec((None, 2, ATT_HEADS, ATT_HEAD_DIM, tr), lambda bi, t: (bi, 0, 0, 0, t)),
        out_shape=jax.ShapeDtypeStruct((b, 2, ATT_HEADS, ATT_HEAD_DIM, rows), F32),
        compiler_params=_params("arbitrary", "arbitrary"),
        name=f"kv_rows_g{g}",
    )(proj3, proj3)
    return out.transpose(0, 4, 1, 2, 3)


def kernel(x_prompt, x_sample, cache_kv_w128, cache_kv_w512, cache_kv_w2048, state_hgrn, c_prompt, c_sample, w_ada, b_ada, norm1_w, norm2_w, w_in, q_norm_w, k_norm_w, hg_lb_logits, hg_norm_w, w_br_a, w_br_b, w_o, w_peer_q, peer_subkeys, peer_u, peer_v):
    bsz, seq, _ = x_prompt.shape
    dec, dec_t, _ = x_sample.shape
    assert w_ada.shape[0] == 1 and dec_t == 1 and seq % (ATT_GROUPS[-1][1] * SPAN) == 0
    for (win, dil), cache in zip(ATT_GROUPS, (cache_kv_w128, cache_kv_w512, cache_kv_w2048)):
        assert win == dil * SPAN and cache.shape[2] == win

    w_ada_bf = w_ada[0].astype(BF16)
    w_in_bf = w_in[0].astype(BF16)
    wa, wb, wo = w_br_a[0].astype(BF16), w_br_b[0].astype(BF16), w_o[0].astype(BF16)
    wq_bf = w_peer_q[0].astype(BF16)
    sk_bf = peer_subkeys[0].astype(BF16)
    u_bf = peer_u[0].astype(BF16)
    v_bf = peer_v[0].astype(BF16)
    n1 = norm1_w[0].reshape(1, D_MODEL)
    n2 = norm2_w[0].reshape(1, D_MODEL)
    qn = jnp.tile(q_norm_w[0], N_GROUPS * ATT_HEADS).reshape(1, ATT_WIDTH)
    kn = jnp.tile(k_norm_w[0], N_GROUPS * ATT_HEADS).reshape(1, ATT_WIDTH)
    hn = hg_norm_w[0].reshape(1, HG_DV)
    seg = _block_diag_ones(MXU_TILE, ATT_HEAD_DIM).astype(BF16)
    tri = _block_diag_ones(128, HG_CHUNK) & (jnp.arange(128)[:, None] >= jnp.arange(128)[None, :])
    tri = jnp.tile(tri.astype(BF16), (1, 3))

    mods = _mods(jnp.concatenate([c_prompt, c_sample], axis=0), w_ada_bf, b_ada)
    mods_p = mods[:bsz].reshape(bsz, 1, 6 * D_MODEL)
    mods_s = mods[bsz:]

    n_p = bsz * seq
    tm_p = 512
    xp2 = x_prompt.reshape(n_p, D_MODEL)
    proj_p = _inproj(xp2, mods_p, False, seq, 1024, n1, w_in_bf, qn, kn, seg)
    proj_p3 = proj_p.reshape(bsz, seq, IN_WIDTH)
    att_p = _attn_prompt(proj_p3)
    hg_p, st_p = _hgrn_prompt(proj_p3, hg_lb_logits, hn, tri, 256)
    x1_p, h2_p = _mix([att_p], hg_p, proj_p, xp2, mods_p, False, seq, tm_p, n2, wa, wb, wo)
    y_p = _peer(h2_p, x1_p, mods_p, False, seq, wq_bf, sk_bf, u_bf, v_bf)

    xs2 = x_sample.reshape(dec, D_MODEL)
    proj_s = _inproj(xs2, mods_s, True, 1, dec, n1, w_in_bf, qn, kn, seg)
    att_s = _decode_attn(proj_s, (cache_kv_w128[0], cache_kv_w512[0], cache_kv_w2048[0]))
    hg_s, st_s = _decode_hgrn(proj_s, state_hgrn[0], hg_lb_logits, hn, 8)
    x1_s, h2_s = _mix([att_s], hg_s, proj_s, xs2, mods_s, True, 1, dec, n2, wa, wb, wo)
    y_s = _peer(h2_s, x1_s, mods_s, True, 1, wq_bf, sk_bf, u_bf, v_bf)

    kv_p = [_kv_rows(proj_p3, g, min(win, seq))[None] for g, (win, _) in enumerate(ATT_GROUPS)]
    proj_s3 = proj_s.reshape(1, dec, IN_WIDTH)
    kv_s = [_kv_rows(proj_s3, g, dec).reshape(1, dec, 1, 2, ATT_HEADS, ATT_HEAD_DIM) for g in range(N_GROUPS)]
    return (y_p.reshape(bsz, seq, D_MODEL), y_s.reshape(dec, 1, D_MODEL), kv_p[0], kv_p[1], kv_p[2], st_p[None],
            kv_s[0], kv_s[1], kv_s[2], st_s[None])
```

```python
import functools

import jax
import jax.numpy as jnp
from jax import lax
from jax.experimental import pallas as pl
from jax.experimental.pallas import tpu as pltpu

F32 = jnp.float32
BF16 = jnp.bfloat16

D_MODEL = 1024
ATT_GROUPS = ((128, 1), (512, 4), (2048, 16))
N_GROUPS = 3
ATT_HEADS = 8
ATT_HEAD_DIM = 64
ATT_OUT = ATT_HEADS * ATT_HEAD_DIM
ATT_WIDTH = N_GROUPS * ATT_OUT
MXU_TILE = 256
SPAN = 128
HG_HEADS = 8
HG_DK = 128
HG_DV = 128
PEER_HEADS = 8
PEER_KEYS = 128
PEER_TOPK = 16
PEER_HALF = 128
EPS = 1e-6
IN_WIDTH = 10752
COL = 512
NCOL = IN_WIDTH // COL
CB_Q, CB_K, CB_V = 0, 3, 6
CB_QH, CB_FH, CB_IH, CB_GH, CB_GA, CB_GB = 9, 11, 13, 15, 17, 19
HG_CHUNK = 16
W_PITCH = 136
VMEM_LIMIT = 56 * 1024 * 1024

NEG_INF = float("-inf")


def _silu(x):
    return x * jax.nn.sigmoid(x)


def _gelu_tanh(x):
    return 0.5 * x * (1.0 + jnp.tanh(0.7978845608028654 * (x + 0.044715 * (x * x * x))))


def _nt_dot(a, b):
    return lax.dot_general(a, b, (((1,), (1,)), ((), ())), preferred_element_type=F32)


def _params(*sem):
    return pltpu.CompilerParams(dimension_semantics=sem, vmem_limit_bytes=VMEM_LIMIT)


def _mods_kernel(c_ref, w_ref, b_ref, o_ref):
    s = _silu(c_ref[...])
    o_ref[...] = jnp.dot(s.astype(BF16), w_ref[...], preferred_element_type=F32) + b_ref[...]


def _mods(c, w_ada_bf, b_ada):
    n = c.shape[0]
    return pl.pallas_call(
        _mods_kernel,
        grid=(6,),
        in_specs=[pl.BlockSpec((n, D_MODEL), lambda j: (0, 0)),
                  pl.BlockSpec((D_MODEL, D_MODEL), lambda j: (0, j)),
                  pl.BlockSpec((1, D_MODEL), lambda j: (0, j))],
        out_specs=pl.BlockSpec((n, D_MODEL), lambda j: (0, j)),
        out_shape=jax.ShapeDtypeStruct((n, 6 * D_MODEL), F32),
        compiler_params=_params("arbitrary"),
        name="mods",
    )(c, w_ada_bf, b_ada)


def _mod_spec(per_row, tm, rows_per_batch, k):
    if per_row:
        return pl.BlockSpec((tm, D_MODEL), lambda i, *_: (i, k))
    tiles = rows_per_batch // tm
    return pl.BlockSpec((None, 1, D_MODEL), lambda i, *_: (i // tiles, 0, k))


def _inproj_kernel(x_ref, sc_ref, sh_ref, n1_ref, w_ref, qn_ref, kn_ref, seg_ref, o_ref, h_scr):
    j = pl.program_id(1)

    @pl.when(j == 0)
    def _():
        x = x_ref[...]
        ms = jnp.mean(x * x, axis=-1, keepdims=True)
        xn = x * lax.rsqrt(ms + EPS) * n1_ref[...]
        h_scr[...] = (xn * (1.0 + sc_ref[...]) + sh_ref[...]).astype(BF16)

    acc = jnp.dot(h_scr[...], w_ref[...], preferred_element_type=F32)

    def head_norm(w_row, scale):
        sq = (acc * acc).astype(BF16)
        seg_w = seg_ref.shape[0]
        ss = jnp.concatenate([jnp.dot(sq[:, c:c + seg_w], seg_ref[...], preferred_element_type=F32)
                              for c in range(0, ATT_WIDTH, seg_w)], axis=-1)
        return acc * lax.rsqrt(ss * (1.0 / ATT_HEAD_DIM) + EPS) * w_row * scale

    @pl.when(j == 0)
    def _():
        o_ref[...] = head_norm(qn_ref[...], ATT_HEAD_DIM ** -0.5)

    @pl.when(j == 1)
    def _():
        o_ref[...] = head_norm(kn_ref[...], 1.0)

    @pl.when(j >= 2)
    def _():
        o_ref[...] = acc


def _inproj(x2, mods, per_row, rows_per_batch, tm, n1, w_in_bf, qn, kn, seg):
    n = x2.shape[0]
    const = lambda i, j: (0, 0)
    return pl.pallas_call(
        _inproj_kernel,
        grid=(n // tm, IN_WIDTH // ATT_WIDTH),
        in_specs=[pl.BlockSpec((tm, D_MODEL), lambda i, j: (i, 0)),
                  _mod_spec(per_row, tm, rows_per_batch, 1),
                  _mod_spec(per_row, tm, rows_per_batch, 0),
                  pl.BlockSpec((1, D_MODEL), const),
                  pl.BlockSpec((D_MODEL, ATT_WIDTH), lambda i, j: (0, j)),
                  pl.BlockSpec((1, ATT_WIDTH), const),
                  pl.BlockSpec((1, ATT_WIDTH), const),
                  pl.BlockSpec(seg.shape, const)],
        out_specs=pl.BlockSpec((tm, ATT_WIDTH), lambda i, j: (i, j)),
        out_shape=jax.ShapeDtypeStruct((n, IN_WIDTH), F32),
        scratch_shapes=[pltpu.VMEM((tm, D_MODEL), BF16)],
        compiler_params=_params("arbitrary", "arbitrary"),
        name="inproj",
    )(x2, mods, mods, n1, w_in_bf, qn, kn, seg)


ATT_TILE = ATT_GROUPS[-1][1] * SPAN
PAIR = 2 * ATT_HEAD_DIM


def _class_rows(ref, start, n, dil):
    if dil == 1:
        return ref[pl.ds(start, n), :]
    return ref[pl.ds(start, n, stride=dil), :]


def _attn_fused_kernel(*refs):
    in_refs, (o_ref, og_scr, lg_scr) = refs[:5 * N_GROUPS], refs[5 * N_GROUPS:]
    tile = pl.program_id(1)
    lane = lax.broadcasted_iota(jnp.int32, (SPAN, PAIR), 1)
    lo = lane < ATT_HEAD_DIM
    qi = lax.broadcasted_iota(jnp.int32, (SPAN, 2 * SPAN), 0)
    kr = lax.broadcasted_iota(jnp.int32, (SPAN, 2 * SPAN), 1)
    delta = qi + SPAN - kr
    band = (delta >= 0) & (delta <= SPAN)
    cur_keys = kr >= SPAN

    def block(g, dil, r, jq, q_ref, k_ref, kp_ref, v_ref, vp_ref):
        q = _class_rows(q_ref, r + dil * SPAN * jq, SPAN, dil).astype(BF16)
        if jq == 0:
            k = jnp.concatenate([_class_rows(kp_ref, r, SPAN, dil), _class_rows(k_ref, r, SPAN, dil)], axis=0)
            v = jnp.concatenate([_class_rows(vp_ref, r, SPAN, dil), _class_rows(v_ref, r, SPAN, dil)], axis=0)
            valid = band & (cur_keys | (tile > 0))
        else:
            k = _class_rows(k_ref, r + dil * SPAN * (jq - 1), 2 * SPAN, dil)
            v = _class_rows(v_ref, r + dil * SPAN * (jq - 1), 2 * SPAN, dil)
            valid = band
        k = k.astype(BF16)
        v = v.astype(BF16)
        o_pair, lse_pair = None, None
        for first in (True, False):
            mine = lo if first else jnp.logical_not(lo)
            qh = jnp.where(mine, q, jnp.zeros_like(q))
            s = jnp.where(valid, _nt_dot(qh, k), NEG_INF)
            m = jnp.max(s, axis=-1, keepdims=True)
            p = jnp.exp(s - m)
            l = jnp.sum(p, axis=-1, keepdims=True)
            oh = jnp.dot(p.astype(BF16), v, preferred_element_type=F32) / l
            lse = m + jnp.log(l)
            o_pair = oh if first else jnp.where(lo, o_pair, oh)
            lse_pair = jnp.broadcast_to(lse, (SPAN, PAIR)) if first else jnp.where(lo, lse_pair, lse)
        start = r + dil * SPAN * jq
        if dil == 1:
            og_scr[g, pl.ds(start, SPAN), :] = o_pair
            lg_scr[g, pl.ds(start, SPAN), :] = lse_pair
        else:
            og_scr[g, pl.ds(start, SPAN, stride=dil), :] = o_pair
            lg_scr[g, pl.ds(start, SPAN, stride=dil), :] = lse_pair

    for g, (win, dil) in enumerate(ATT_GROUPS):
        grefs = in_refs[5 * g:5 * g + 5]
        nblk = ATT_TILE // (dil * SPAN)
        if dil == 1:
            for jq in range(nblk):
                block(g, dil, 0, jq, *grefs)
        else:
            def residue(r, carry, g=g, dil=dil, nblk=nblk, grefs=grefs):
                for jq in range(nblk):
                    block(g, dil, r, jq, *grefs)
                return carry
            lax.fori_loop(0, dil, residue, 0, unroll=max(1, 2 // nblk))

    lses = [lg_scr[g] for g in range(N_GROUPS)]
    mx = functools.reduce(jnp.maximum, lses)
    es = [jnp.exp(x - mx) for x in lses]
    num = es[0] * og_scr[0] + es[1] * og_scr[1] + es[2] * og_scr[2]
    o_ref[...] = (num / (es[0] + es[1] + es[2])).astype(o_ref.dtype)


def _attn_prompt(proj3):
    b, s, _ = proj3.shape
    npair = ATT_OUT // PAIR
    in_specs = []
    for g, (win, dil) in enumerate(ATT_GROUPS):
        prev_rows = dil * SPAN
        per_tile = ATT_TILE // prev_rows
        for cb in (CB_Q, CB_K, CB_V):
            col = lambda hp, cb=cb, g=g: (cb + g) * (COL // PAIR) + hp
            in_specs.append(pl.BlockSpec((None, ATT_TILE, PAIR), lambda bi, t, hp, col=col: (bi, t, col(hp))))
            if cb != CB_Q:
                in_specs.append(pl.BlockSpec(
                    (None, prev_rows, PAIR),
                    lambda bi, t, hp, col=col, per_tile=per_tile: (bi, jnp.maximum(t * per_tile - 1, 0), col(hp))))
    att = pl.pallas_call(
        _attn_fused_kernel,
        grid=(b, s // ATT_TILE, npair),
        in_specs=in_specs,
        out_specs=pl.BlockSpec((None, ATT_TILE, PAIR), lambda bi, t, hp: (bi, t, hp)),
        out_shape=jax.ShapeDtypeStruct((b, s, ATT_OUT), BF16),
        scratch_shapes=[pltpu.VMEM((N_GROUPS, ATT_TILE, PAIR), F32), pltpu.VMEM((N_GROUPS, ATT_TILE, PAIR), F32)],
        compiler_params=_params("arbitrary", "arbitrary", "arbitrary"),
        name="attn_prompt",
    )(*([proj3] * len(in_specs)))
    return att.reshape(b * s, ATT_OUT)


def _lower_bound(logits):
    mx = jnp.max(logits, axis=0, keepdims=True)
    e = jnp.exp(logits - mx)
    return e[0:1] / jnp.sum(e, axis=0, keepdims=True)


HG_SEQS = 1


def _hgrn_kernel(qh_ref, fh_ref, ih_ref, gh_ref, lbl_ref, hn_ref, tri_ref, o_ref, st_ref,
                 st_scr, q_scr, k_scr, b_scr, *, tb):
    t = pl.program_id(2)
    nh = COL // HG_DK
    c_rows = HG_CHUNK
    rows_all = HG_SEQS * tb

    @pl.when(t == 0)
    def _():
        st_scr[...] = jnp.zeros_like(st_scr)

    lb = _lower_bound(lbl_ref[...])
    f = lb + (1.0 - lb) * jax.nn.sigmoid(fh_ref[...].reshape(rows_all, COL))
    k_scr[...] = 1.0 - f
    q_scr[...] = _silu(qh_ref[...].reshape(rows_all, COL))
    logf = jnp.log(f)
    hi = logf.astype(BF16)
    rest = logf - hi.astype(F32)
    mid = rest.astype(BF16)
    low = (rest - mid.astype(F32)).astype(BF16)
    for r in range(rows_all // 128):
        rows = slice(r * 128, (r + 1) * 128)
        pieces = jnp.concatenate([hi[rows, :], mid[rows, :], low[rows, :]], axis=0)
        b_scr[rows, :] = jnp.dot(tri_ref[...], pieces, preferred_element_type=F32)
    rowid8 = lax.broadcasted_iota(jnp.int32, (8, 1), 0)

    def chunk(c, carry):
        r0 = pl.multiple_of(c * c_rows, c_rows)
        rows = pl.ds(r0, c_rows)
        for sq in range(HG_SEQS):
            srows = pl.ds(pl.multiple_of(sq * tb + r0, c_rows), c_rows)
            for h in range(nh):
                cols = slice(h * HG_DK, (h + 1) * HG_DK)
                b = b_scr[srows, cols]
                qc = q_scr[srows, cols]
                kc = k_scr[srows, cols]
                vc = ih_ref[sq, rows, cols]
                st = st_scr[sq * nh + h]
                o_state = _nt_dot((qc * jnp.exp(b)).astype(BF16), st.astype(BF16))
                prods, where = [], []
                for g8 in range(c_rows // 8):
                    tr = slice(g8 * 8, (g8 + 1) * 8)
                    bg, qg = b[tr], qc[tr]
                    for s in range(g8 * 8 + 8):
                        d = bg - b[s:s + 1]
                        if s >= g8 * 8:
                            d = jnp.where(rowid8 >= s - g8 * 8, d, NEG_INF)
                        prods.append(qg * kc[s:s + 1] * jnp.exp(d))
                        where.append((g8, s))
                x = jnp.concatenate(prods, axis=0).astype(BF16)
                sums = jnp.dot(x, jnp.ones((HG_DK, HG_DV), BF16), preferred_element_type=F32)
                parts = [o_state[g8 * 8:(g8 + 1) * 8] for g8 in range(c_rows // 8)]
                for i, (g8, s) in enumerate(where):
                    parts[g8] = parts[g8] + sums[i * 8:(i + 1) * 8] * vc[s:s + 1]
                o = jnp.concatenate(parts, axis=0)
                bl = b[c_rows - 1:c_rows]
                kt = kc * jnp.exp(bl - b)
                upd = lax.dot_general(vc.astype(BF16), kt.astype(BF16), (((0,), (0,)), ((), ())),
                                      preferred_element_type=F32)
                st_scr[sq * nh + h] = st * jnp.exp(bl) + upd
                ms = jnp.mean(o * o, axis=-1, keepdims=True)
                on = o * lax.rsqrt(ms + EPS) * hn_ref[...]
                o_ref[sq, rows, cols] = on * _silu(gh_ref[sq, rows, cols])
        return carry

    lax.fori_loop(0, tb // c_rows, chunk, 0, unroll=2)

    @pl.when(t == pl.num_programs(2) - 1)
    def _():
        for sq in range(HG_SEQS):
            for h in range(nh):
                st_ref[sq, h] = st_scr[sq * nh + h].T


def _hgrn_prompt(proj3, lb_logits, hn, tri, tb):
    b, s, _ = proj3.shape
    assert b % HG_SEQS == 0
    nhb = HG_HEADS * HG_DK // COL
    nh = COL // HG_DK

    def spec(cb):
        return pl.BlockSpec((HG_SEQS, tb, COL), lambda bi, hb, t: (bi, t, cb + hb))

    o, st = pl.pallas_call(
        functools.partial(_hgrn_kernel, tb=tb),
        grid=(b // HG_SEQS, nhb, s // tb),
        in_specs=[spec(CB_QH), spec(CB_FH), spec(CB_IH), spec(CB_GH),
                  pl.BlockSpec((lb_logits.shape[0], COL), lambda bi, hb, t: (0, hb)),
                  pl.BlockSpec((1, HG_DV), lambda bi, hb, t: (0, 0)),
                  pl.BlockSpec(tri.shape, lambda bi, hb, t: (0, 0))],
        out_specs=[pl.BlockSpec((HG_SEQS, tb, COL), lambda bi, hb, t: (bi, t, hb)),
                   pl.BlockSpec((HG_SEQS, nh, HG_DK, HG_DV), lambda bi, hb, t: (bi, hb, 0, 0))],
        out_shape=[jax.ShapeDtypeStruct((b, s, HG_HEADS * HG_DV), F32),
                   jax.ShapeDtypeStruct((b, HG_HEADS, HG_DK, HG_DV), F32)],
        scratch_shapes=[pltpu.VMEM((HG_SEQS * nh, HG_DV, HG_DK), F32)]
                       + [pltpu.VMEM((HG_SEQS * tb, COL), F32)] * 3,
        compiler_params=_params("arbitrary", "arbitrary", "arbitrary"),
        name="hgrn_prompt",
    )(proj3, proj3, proj3, proj3, lb_logits, hn, tri)
    return o.reshape(b * s, HG_HEADS * HG_DV), st


def _mix_kernel(*refs, n_att):
    n_refs = 1 if n_att == 1 else 2 * n_att
    att_refs = refs[:n_refs]
    (hg_ref, ga0, ga1, gb0, gb1, x_ref, g1_ref, sc2_ref, sh2_ref, n2_ref,
     wa_ref, wb_ref, wo_ref, x1_ref, h2_ref) = refs[n_refs:]
    if n_att == 1:
        att = att_refs[0][...]
    else:
        lses = [att_refs[2 * g + 1][...] for g in range(n_att)]
        mx = functools.reduce(jnp.maximum, lses)
        es = [jnp.exp(l - mx) for l in lses]
        den = functools.reduce(lambda a, b: a + b, es)
        num = functools.reduce(lambda a, b: a + b, [es[g] * att_refs[2 * g][...] for g in range(n_att)])
        att = num / den
    ga = jnp.concatenate([ga0[...], ga1[...]], axis=-1)
    gb = jnp.concatenate([gb0[...], gb1[...]], axis=-1)
    ya = jnp.dot(att.astype(BF16), wa_ref[...], preferred_element_type=F32)
    yb = jnp.dot(hg_ref[...].astype(BF16), wb_ref[...], preferred_element_type=F32)
    y = jax.nn.sigmoid(ga) * ya + jax.nn.sigmoid(gb) * yb
    x1 = x_ref[...] + g1_ref[...] * jnp.dot(y.astype(BF16), wo_ref[...], preferred_element_type=F32)
    x1_ref[...] = x1
    ms = jnp.mean(x1 * x1, axis=-1, keepdims=True)
    xn = x1 * lax.rsqrt(ms + EPS) * n2_ref[...]
    h2_ref[...] = (xn * (1.0 + sc2_ref[...]) + sh2_ref[...]).astype(BF16)


def _mix(att_list, hg, proj2, x2, mods, per_row, rows_per_batch, tm, n2, wa, wb, wo):
    n = x2.shape[0]
    row = lambda w: pl.BlockSpec((tm, w), lambda i: (i, 0))
    colblk = lambda cb: pl.BlockSpec((tm, COL), lambda i: (i, cb))
    full = lambda a: pl.BlockSpec(a.shape, lambda i: (0, 0))
    in_specs = ([row(ATT_OUT)] * len(att_list)
                + [row(D_MODEL), colblk(CB_GA), colblk(CB_GA + 1), colblk(CB_GB), colblk(CB_GB + 1), row(D_MODEL),
                   _mod_spec(per_row, tm, rows_per_batch, 2), _mod_spec(per_row, tm, rows_per_batch, 4),
                   _mod_spec(per_row, tm, rows_per_batch, 3), full(n2), full(wa), full(wb), full(wo)])
    n_att = 1 if len(att_list) == 1 else len(att_list) // 2
    return pl.pallas_call(
        functools.partial(_mix_kernel, n_att=n_att),
        grid=(n // tm,),
        in_specs=in_specs,
        out_specs=[row(D_MODEL), row(D_MODEL)],
        out_shape=[jax.ShapeDtypeStruct((n, D_MODEL), F32), jax.ShapeDtypeStruct((n, D_MODEL), BF16)],
        compiler_params=_params("arbitrary"),
        name="mix",
    )(*att_list, hg, proj2, proj2, proj2, proj2, x2, mods, mods, mods, n2, wa, wb, wo)


TOK_LANES = 128
CAND_ROWS = ((0, 16), (1, 16), (2, 8), (3, 8), (4, 8), (5, 8), (6, 8), (7, 8))


def _top16_rows(s, ridx, sentinel):
    slabs = [(s[r:r + 8], ridx[r:r + 8]) for r in range(0, s.shape[0], 8)]
    vals, idxs = [], []
    for _ in range(PEER_TOPK):
        level = slabs
        while len(level) > 1:
            nxt = []
            for i in range(0, len(level) - 1, 2):
                (va, ia), (vb, ib) = level[i], level[i + 1]
                keep_a = va >= vb
                nxt.append((jnp.where(keep_a, va, vb), jnp.where(keep_a, ia, ib)))
            level = nxt + ([level[-1]] if len(level) % 2 else [])
        v8, i8 = level[0]
        m = jnp.max(v8, axis=0, keepdims=True)
        am = jnp.min(jnp.where(v8 == m, i8, sentinel), axis=0, keepdims=True)
        vals.append(m)
        idxs.append(am)
        slabs = [(jnp.where(ix == am, NEG_INF, sv), ix) for sv, ix in slabs]
    return jnp.concatenate(vals, axis=0), jnp.concatenate(idxs, axis=0)


def _route_kernel(h2_ref, wq_ref, sk_ref, a_ref, b_ref, g_ref, q_scr, at_scr, bt_scr, gt_scr):
    tm = h2_ref.shape[0]
    tt = TOK_LANES
    q = jnp.dot(h2_ref[...], wq_ref[...], preferred_element_type=F32)
    for hp in range(2 * PEER_HEADS):
        q_scr[hp] = q[:, hp * PEER_HALF:(hp + 1) * PEER_HALF].astype(BF16)
    kidx = lax.broadcasted_iota(jnp.int32, (PEER_KEYS, tt), 0)
    sub16 = lax.broadcasted_iota(jnp.int32, (PEER_TOPK, tt), 0)
    sub8 = lax.broadcasted_iota(jnp.int32, (8, tt), 0)
    cflat = jnp.concatenate([p * PEER_TOPK + (sub16 if nq == 16 else sub8) for p, nq in CAND_ROWS]
                            + [(sub8 + 8) * PEER_TOPK], axis=0)

    def group(gi, carry):
        rows = pl.ds(pl.multiple_of(gi * tt, tt), tt)

        def sub_key_top(h):
            v1, i1 = _top16_rows(_nt_dot(sk_ref[0], q_scr[2 * h, rows, :]), kidx, PEER_KEYS)
            v2, i2 = _top16_rows(_nt_dot(sk_ref[1], q_scr[2 * h + 1, rows, :]), kidx, PEER_KEYS)
            return v1, i1, v2, i2

        def select(h, tops):
            v1, i1, v2, i2 = tops
            cand = jnp.concatenate([v1[p:p + 1] + v2[0:nq] for p, nq in CAND_ROWS] + [v1[8:16] + v2[0:1]], axis=0)
            tv, tp = _top16_rows(cand, cflat, PEER_TOPK * PEER_TOPK)
            e = jnp.exp(tv - tv[0:1])
            g = e / jnp.sum(e, axis=0, keepdims=True)
            pr = tp >> 4
            qr = tp & 15
            a_sel = jnp.zeros((PEER_TOPK, tt), jnp.int32)
            b_sel = jnp.zeros((PEER_TOPK, tt), jnp.int32)
            for p in range(PEER_TOPK):
                a_sel = jnp.where(pr == p, i1[p:p + 1], a_sel)
                b_sel = jnp.where(qr == p, i2[p:p + 1], b_sel)
            first = h * PEER_TOPK if isinstance(h, int) else pl.multiple_of(h * PEER_TOPK, PEER_TOPK)
            slot = pl.ds(first, PEER_TOPK)
            at_scr[slot, :] = a_sel
            bt_scr[slot, :] = b_sel
            gt_scr[slot, :] = g

        def head(h, tops):
            select(h, tops)
            return sub_key_top(h + 1)

        select(PEER_HEADS - 1, lax.fori_loop(0, PEER_HEADS - 1, head, sub_key_top(0)))
        a_ref[rows, :] = at_scr[...].T
        b_ref[rows, :] = bt_scr[...].T
        g_ref[rows, :] = gt_scr[...].T
        return carry

    lax.fori_loop(0, tm // tt, group, 0)


def _route(h2, wq_bf, sk_bf, tm):
    n = h2.shape[0]
    assert tm % TOK_LANES == 0 and n % tm == 0
    row = pl.BlockSpec((tm, 128), lambda i: (i, 0))
    slots = PEER_HEADS * PEER_TOPK
    return pl.pallas_call(
        _route_kernel,
        grid=(n // tm,),
        in_specs=[pl.BlockSpec((tm, D_MODEL), lambda i: (i, 0)),
                  pl.BlockSpec(wq_bf.shape, lambda i: (0, 0)),
                  pl.BlockSpec(sk_bf.shape, lambda i: (0, 0, 0))],
        out_specs=[row, row, row],
        out_shape=[jax.ShapeDtypeStruct((n, 128), jnp.int32), jax.ShapeDtypeStruct((n, 128), jnp.int32),
                   jax.ShapeDtypeStruct((n, 128), F32)],
        scratch_shapes=[pltpu.VMEM((2 * PEER_HEADS, tm, PEER_HALF), BF16),
                        pltpu.VMEM((slots, TOK_LANES), jnp.int32), pltpu.VMEM((slots, TOK_LANES), jnp.int32),
                        pltpu.VMEM((slots, TOK_LANES), F32)],
        compiler_params=_params("arbitrary"),
        name="peer_route",
    )(h2, wq_bf, sk_bf)


def _peer_u_kernel(h2_ref, u_ref, a_ref, b_ref, g_ref, w_ref, act_scr, *, ac):
    c = pl.program_id(1)

    @pl.when(c == 0)
    def _():
        act_scr[...] = jnp.zeros_like(act_scr)

    hc = _nt_dot(h2_ref[...], u_ref[...])
    a_idx = a_ref[...]
    b_idx = b_ref[...]
    act = act_scr[...]
    for i in range(ac):
        gathered = jnp.take_along_axis(hc[:, i * 128:(i + 1) * 128], b_idx, axis=1)
        act = jnp.where(a_idx == c * ac + i, gathered, act)
    act_scr[...] = act

    @pl.when(c == pl.num_programs(1) - 1)
    def _():
        w_ref[...] = g_ref[...] * _gelu_tanh(act)


def _peer_u(h2, u_bf, a_idx, b_idx, gate, tm, ac):
    n = h2.shape[0]
    row = pl.BlockSpec((tm, 128), lambda i, c: (i, 0))
    return pl.pallas_call(
        functools.partial(_peer_u_kernel, ac=ac),
        grid=(n // tm, PEER_KEYS // ac),
        in_specs=[pl.BlockSpec((tm, D_MODEL), lambda i, c: (i, 0)),
                  pl.BlockSpec((ac * 128, D_MODEL), lambda i, c: (c, 0)),
                  row, row, row],
        out_specs=row,
        out_shape=jax.ShapeDtypeStruct((n, 128), F32),
        scratch_shapes=[pltpu.VMEM((tm, 128), F32)],
        compiler_params=_params("arbitrary", "arbitrary"),
        name="peer_u",
    )(h2, u_bf, a_idx, b_idx, gate)


def _peer_v_kernel(a_ref, b_ref, w_ref, v_ref, x1_ref, g2_ref, o_ref, w3_scr, acc_scr, *, ac, tm):
    c = pl.program_id(1)

    @pl.when(c == 0)
    def _():
        sub = lax.broadcasted_iota(jnp.int32, (128, 128), 0)

        def build(n, carry):
            ar = a_ref[pl.ds(n, 1), :]
            br = b_ref[pl.ds(n, 1), :]
            wr = w_ref[pl.ds(n, 1), :]
            at = jnp.where(sub == ar, 1.0, 0.0).astype(BF16)
            rt = jnp.where(sub == br, wr, 0.0).astype(BF16)
            w3_scr[pl.ds(pl.multiple_of(n * W_PITCH, 8), 128), :] = _nt_dot(at, rt)
            return carry

        lax.fori_loop(0, tm, build, 0, unroll=32)

    part = None
    for i in range(0, ac, 2):
        lhs = jnp.concatenate([w3_scr[pl.ds(c * ac + i, tm, stride=W_PITCH), :],
                               w3_scr[pl.ds(c * ac + i + 1, tm, stride=W_PITCH), :]], axis=-1)
        d = jnp.dot(lhs.astype(BF16), v_ref[i * 128:(i + 2) * 128, :], preferred_element_type=F32)
        part = d if part is None else part + d

    @pl.when(c == 0)
    def _():
        acc_scr[...] = part

    @pl.when(c > 0)
    def _():
        acc_scr[...] += part

    @pl.when(c == pl.num_programs(1) - 1)
    def _():
        o_ref[...] = x1_ref[...] + g2_ref[...] * acc_scr[...]


def _peer_v(a_idx, b_idx, wts, v_bf, x1, mods, per_row, rows_per_batch, tm, ac):
    n = x1.shape[0]
    row = pl.BlockSpec((tm, 128), lambda i, c: (i, 0))
    wide = pl.BlockSpec((tm, D_MODEL), lambda i, c: (i, 0))
    return pl.pallas_call(
        functools.partial(_peer_v_kernel, ac=ac, tm=tm),
        grid=(n // tm, PEER_KEYS // ac),
        in_specs=[row, row, row,
                  pl.BlockSpec((ac * 128, D_MODEL), lambda i, c: (c, 0)),
                  wide, _mod_spec(per_row, tm, rows_per_batch, 5)],
        out_specs=wide,
        out_shape=jax.ShapeDtypeStruct((n, D_MODEL), F32),
        scratch_shapes=[pltpu.VMEM((tm * W_PITCH, 128), F32), pltpu.VMEM((tm, D_MODEL), F32)],
        compiler_params=_params("arbitrary", "arbitrary"),
        name="peer_v",
    )(a_idx, b_idx, wts, v_bf, x1, mods)


def _decode_attn_kernel(q_ref, k_ref, v_ref, c0_ref, c1_ref, c2_ref, o_ref):
    caches = (c0_ref, c1_ref, c2_ref)
    lses, outs = [], []
    for g, (win, dil) in enumerate(ATT_GROUPS):
        length = caches[g].shape[-1]
        pos = lax.broadcasted_iota(jnp.int32, (1, length), 1)
        tap = pos % dil == 0
        o_cols, lse_cols = [], []
        for h in range(ATT_HEADS):
            q = q_ref[g][:, h:h + 1]
            kn = k_ref[g][:, h:h + 1]
            vn = v_ref[g][:, h:h + 1]
            s = jnp.sum(caches[g][0, h] * q, axis=0, keepdims=True)
            s = jnp.where(tap, s, NEG_INF)
            s0 = jnp.sum(kn * q, axis=0, keepdims=True)
            m = jnp.maximum(jnp.max(s, axis=1, keepdims=True), s0)
            p = jnp.exp(s - m)
            p0 = jnp.exp(s0 - m)
            l = jnp.sum(p, axis=1, keepdims=True) + p0
            o_cols.append((jnp.sum(caches[g][1, h] * p, axis=1, keepdims=True) + p0 * vn) / l)
            lse_cols.append(m + jnp.log(l))
        outs.append(jnp.concatenate(o_cols, axis=1))
        lses.append(jnp.concatenate(lse_cols, axis=1))
    mx = functools.reduce(jnp.maximum, lses)
    es = [jnp.exp(x - mx) for x in lses]
    o_ref[...] = (es[0] * outs[0] + es[1] * outs[1] + es[2] * outs[2]) / (es[0] + es[1] + es[2])


def _decode_attn(proj_s, caches):
    n = proj_s.shape[0]

    def cols_t(cb):
        x = proj_s[:, cb * COL:cb * COL + ATT_WIDTH].reshape(n, N_GROUPS, ATT_HEADS, ATT_HEAD_DIM)
        return x.transpose(0, 1, 3, 2)

    views, specs = [], []
    for cache in caches:
        views.append(cache.transpose(0, 2, 3, 4, 1))
        specs.append(pl.BlockSpec((None,) + views[-1].shape[1:], lambda i: (i, 0, 0, 0, 0)))
    qkv_spec = pl.BlockSpec((None, N_GROUPS, ATT_HEAD_DIM, ATT_HEADS), lambda i: (i, 0, 0, 0))
    att = pl.pallas_call(
        _decode_attn_kernel,
        grid=(n,),
        in_specs=[qkv_spec, qkv_spec, qkv_spec] + specs,
        out_specs=pl.BlockSpec((None, ATT_HEAD_DIM, ATT_HEADS), lambda i: (i, 0, 0)),
        out_shape=jax.ShapeDtypeStruct((n, ATT_HEAD_DIM, ATT_HEADS), F32),
        compiler_params=_params("arbitrary"),
        name="decode_attn",
    )(cols_t(CB_Q), cols_t(CB_K), cols_t(CB_V), *views)
    return att.transpose(0, 2, 1).reshape(n, ATT_OUT)


def _decode_hgrn_kernel(q0, q1, f0, f1, i0, i1, g0, g1, lbl_ref, hn_ref, st_ref, o_ref, sto_ref, *, bt):
    nh = COL // HG_DK
    lb = _lower_bound(lbl_ref[...])
    for h in range(HG_HEADS):
        qr, fr, ir, gr = ((q0, f0, i0, g0), (q1, f1, i1, g1))[h // nh]
        cols = slice((h % nh) * HG_DK, (h % nh + 1) * HG_DK)
        lbh = lb[:, h * HG_DK:(h + 1) * HG_DK]
        f = lbh + (1.0 - lbh) * jax.nn.sigmoid(fr[:, cols])
        ft = f.T
        kt = 1.0 - ft
        qt = _silu(qr[:, cols]).T
        for i in range(bt):
            v = ir[i:i + 1, cols]
            s_new = ft[:, i:i + 1] * st_ref[i, h] + kt[:, i:i + 1] * v
            sto_ref[i, h] = s_new
            o = jnp.sum(qt[:, i:i + 1] * s_new, axis=0, keepdims=True)
            ms = jnp.mean(o * o, axis=-1, keepdims=True)
            o_ref[i:i + 1, h * HG_DV:(h + 1) * HG_DV] = (o * lax.rsqrt(ms + EPS) * hn_ref[...]
                                                         * _silu(gr[i:i + 1, cols]))


def _decode_hgrn(proj_s, state, lb_logits, hn, bt):
    n = proj_s.shape[0]
    st_spec = pl.BlockSpec((bt, HG_HEADS, HG_DK, HG_DV), lambda i: (i, 0, 0, 0))
    half = lambda cb: pl.BlockSpec((bt, COL), lambda i: (i, cb))
    return pl.pallas_call(
        functools.partial(_decode_hgrn_kernel, bt=bt),
        grid=(n // bt,),
        in_specs=[half(CB_QH), half(CB_QH + 1), half(CB_FH), half(CB_FH + 1), half(CB_IH), half(CB_IH + 1),
                  half(CB_GH), half(CB_GH + 1),
                  pl.BlockSpec(lb_logits.shape, lambda i: (0, 0)),
                  pl.BlockSpec((1, HG_DV), lambda i: (0, 0)),
                  st_spec],
        out_specs=[pl.BlockSpec((bt, HG_HEADS * HG_DV), lambda i: (i, 0)), st_spec],
        out_shape=[jax.ShapeDtypeStruct((n, HG_HEADS * HG_DV), F32),
                   jax.ShapeDtypeStruct(state.shape, F32)],
        compiler_params=_params("arbitrary"),
        name="decode_hgrn",
    )(proj_s, proj_s, proj_s, proj_s, proj_s, proj_s, proj_s, proj_s, lb_logits, hn, state)


def _block_diag_ones(n, seg):
    i = jnp.arange(n)
    return (i[:, None] // seg == i[None, :] // seg)


def _peer(h2, x1, mods, per_row, rows_per_batch, wq_bf, sk_bf, u_bf, v_bf):
    n = h2.shape[0]
    tm_u, ac_u = min(512, n), 32
    tm_v, ac_v = min(256, n), 32
    a_idx, b_idx, gate = _route(h2, wq_bf, sk_bf, min(256, n))
    wts = _peer_u(h2, u_bf, a_idx, b_idx, gate, tm_u, ac_u)
    return _peer_v(a_idx, b_idx, wts, v_bf, x1, mods, per_row, rows_per_batch, tm_v, ac_v)


def _kv_rows_kernel(k_ref, v_ref, o_ref):
    for j, ref in enumerate((k_ref, v_ref)):
        t = ref[...].T
        for h in range(ATT_HEADS):
            o_ref[j, h] = t[h * ATT_HEAD_DIM:(h + 1) * ATT_HEAD_DIM, :]


def _kv_rows(proj3, g, rows):
    b, s, _ = proj3.shape
    tr = min(256, rows)
    assert rows % tr == 0 and (s - rows) % tr == 0
    first = (s - rows) // tr
    out = pl.pallas_call(
        _kv_rows_kernel,
        grid=(b, rows // tr),
        in_specs=[pl.BlockSpec((None, tr, COL), lambda bi, t: (bi, first + t, CB_K + g)),
                  pl.BlockSpec((None, tr, COL), lambda bi, t: (bi, first + t, CB_V + g))],
        out_specs=pl.BlockSpec((None, 2, ATT_HEADS, ATT_HEAD_DIM, tr), lambda bi, t: (bi, 0, 0, 0, t)),
        out_shape=jax.ShapeDtypeStruct((b, 2, ATT_HEADS, ATT_HEAD_DIM, rows), F32),
        compiler_params=_params("arbitrary", "arbitrary"),
        name=f"kv_rows_g{g}",
    )(proj3, proj3)
    return out.transpose(0, 4, 1, 2, 3)


def kernel(x_prompt, x_sample, cache_kv_w128, cache_kv_w512, cache_kv_w2048, state_hgrn, c_prompt, c_sample, w_ada, b_ada, norm1_w, norm2_w, w_in, q_norm_w, k_norm_w, hg_lb_logits, hg_norm_w, w_br_a, w_br_b, w_o, w_peer_q, peer_subkeys, peer_u, peer_v):
    bsz, seq, _ = x_prompt.shape
    dec, dec_t, _ = x_sample.shape
    assert w_ada.shape[0] == 1 and dec_t == 1 and seq % (ATT_GROUPS[-1][1] * SPAN) == 0
    for (win, dil), cache in zip(ATT_GROUPS, (cache_kv_w128, cache_kv_w512, cache_kv_w2048)):
        assert win == dil * SPAN and cache.shape[2] == win

    w_ada_bf = w_ada[0].astype(BF16)
    w_in_bf = w_in[0].astype(BF16)
    wa, wb, wo = w_br_a[0].astype(BF16), w_br_b[0].astype(BF16), w_o[0].astype(BF16)
    wq_bf = w_peer_q[0].astype(BF16)
    sk_bf = peer_subkeys[0].astype(BF16)
    u_bf = peer_u[0].astype(BF16)
    v_bf = peer_v[0].astype(BF16)
    n1 = norm1_w[0].reshape(1, D_MODEL)
    n2 = norm2_w[0].reshape(1, D_MODEL)
    qn = jnp.tile(q_norm_w[0], N_GROUPS * ATT_HEADS).reshape(1, ATT_WIDTH)
    kn = jnp.tile(k_norm_w[0], N_GROUPS * ATT_HEADS).reshape(1, ATT_WIDTH)
    hn = hg_norm_w[0].reshape(1, HG_DV)
    seg = _block_diag_ones(MXU_TILE, ATT_HEAD_DIM).astype(BF16)
    tri = _block_diag_ones(128, HG_CHUNK) & (jnp.arange(128)[:, None] >= jnp.arange(128)[None, :])
    tri = jnp.tile(tri.astype(BF16), (1, 3))

    mods = _mods(jnp.concatenate([c_prompt, c_sample], axis=0), w_ada_bf, b_ada)
    mods_p = mods[:bsz].reshape(bsz, 1, 6 * D_MODEL)
    mods_s = mods[bsz:]

    n_p = bsz * seq
    tm_p = 512
    xp2 = x_prompt.reshape(n_p, D_MODEL)
    proj_p = _inproj(xp2, mods_p, False, seq, 1024, n1, w_in_bf, qn, kn, seg)
    proj_p3 = proj_p.reshape(bsz, seq, IN_WIDTH)
    att_p = _attn_prompt(proj_p3)
    hg_p, st_p = _hgrn_prompt(proj_p3, hg_lb_logits, hn, tri, 256)
    x1_p, h2_p = _mix([att_p], hg_p, proj_p, xp2, mods_p, False, seq, tm_p, n2, wa, wb, wo)
    y_p = _peer(h2_p, x1_p, mods_p, False, seq, wq_bf, sk_bf, u_bf, v_bf)

    xs2 = x_sample.reshape(dec, D_MODEL)
    proj_s = _inproj(xs2, mods_s, True, 1, dec, n1, w_in_bf, qn, kn, seg)
    att_s = _decode_attn(proj_s, (cache_kv_w128[0], cache_kv_w512[0], cache_kv_w2048[0]))
    hg_s, st_s = _decode_hgrn(proj_s, state_hgrn[0], hg_lb_logits, hn, 8)
    x1_s, h2_s = _mix([att_s], hg_s, proj_s, xs2, mods_s, True, 1, dec, n2, wa, wb, wo)
    y_s = _peer(h2_s, x1_s, mods_s, True, 1, wq_bf, sk_bf, u_bf, v_bf)

    kv_p = [_kv_rows(proj_p3, g, min(win, seq))[None] for g, (win, _) in enumerate(ATT_GROUPS)]
    proj_s3 = proj_s.reshape(1, dec, IN_WIDTH)
    kv_s = [_kv_rows(proj_s3, g, dec).reshape(1, dec, 1, 2, ATT_HEADS, ATT_HEAD_DIM) for g in range(N_GROUPS)]
    return (y_p.reshape(bsz, seq, D_MODEL), y_s.reshape(dec, 1, D_MODEL), kv_p[0], kv_p[1], kv_p[2], st_p[None],
            kv_s[0], kv_s[1], kv_s[2], st_s[None])
```

```python
import functools

import jax
import jax.numpy as jnp
from jax import lax
from jax.experimental import pallas as pl
from jax.experimental.pallas import tpu as pltpu

F32 = jnp.float32
BF16 = jnp.bfloat16

D_MODEL = 1024
ATT_GROUPS = ((128, 1), (512, 4), (2048, 16))
N_GROUPS = 3
ATT_HEADS = 8
ATT_HEAD_DIM = 64
ATT_OUT = ATT_HEADS * ATT_HEAD_DIM
ATT_WIDTH = N_GROUPS * ATT_OUT
MXU_TILE = 256
SPAN = 128
HG_HEADS = 8
HG_DK = 128
HG_DV = 128
PEER_HEADS = 8
PEER_KEYS = 128
PEER_TOPK = 16
PEER_HALF = 128
EPS = 1e-6
IN_WIDTH = 10752
COL = 512
NCOL = IN_WIDTH // COL
CB_Q, CB_K, CB_V = 0, 3, 6
CB_QH, CB_FH, CB_IH, CB_GH, CB_GA, CB_GB = 9, 11, 13, 15, 17, 19
HG_CHUNK = 16
W_PITCH = 136
VMEM_LIMIT = 56 * 1024 * 1024

NEG_INF = float("-inf")


def _silu(x):
    return x * jax.nn.sigmoid(x)


def _gelu_tanh(x):
    return 0.5 * x * (1.0 + jnp.tanh(0.7978845608028654 * (x + 0.044715 * (x * x * x))))


def _nt_dot(a, b):
    return lax.dot_general(a, b, (((1,), (1,)), ((), ())), preferred_element_type=F32)


def _params(*sem):
    return pltpu.CompilerParams(dimension_semantics=sem, vmem_limit_bytes=VMEM_LIMIT)


def _mods_kernel(c_ref, w_ref, b_ref, o_ref):
    s = _silu(c_ref[...])
    o_ref[...] = jnp.dot(s.astype(BF16), w_ref[...], preferred_element_type=F32) + b_ref[...]


def _mods(c, w_ada_bf, b_ada):
    n = c.shape[0]
    return pl.pallas_call(
        _mods_kernel,
        grid=(6,),
        in_specs=[pl.BlockSpec((n, D_MODEL), lambda j: (0, 0)),
                  pl.BlockSpec((D_MODEL, D_MODEL), lambda j: (0, j)),
                  pl.BlockSpec((1, D_MODEL), lambda j: (0, j))],
        out_specs=pl.BlockSpec((n, D_MODEL), lambda j: (0, j)),
        out_shape=jax.ShapeDtypeStruct((n, 6 * D_MODEL), F32),
        compiler_params=_params("arbitrary"),
        name="mods",
    )(c, w_ada_bf, b_ada)


def _mod_spec(per_row, tm, rows_per_batch, k):
    if per_row:
        return pl.BlockSpec((tm, D_MODEL), lambda i, *_: (i, k))
    tiles = rows_per_batch // tm
    return pl.BlockSpec((None, 1, D_MODEL), lambda i, *_: (i // tiles, 0, k))


def _inproj_kernel(x_ref, sc_ref, sh_ref, n1_ref, w_ref, qn_ref, kn_ref, seg_ref, o_ref, h_scr):
    j = pl.program_id(1)

    @pl.when(j == 0)
    def _():
        x = x_ref[...]
        ms = jnp.mean(x * x, axis=-1, keepdims=True)
        xn = x * lax.rsqrt(ms + EPS) * n1_ref[...]
        h_scr[...] = (xn * (1.0 + sc_ref[...]) + sh_ref[...]).astype(BF16)

    acc = jnp.dot(h_scr[...], w_ref[...], preferred_element_type=F32)

    def head_norm(w_row, scale):
        sq = (acc * acc).astype(BF16)
        seg_w = seg_ref.shape[0]
        ss = jnp.concatenate([jnp.dot(sq[:, c:c + seg_w], seg_ref[...], preferred_element_type=F32)
                              for c in range(0, ATT_WIDTH, seg_w)], axis=-1)
        return acc * lax.rsqrt(ss * (1.0 / ATT_HEAD_DIM) + EPS) * w_row * scale

    @pl.when(j == 0)
    def _():
        o_ref[...] = head_norm(qn_ref[...], ATT_HEAD_DIM ** -0.5)

    @pl.when(j == 1)
    def _():
        o_ref[...] = head_norm(kn_ref[...], 1.0)

    @pl.when(j >= 2)
    def _():
        o_ref[...] = acc


def _inproj(x2, mods, per_row, rows_per_batch, tm, n1, w_in_bf, qn, kn, seg):
    n = x2.shape[0]
    const = lambda i, j: (0, 0)
    return pl.pallas_call(
        _inproj_kernel,
        grid=(n // tm, IN_WIDTH // ATT_WIDTH),
        in_specs=[pl.BlockSpec((tm, D_MODEL), lambda i, j: (i, 0)),
                  _mod_spec(per_row, tm, rows_per_batch, 1),
                  _mod_spec(per_row, tm, rows_per_batch, 0),
                  pl.BlockSpec((1, D_MODEL), const),
                  pl.BlockSpec((D_MODEL, ATT_WIDTH), lambda i, j: (0, j)),
                  pl.BlockSpec((1, ATT_WIDTH), const),
                  pl.BlockSpec((1, ATT_WIDTH), const),
                  pl.BlockSpec(seg.shape, const)],
        out_specs=pl.BlockSpec((tm, ATT_WIDTH), lambda i, j: (i, j)),
        out_shape=jax.ShapeDtypeStruct((n, IN_WIDTH), F32),
        scratch_shapes=[pltpu.VMEM((tm, D_MODEL), BF16)],
        compiler_params=_params("arbitrary", "arbitrary"),
        name="inproj",
    )(x2, mods, mods, n1, w_in_bf, qn, kn, seg)


ATT_TILE = ATT_GROUPS[-1][1] * SPAN
PAIR = 2 * ATT_HEAD_DIM


def _class_rows(ref, start, n, dil):
    if dil == 1:
        return ref[pl.ds(start, n), :]
    return ref[pl.ds(start, n, stride=dil), :]


def _attn_fused_kernel(*refs):
    in_refs, (o_ref, og_scr, lg_scr) = refs[:5 * N_GROUPS], refs[5 * N_GROUPS:]
    tile = pl.program_id(1)
    lane = lax.broadcasted_iota(jnp.int32, (SPAN, PAIR), 1)
    lo = lane < ATT_HEAD_DIM
    qi = lax.broadcasted_iota(jnp.int32, (SPAN, 2 * SPAN), 0)
    kr = lax.broadcasted_iota(jnp.int32, (SPAN, 2 * SPAN), 1)
    delta = qi + SPAN - kr
    band = (delta >= 0) & (delta <= SPAN)
    cur_keys = kr >= SPAN

    def block(g, dil, r, jq, q_ref, k_ref, kp_ref, v_ref, vp_ref):
        q = _class_rows(q_ref, r + dil * SPAN * jq, SPAN, dil).astype(BF16)
        if jq == 0:
            k = jnp.concatenate([_class_rows(kp_ref, r, SPAN, dil), _class_rows(k_ref, r, SPAN, dil)], axis=0)
            v = jnp.concatenate([_class_rows(vp_ref, r, SPAN, dil), _class_rows(v_ref, r, SPAN, dil)], axis=0)
            valid = band & (cur_keys | (tile > 0))
        else:
            k = _class_rows(k_ref, r + dil * SPAN * (jq - 1), 2 * SPAN, dil)
            v = _class_rows(v_ref, r + dil * SPAN * (jq - 1), 2 * SPAN, dil)
            valid = band
        k = k.astype(BF16)
        v = v.astype(BF16)
        o_pair, lse_pair = None, None
        for first in (True, False):
            mine = lo if first else jnp.logical_not(lo)
            qh = jnp.where(mine, q, jnp.zeros_like(q))
            s = jnp.where(valid, _nt_dot(qh, k), NEG_INF)
            m = jnp.max(s, axis=-1, keepdims=True)
            p = jnp.exp(s - m)
            l = jnp.sum(p, axis=-1, keepdims=True)
            oh = jnp.dot(p.astype(BF16), v, preferred_element_type=F32) / l
            lse = m + jnp.log(l)
            o_pair = oh if first else jnp.where(lo, o_pair, oh)
            lse_pair = jnp.broadcast_to(lse, (SPAN, PAIR)) if first else jnp.where(lo, lse_pair, lse)
        start = r + dil * SPAN * jq
        if dil == 1:
            og_scr[g, pl.ds(start, SPAN), :] = o_pair
            lg_scr[g, pl.ds(start, SPAN), :] = lse_pair
        else:
            og_scr[g, pl.ds(start, SPAN, stride=dil), :] = o_pair
            lg_scr[g, pl.ds(start, SPAN, stride=dil), :] = lse_pair

    for g, (win, dil) in enumerate(ATT_GROUPS):
        grefs = in_refs[5 * g:5 * g + 5]
        nblk = ATT_TILE // (dil * SPAN)
        if dil == 1:
            for jq in range(nblk):
                block(g, dil, 0, jq, *grefs)
        else:
            def residue(r, carry, g=g, dil=dil, nblk=nblk, grefs=grefs):
                for jq in range(nblk):
                    block(g, dil, r, jq, *grefs)
                return carry
            lax.fori_loop(0, dil, residue, 0, unroll=max(1, 4 // nblk))

    lses = [lg_scr[g] for g in range(N_GROUPS)]
    mx = functools.reduce(jnp.maximum, lses)
    es = [jnp.exp(x - mx) for x in lses]
    num = es[0] * og_scr[0] + es[1] * og_scr[1] + es[2] * og_scr[2]
    o_ref[...] = (num / (es[0] + es[1] + es[2])).astype(o_ref.dtype)


def _attn_prompt(proj3):
    b, s, _ = proj3.shape
    npair = ATT_OUT // PAIR
    in_specs = []
    for g, (win, dil) in enumerate(ATT_GROUPS):
        prev_rows = dil * SPAN
        per_tile = ATT_TILE // prev_rows
        for cb in (CB_Q, CB_K, CB_V):
            col = lambda hp, cb=cb, g=g: (cb + g) * (COL // PAIR) + hp
            in_specs.append(pl.BlockSpec((None, ATT_TILE, PAIR), lambda bi, t, hp, col=col: (bi, t, col(hp))))
            if cb != CB_Q:
                in_specs.append(pl.BlockSpec(
                    (None, prev_rows, PAIR),
                    lambda bi, t, hp, col=col, per_tile=per_tile: (bi, jnp.maximum(t * per_tile - 1, 0), col(hp))))
    att = pl.pallas_call(
        _attn_fused_kernel,
        grid=(b, s // ATT_TILE, npair),
        in_specs=in_specs,
        out_specs=pl.BlockSpec((None, ATT_TILE, PAIR), lambda bi, t, hp: (bi, t, hp)),
        out_shape=jax.ShapeDtypeStruct((b, s, ATT_OUT), BF16),
        scratch_shapes=[pltpu.VMEM((N_GROUPS, ATT_TILE, PAIR), F32), pltpu.VMEM((N_GROUPS, ATT_TILE, PAIR), F32)],
        compiler_params=_params("arbitrary", "arbitrary", "arbitrary"),
        name="attn_prompt",
    )(*([proj3] * len(in_specs)))
    return att.reshape(b * s, ATT_OUT)


def _lower_bound(logits):
    mx = jnp.max(logits, axis=0, keepdims=True)
    e = jnp.exp(logits - mx)
    return e[0:1] / jnp.sum(e, axis=0, keepdims=True)


HG_SEQS = 1


def _hgrn_kernel(qh_ref, fh_ref, ih_ref, gh_ref, lbl_ref, hn_ref, tri_ref, o_ref, st_ref,
                 st_scr, q_scr, k_scr, b_scr, *, tb):
    t = pl.program_id(2)
    nh = COL // HG_DK
    c_rows = HG_CHUNK
    rows_all = HG_SEQS * tb

    @pl.when(t == 0)
    def _():
        st_scr[...] = jnp.zeros_like(st_scr)

    lb = _lower_bound(lbl_ref[...])
    f = lb + (1.0 - lb) * jax.nn.sigmoid(fh_ref[...].reshape(rows_all, COL))
    k_scr[...] = 1.0 - f
    q_scr[...] = _silu(qh_ref[...].reshape(rows_all, COL))
    logf = jnp.log(f)
    hi = logf.astype(BF16)
    rest = logf - hi.astype(F32)
    mid = rest.astype(BF16)
    low = (rest - mid.astype(F32)).astype(BF16)
    for r in range(rows_all // 128):
        rows = slice(r * 128, (r + 1) * 128)
        pieces = jnp.concatenate([hi[rows, :], mid[rows, :], low[rows, :]], axis=0)
        b_scr[rows, :] = jnp.dot(tri_ref[...], pieces, preferred_element_type=F32)
    rowid8 = lax.broadcasted_iota(jnp.int32, (8, 1), 0)

    def chunk(c, carry):
        r0 = pl.multiple_of(c * c_rows, c_rows)
        rows = pl.ds(r0, c_rows)
        for sq in range(HG_SEQS):
            srows = pl.ds(pl.multiple_of(sq * tb + r0, c_rows), c_rows)
            for h in range(nh):
                cols = slice(h * HG_DK, (h + 1) * HG_DK)
                b = b_scr[srows, cols]
                qc = q_scr[srows, cols]
                kc = k_scr[srows, cols]
                vc = ih_ref[sq, rows, cols]
                st = st_scr[sq * nh + h]
                o_state = _nt_dot((qc * jnp.exp(b)).astype(BF16), st.astype(BF16))
                prods, where = [], []
                for g8 in range(c_rows // 8):
                    tr = slice(g8 * 8, (g8 + 1) * 8)
                    bg, qg = b[tr], qc[tr]
                    for s in range(g8 * 8 + 8):
                        d = bg - b[s:s + 1]
                        if s >= g8 * 8:
                            d = jnp.where(rowid8 >= s - g8 * 8, d, NEG_INF)
                        prods.append(qg * kc[s:s + 1] * jnp.exp(d))
                        where.append((g8, s))
                x = jnp.concatenate(prods, axis=0).astype(BF16)
                sums = jnp.dot(x, jnp.ones((HG_DK, HG_DV), BF16), preferred_element_type=F32)
                parts = [o_state[g8 * 8:(g8 + 1) * 8] for g8 in range(c_rows // 8)]
                for i, (g8, s) in enumerate(where):
                    parts[g8] = parts[g8] + sums[i * 8:(i + 1) * 8] * vc[s:s + 1]
                o = jnp.concatenate(parts, axis=0)
                bl = b[c_rows - 1:c_rows]
                kt = kc * jnp.exp(bl - b)
                upd = lax.dot_general(vc.astype(BF16), kt.astype(BF16), (((0,), (0,)), ((), ())),
                                      preferred_element_type=F32)
                st_scr[sq * nh + h] = st * jnp.exp(bl) + upd
                ms = jnp.mean(o * o, axis=-1, keepdims=True)
                on = o * lax.rsqrt(ms + EPS) * hn_ref[...]
                o_ref[sq, rows, cols] = on * _silu(gh_ref[sq, rows, cols])
        return carry

    lax.fori_loop(0, tb // c_rows, chunk, 0, unroll=2)

    @pl.when(t == pl.num_programs(2) - 1)
    def _():
        for sq in range(HG_SEQS):
            for h in range(nh):
                st_ref[sq, h] = st_scr[sq * nh + h].T


def _hgrn_prompt(proj3, lb_logits, hn, tri, tb):
    b, s, _ = proj3.shape
    assert b % HG_SEQS == 0
    nhb = HG_HEADS * HG_DK // COL
    nh = COL // HG_DK

    def spec(cb):
        return pl.BlockSpec((HG_SEQS, tb, COL), lambda bi, hb, t: (bi, t, cb + hb))

    o, st = pl.pallas_call(
        functools.partial(_hgrn_kernel, tb=tb),
        grid=(b // HG_SEQS, nhb, s // tb),
        in_specs=[spec(CB_QH), spec(CB_FH), spec(CB_IH), spec(CB_GH),
                  pl.BlockSpec((lb_logits.shape[0], COL), lambda bi, hb, t: (0, hb)),
                  pl.BlockSpec((1, HG_DV), lambda bi, hb, t: (0, 0)),
                  pl.BlockSpec(tri.shape, lambda bi, hb, t: (0, 0))],
        out_specs=[pl.BlockSpec((HG_SEQS, tb, COL), lambda bi, hb, t: (bi, t, hb)),
                   pl.BlockSpec((HG_SEQS, nh, HG_DK, HG_DV), lambda bi, hb, t: (bi, hb, 0, 0))],
        out_shape=[jax.ShapeDtypeStruct((b, s, HG_HEADS * HG_DV), F32),
                   jax.ShapeDtypeStruct((b, HG_HEADS, HG_DK, HG_DV), F32)],
        scratch_shapes=[pltpu.VMEM((HG_SEQS * nh, HG_DV, HG_DK), F32)]
                       + [pltpu.VMEM((HG_SEQS * tb, COL), F32)] * 3,
        compiler_params=_params("arbitrary", "arbitrary", "arbitrary"),
        name="hgrn_prompt",
    )(proj3, proj3, proj3, proj3, lb_logits, hn, tri)
    return o.reshape(b * s, HG_HEADS * HG_DV), st


def _mix_kernel(att_ref, hg_ref, ga0, ga1, gb0, gb1, x_ref, g1_ref, sc2_ref, sh2_ref, n2_ref,
                wa_ref, wb_ref, wo_ref, x1_ref, h2_ref):
    att = att_ref[...]
    ga = jnp.concatenate([ga0[...], ga1[...]], axis=-1)
    gb = jnp.concatenate([gb0[...], gb1[...]], axis=-1)
    ya = jnp.dot(att.astype(BF16), wa_ref[...], preferred_element_type=F32)
    yb = jnp.dot(hg_ref[...].astype(BF16), wb_ref[...], preferred_element_type=F32)
    y = jax.nn.sigmoid(ga) * ya + jax.nn.sigmoid(gb) * yb
    x1 = x_ref[...] + g1_ref[...] * jnp.dot(y.astype(BF16), wo_ref[...], preferred_element_type=F32)
    x1_ref[...] = x1
    ms = jnp.mean(x1 * x1, axis=-1, keepdims=True)
    xn = x1 * lax.rsqrt(ms + EPS) * n2_ref[...]
    h2_ref[...] = (xn * (1.0 + sc2_ref[...]) + sh2_ref[...]).astype(BF16)


def _mix(att, hg, proj2, x2, mods, per_row, rows_per_batch, tm, n2, wa, wb, wo):
    n = x2.shape[0]
    row = lambda w: pl.BlockSpec((tm, w), lambda i: (i, 0))
    colblk = lambda cb: pl.BlockSpec((tm, COL), lambda i: (i, cb))
    full = lambda a: pl.BlockSpec(a.shape, lambda i: (0, 0))
    in_specs = [row(ATT_OUT), row(D_MODEL), colblk(CB_GA), colblk(CB_GA + 1), colblk(CB_GB), colblk(CB_GB + 1),
                row(D_MODEL), _mod_spec(per_row, tm, rows_per_batch, 2), _mod_spec(per_row, tm, rows_per_batch, 4),
                _mod_spec(per_row, tm, rows_per_batch, 3), full(n2), full(wa), full(wb), full(wo)]
    return pl.pallas_call(
        _mix_kernel,
        grid=(n // tm,),
        in_specs=in_specs,
        out_specs=[row(D_MODEL), row(D_MODEL)],
        out_shape=[jax.ShapeDtypeStruct((n, D_MODEL), F32), jax.ShapeDtypeStruct((n, D_MODEL), BF16)],
        compiler_params=_params("arbitrary"),
        name="mix",
    )(att, hg, proj2, proj2, proj2, proj2, x2, mods, mods, mods, n2, wa, wb, wo)


TOK_LANES = 128
CAND_ROWS = ((0, 16), (1, 16), (2, 8), (3, 8), (4, 8), (5, 8), (6, 8), (7, 8))


def _top16_rows(s, ridx, sentinel):
    slabs = [(s[r:r + 8], ridx[r:r + 8]) for r in range(0, s.shape[0], 8)]
    vals, idxs = [], []
    for _ in range(PEER_TOPK):
        level = slabs
        while len(level) > 1:
            nxt = []
            for i in range(0, len(level) - 1, 2):
                (va, ia), (vb, ib) = level[i], level[i + 1]
                keep_a = va >= vb
                nxt.append((jnp.where(keep_a, va, vb), jnp.where(keep_a, ia, ib)))
            level = nxt + ([level[-1]] if len(level) % 2 else [])
        v8, i8 = level[0]
        m = jnp.max(v8, axis=0, keepdims=True)
        am = jnp.min(jnp.where(v8 == m, i8, sentinel), axis=0, keepdims=True)
        vals.append(m)
        idxs.append(am)
        slabs = [(jnp.where(ix == am, NEG_INF, sv), ix) for sv, ix in slabs]
    return jnp.concatenate(vals, axis=0), jnp.concatenate(idxs, axis=0)


def _route_kernel(h2_ref, wq_ref, sk_ref, a_ref, b_ref, g_ref, q_scr, at_scr, bt_scr, gt_scr):
    tm = h2_ref.shape[0]
    tt = TOK_LANES
    q = jnp.dot(h2_ref[...], wq_ref[...], preferred_element_type=F32)
    for hp in range(2 * PEER_HEADS):
        q_scr[hp] = q[:, hp * PEER_HALF:(hp + 1) * PEER_HALF].astype(BF16)
    kidx = lax.broadcasted_iota(jnp.int32, (PEER_KEYS, tt), 0)
    sub16 = lax.broadcasted_iota(jnp.int32, (PEER_TOPK, tt), 0)
    sub8 = lax.broadcasted_iota(jnp.int32, (8, tt), 0)
    cflat = jnp.concatenate([p * PEER_TOPK + (sub16 if nq == 16 else sub8) for p, nq in CAND_ROWS]
                            + [(sub8 + 8) * PEER_TOPK], axis=0)

    def group(gi, carry):
        rows = pl.ds(pl.multiple_of(gi * tt, tt), tt)

        def sub_key_top(h):
            v1, i1 = _top16_rows(_nt_dot(sk_ref[0], q_scr[2 * h, rows, :]), kidx, PEER_KEYS)
            v2, i2 = _top16_rows(_nt_dot(sk_ref[1], q_scr[2 * h + 1, rows, :]), kidx, PEER_KEYS)
            return v1, i1, v2, i2

        def select(h, tops):
            v1, i1, v2, i2 = tops
            cand = jnp.concatenate([v1[p:p + 1] + v2[0:nq] for p, nq in CAND_ROWS] + [v1[8:16] + v2[0:1]], axis=0)
            tv, tp = _top16_rows(cand, cflat, PEER_TOPK * PEER_TOPK)
            e = jnp.exp(tv - tv[0:1])
            g = e / jnp.sum(e, axis=0, keepdims=True)
            pr = tp >> 4
            qr = tp & 15
            a_sel = jnp.zeros((PEER_TOPK, tt), jnp.int32)
            b_sel = jnp.zeros((PEER_TOPK, tt), jnp.int32)
            for p in range(PEER_TOPK):
                a_sel = jnp.where(pr == p, i1[p:p + 1], a_sel)
                b_sel = jnp.where(qr == p, i2[p:p + 1], b_sel)
            first = h * PEER_TOPK if isinstance(h, int) else pl.multiple_of(h * PEER_TOPK, PEER_TOPK)
            slot = pl.ds(first, PEER_TOPK)
            at_scr[slot, :] = a_sel
            bt_scr[slot, :] = b_sel
            gt_scr[slot, :] = g

        def head(h, tops):
            select(h, tops)
            return sub_key_top(h + 1)

        select(PEER_HEADS - 1, lax.fori_loop(0, PEER_HEADS - 1, head, sub_key_top(0)))
        a_ref[rows, :] = at_scr[...].T
        b_ref[rows, :] = bt_scr[...].T
        g_ref[rows, :] = gt_scr[...].T
        return carry

    lax.fori_loop(0, tm // tt, group, 0)


def _route(h2, wq_bf, sk_bf, tm):
    n = h2.shape[0]
    assert tm % TOK_LANES == 0 and n % tm == 0
    row = pl.BlockSpec((tm, 128), lambda i: (i, 0))
    slots = PEER_HEADS * PEER_TOPK
    return pl.pallas_call(
        _route_kernel,
        grid=(n // tm,),
        in_specs=[pl.BlockSpec((tm, D_MODEL), lambda i: (i, 0)),
                  pl.BlockSpec(wq_bf.shape, lambda i: (0, 0)),
                  pl.BlockSpec(sk_bf.shape, lambda i: (0, 0, 0))],
        out_specs=[row, row, row],
        out_shape=[jax.ShapeDtypeStruct((n, 128), jnp.int32), jax.ShapeDtypeStruct((n, 128), jnp.int32),
                   jax.ShapeDtypeStruct((n, 128), F32)],
        scratch_shapes=[pltpu.VMEM((2 * PEER_HEADS, tm, PEER_HALF), BF16),
                        pltpu.VMEM((slots, TOK_LANES), jnp.int32), pltpu.VMEM((slots, TOK_LANES), jnp.int32),
                        pltpu.VMEM((slots, TOK_LANES), F32)],
        compiler_params=_params("arbitrary"),
        name="peer_route",
    )(h2, wq_bf, sk_bf)


def _peer_u_kernel(h2_ref, u_ref, a_ref, b_ref, g_ref, w_ref, act_scr, *, ac):
    c = pl.program_id(1)

    @pl.when(c == 0)
    def _():
        act_scr[...] = jnp.zeros_like(act_scr)

    hc = _nt_dot(h2_ref[...], u_ref[...])
    a_idx = a_ref[...]
    b_idx = b_ref[...]
    act = act_scr[...]
    for i in range(ac):
        gathered = jnp.take_along_axis(hc[:, i * 128:(i + 1) * 128], b_idx, axis=1)
        act = jnp.where(a_idx == c * ac + i, gathered, act)
    act_scr[...] = act

    @pl.when(c == pl.num_programs(1) - 1)
    def _():
        w_ref[...] = g_ref[...] * _gelu_tanh(act)


def _peer_u(h2, u_bf, a_idx, b_idx, gate, tm, ac):
    n = h2.shape[0]
    row = pl.BlockSpec((tm, 128), lambda i, c: (i, 0))
    return pl.pallas_call(
        functools.partial(_peer_u_kernel, ac=ac),
        grid=(n // tm, PEER_KEYS // ac),
        in_specs=[pl.BlockSpec((tm, D_MODEL), lambda i, c: (i, 0)),
                  pl.BlockSpec((ac * 128, D_MODEL), lambda i, c: (c, 0)),
                  row, row, row],
        out_specs=row,
        out_shape=jax.ShapeDtypeStruct((n, 128), F32),
        scratch_shapes=[pltpu.VMEM((tm, 128), F32)],
        compiler_params=_params("arbitrary", "arbitrary"),
        name="peer_u",
    )(h2, u_bf, a_idx, b_idx, gate)


def _peer_v_kernel(a_ref, b_ref, w_ref, v_ref, x1_ref, g2_ref, o_ref, w3_scr, acc_scr, *, ac, tm):
    c = pl.program_id(1)

    @pl.when(c == 0)
    def _():
        sub = lax.broadcasted_iota(jnp.int32, (128, 128), 0)

        def build(n, carry):
            ar = a_ref[pl.ds(n, 1), :]
            br = b_ref[pl.ds(n, 1), :]
            wr = w_ref[pl.ds(n, 1), :]
            at = jnp.where(sub == ar, 1.0, 0.0).astype(BF16)
            rt = jnp.where(sub == br, wr, 0.0).astype(BF16)
            w3_scr[pl.ds(pl.multiple_of(n * W_PITCH, 8), 128), :] = _nt_dot(at, rt)
            return carry

        lax.fori_loop(0, tm, build, 0, unroll=32)

    part = None
    for i in range(0, ac, 2):
        lhs = jnp.concatenate([w3_scr[pl.ds(c * ac + i, tm, stride=W_PITCH), :],
                               w3_scr[pl.ds(c * ac + i + 1, tm, stride=W_PITCH), :]], axis=-1)
        d = jnp.dot(lhs.astype(BF16), v_ref[i * 128:(i + 2) * 128, :], preferred_element_type=F32)
        part = d if part is None else part + d

    @pl.when(c == 0)
    def _():
        acc_scr[...] = part

    @pl.when(c > 0)
    def _():
        acc_scr[...] += part

    @pl.when(c == pl.num_programs(1) - 1)
    def _():
        o_ref[...] = x1_ref[...] + g2_ref[...] * acc_scr[...]


def _peer_v(a_idx, b_idx, wts, v_bf, x1, mods, per_row, rows_per_batch, tm, ac):
    n = x1.shape[0]
    row = pl.BlockSpec((tm, 128), lambda i, c: (i, 0))
    wide = pl.BlockSpec((tm, D_MODEL), lambda i, c: (i, 0))
    return pl.pallas_call(
        functools.partial(_peer_v_kernel, ac=ac, tm=tm),
        grid=(n // tm, PEER_KEYS // ac),
        in_specs=[row, row, row,
                  pl.BlockSpec((ac * 128, D_MODEL), lambda i, c: (c, 0)),
                  wide, _mod_spec(per_row, tm, rows_per_batch, 5)],
        out_specs=wide,
        out_shape=jax.ShapeDtypeStruct((n, D_MODEL), F32),
        scratch_shapes=[pltpu.VMEM((tm * W_PITCH, 128), F32), pltpu.VMEM((tm, D_MODEL), F32)],
        compiler_params=_params("arbitrary", "arbitrary"),
        name="peer_v",
    )(a_idx, b_idx, wts, v_bf, x1, mods)


def _decode_attn_kernel(q_ref, k_ref, v_ref, c0_ref, c1_ref, c2_ref, o_ref):
    caches = (c0_ref, c1_ref, c2_ref)
    lses, outs = [], []
    for g, (win, dil) in enumerate(ATT_GROUPS):
        length = caches[g].shape[-1]
        pos = lax.broadcasted_iota(jnp.int32, (1, length), 1)
        tap = pos % dil == 0
        o_cols, lse_cols = [], []
        for h in range(ATT_HEADS):
            q = q_ref[g][:, h:h + 1]
            kn = k_ref[g][:, h:h + 1]
            vn = v_ref[g][:, h:h + 1]
            s = jnp.sum(caches[g][0, h] * q, axis=0, keepdims=True)
            s = jnp.where(tap, s, NEG_INF)
            s0 = jnp.sum(kn * q, axis=0, keepdims=True)
            m = jnp.maximum(jnp.max(s, axis=1, keepdims=True), s0)
            p = jnp.exp(s - m)
            p0 = jnp.exp(s0 - m)
            l = jnp.sum(p, axis=1, keepdims=True) + p0
            o_cols.append((jnp.sum(caches[g][1, h] * p, axis=1, keepdims=True) + p0 * vn) / l)
            lse_cols.append(m + jnp.log(l))
        outs.append(jnp.concatenate(o_cols, axis=1))
        lses.append(jnp.concatenate(lse_cols, axis=1))
    mx = functools.reduce(jnp.maximum, lses)
    es = [jnp.exp(x - mx) for x in lses]
    o_ref[...] = (es[0] * outs[0] + es[1] * outs[1] + es[2] * outs[2]) / (es[0] + es[1] + es[2])


def _decode_attn(proj_s, caches):
    n = proj_s.shape[0]

    def cols_t(cb):
        x = proj_s[:, cb * COL:cb * COL + ATT_WIDTH].reshape(n, N_GROUPS, ATT_HEADS, ATT_HEAD_DIM)
        return x.transpose(0, 1, 3, 2)

    views, specs = [], []
    for cache in caches:
        views.append(cache.transpose(0, 2, 3, 4, 1))
        specs.append(pl.BlockSpec((None,) + views[-1].shape[1:], lambda i: (i, 0, 0, 0, 0)))
    qkv_spec = pl.BlockSpec((None, N_GROUPS, ATT_HEAD_DIM, ATT_HEADS), lambda i: (i, 0, 0, 0))
    att = pl.pallas_call(
        _decode_attn_kernel,
        grid=(n,),
        in_specs=[qkv_spec, qkv_spec, qkv_spec] + specs,
        out_specs=pl.BlockSpec((None, ATT_HEAD_DIM, ATT_HEADS), lambda i: (i, 0, 0)),
        out_shape=jax.ShapeDtypeStruct((n, ATT_HEAD_DIM, ATT_HEADS), F32),
        compiler_params=_params("arbitrary"),
        name="decode_attn",
    )(cols_t(CB_Q), cols_t(CB_K), cols_t(CB_V), *views)
    return att.transpose(0, 2, 1).reshape(n, ATT_OUT)


def _decode_hgrn_kernel(q0, q1, f0, f1, i0, i1, g0, g1, lbl_ref, hn_ref, st_ref, o_ref, sto_ref, *, bt):
    nh = COL // HG_DK
    lb = _lower_bound(lbl_ref[...])
    for h in range(HG_HEADS):
        qr, fr, ir, gr = ((q0, f0, i0, g0), (q1, f1, i1, g1))[h // nh]
        cols = slice((h % nh) * HG_DK, (h % nh + 1) * HG_DK)
        lbh = lb[:, h * HG_DK:(h + 1) * HG_DK]
        f = lbh + (1.0 - lbh) * jax.nn.sigmoid(fr[:, cols])
        ft = f.T
        kt = 1.0 - ft
        qt = _silu(qr[:, cols]).T
        for i in range(bt):
            v = ir[i:i + 1, cols]
            s_new = ft[:, i:i + 1] * st_ref[i, h] + kt[:, i:i + 1] * v
            sto_ref[i, h] = s_new
            o = jnp.sum(qt[:, i:i + 1] * s_new, axis=0, keepdims=True)
            ms = jnp.mean(o * o, axis=-1, keepdims=True)
            o_ref[i:i + 1, h * HG_DV:(h + 1) * HG_DV] = (o * lax.rsqrt(ms + EPS) * hn_ref[...]
                                                         * _silu(gr[i:i + 1, cols]))


def _decode_hgrn(proj_s, state, lb_logits, hn, bt):
    n = proj_s.shape[0]
    st_spec = pl.BlockSpec((bt, HG_HEADS, HG_DK, HG_DV), lambda i: (i, 0, 0, 0))
    half = lambda cb: pl.BlockSpec((bt, COL), lambda i: (i, cb))
    return pl.pallas_call(
        functools.partial(_decode_hgrn_kernel, bt=bt),
        grid=(n // bt,),
        in_specs=[half(CB_QH), half(CB_QH + 1), half(CB_FH), half(CB_FH + 1), half(CB_IH), half(CB_IH + 1),
                  half(CB_GH), half(CB_GH + 1),
                  pl.BlockSpec(lb_logits.shape, lambda i: (0, 0)),
                  pl.BlockSpec((1, HG_DV), lambda i: (0, 0)),
                  st_spec],
        out_specs=[pl.BlockSpec((bt, HG_HEADS * HG_DV), lambda i: (i, 0)), st_spec],
        out_shape=[jax.ShapeDtypeStruct((n, HG_HEADS * HG_DV), F32),
                   jax.ShapeDtypeStruct(state.shape, F32)],
        compiler_params=_params("arbitrary"),
        name="decode_hgrn",
    )(proj_s, proj_s, proj_s, proj_s, proj_s, proj_s, proj_s, proj_s, lb_logits, hn, state)


def _block_diag_ones(n, seg):
    i = jnp.arange(n)
    return (i[:, None] // seg == i[None, :] // seg)


def _peer(h2, x1, mods, per_row, rows_per_batch, wq_bf, sk_bf, u_bf, v_bf):
    n = h2.shape[0]
    tm_u, ac_u = min(512, n), 32
    tm_v, ac_v = min(256, n), 32
    a_idx, b_idx, gate = _route(h2, wq_bf, sk_bf, min(256, n))
    wts = _peer_u(h2, u_bf, a_idx, b_idx, gate, tm_u, ac_u)
    return _peer_v(a_idx, b_idx, wts, v_bf, x1, mods, per_row, rows_per_batch, tm_v, ac_v)


def _kv_rows_kernel(k_ref, v_ref, o_ref):
    for j, ref in enumerate((k_ref, v_ref)):
        t = ref[...].T
        for h in range(ATT_HEADS):
            o_ref[j, h] = t[h * ATT_HEAD_DIM:(h + 1) * ATT_HEAD_DIM, :]


def _kv_rows(proj3, g, rows):
    b, s, _ = proj3.shape
    tr = min(256, rows)
    assert rows % tr == 0 and (s - rows) % tr == 0
    first = (s - rows) // tr
    out = pl.pallas_call(
        _kv_rows_kernel,
        grid=(b, rows // tr),
        in_specs=[pl.BlockSpec((None, tr, COL), lambda bi, t: (bi, first + t, CB_K + g)),
                  pl.BlockSpec((None, tr, COL), lambda bi, t: (bi, first + t, CB_V + g))],
        out_specs=pl.BlockSpec((None, 2, ATT_HEADS, ATT_HEAD_DIM, tr), lambda bi, t: (bi, 0, 0, 0, t)),
        out_shape=jax.ShapeDtypeStruct((b, 2, ATT_HEADS, ATT_HEAD_DIM, rows), F32),
        compiler_params=_params("arbitrary", "arbitrary"),
        name=f"kv_rows_g{g}",
    )(proj3, proj3)
    return out.transpose(0, 4, 1, 2, 3)


def kernel(x_prompt, x_sample, cache_kv_w128, cache_kv_w512, cache_kv_w2048, state_hgrn, c_prompt, c_sample, w_ada, b_ada, norm1_w, norm2_w, w_in, q_norm_w, k_norm_w, hg_lb_logits, hg_norm_w, w_br_a, w_br_b, w_o, w_peer_q, peer_subkeys, peer_u, peer_v):
    bsz, seq, _ = x_prompt.shape
    dec, dec_t, _ = x_sample.shape
    assert w_ada.shape[0] == 1 and dec_t == 1 and seq % (ATT_GROUPS[-1][1] * SPAN) == 0
    for (win, dil), cache in zip(ATT_GROUPS, (cache_kv_w128, cache_kv_w512, cache_kv_w2048)):
        assert win == dil * SPAN and cache.shape[2] == win

    w_ada_bf = w_ada[0].astype(BF16)
    w_in_bf = w_in[0].astype(BF16)
    wa, wb, wo = w_br_a[0].astype(BF16), w_br_b[0].astype(BF16), w_o[0].astype(BF16)
    wq_bf = w_peer_q[0].astype(BF16)
    sk_bf = peer_subkeys[0].astype(BF16)
    u_bf = peer_u[0].astype(BF16)
    v_bf = peer_v[0].astype(BF16)
    n1 = norm1_w[0].reshape(1, D_MODEL)
    n2 = norm2_w[0].reshape(1, D_MODEL)
    qn = jnp.tile(q_norm_w[0], N_GROUPS * ATT_HEADS).reshape(1, ATT_WIDTH)
    kn = jnp.tile(k_norm_w[0], N_GROUPS * ATT_HEADS).reshape(1, ATT_WIDTH)
    hn = hg_norm_w[0].reshape(1, HG_DV)
    seg = _block_diag_ones(MXU_TILE, ATT_HEAD_DIM).astype(BF16)
    tri = _block_diag_ones(128, HG_CHUNK) & (jnp.arange(128)[:, None] >= jnp.arange(128)[None, :])
    tri = jnp.tile(tri.astype(BF16), (1, 3))

    mods = _mods(jnp.concatenate([c_prompt, c_sample], axis=0), w_ada_bf, b_ada)
    mods_p = mods[:bsz].reshape(bsz, 1, 6 * D_MODEL)
    mods_s = mods[bsz:]

    n_p = bsz * seq
    tm_p = 512
    xp2 = x_prompt.reshape(n_p, D_MODEL)
    proj_p = _inproj(xp2, mods_p, False, seq, 1024, n1, w_in_bf, qn, kn, seg)
    proj_p3 = proj_p.reshape(bsz, seq, IN_WIDTH)
    att_p = _attn_prompt(proj_p3)
    hg_p, st_p = _hgrn_prompt(proj_p3, hg_lb_logits, hn, tri, 256)
    x1_p, h2_p = _mix(att_p,hg_p, proj_p, xp2, mods_p, False, seq, tm_p, n2, wa, wb, wo)
    y_p = _peer(h2_p, x1_p, mods_p, False, seq, wq_bf, sk_bf, u_bf, v_bf)

    xs2 = x_sample.reshape(dec, D_MODEL)
    proj_s = _inproj(xs2, mods_s, True, 1, dec, n1, w_in_bf, qn, kn, seg)
    att_s = _decode_attn(proj_s, (cache_kv_w128[0], cache_kv_w512[0], cache_kv_w2048[0]))
    hg_s, st_s = _decode_hgrn(proj_s, state_hgrn[0], hg_lb_logits, hn, 8)
    x1_s, h2_s = _mix(att_s,hg_s, proj_s, xs2, mods_s, True, 1, dec, n2, wa, wb, wo)
    y_s = _peer(h2_s, x1_s, mods_s, True, 1, wq_bf, sk_bf, u_bf, v_bf)

    kv_p = [_kv_rows(proj_p3, g, min(win, seq))[None] for g, (win, _) in enumerate(ATT_GROUPS)]
    proj_s3 = proj_s.reshape(1, dec, IN_WIDTH)
    kv_s = [_kv_rows(proj_s3, g, dec).reshape(1, dec, 1, 2, ATT_HEADS, ATT_HEAD_DIM) for g in range(N_GROUPS)]
    return (y_p.reshape(bsz, seq, D_MODEL), y_s.reshape(dec, 1, D_MODEL), kv_p[0], kv_p[1], kv_p[2], st_p[None],
            kv_s[0], kv_s[1], kv_s[2], st_s[None])
```

```python
import functools

import jax
import jax.numpy as jnp
from jax import lax
from jax.experimental import pallas as pl
from jax.experimental.pallas import tpu as pltpu

F32 = jnp.float32
BF16 = jnp.bfloat16

D_MODEL = 1024
ATT_GROUPS = ((128, 1), (512, 4), (2048, 16))
N_GROUPS = 3
ATT_HEADS = 8
ATT_HEAD_DIM = 64
ATT_OUT = ATT_HEADS * ATT_HEAD_DIM
ATT_WIDTH = N_GROUPS * ATT_OUT
MXU_TILE = 256
SPAN = 128
HG_HEADS = 8
HG_DK = 128
HG_DV = 128
PEER_HEADS = 8
PEER_KEYS = 128
PEER_TOPK = 16
PEER_HALF = 128
EPS = 1e-6
IN_WIDTH = 10752
COL = 512
NCOL = IN_WIDTH // COL
CB_Q, CB_K, CB_V = 0, 3, 6
CB_QH, CB_FH, CB_IH, CB_GH, CB_GA, CB_GB = 9, 11, 13, 15, 17, 19
HG_CHUNK = 16
W_PITCH = 136
VMEM_LIMIT = 56 * 1024 * 1024

NEG_INF = float("-inf")


def _silu(x):
    return x * jax.nn.sigmoid(x)


def _gelu_tanh(x):
    return 0.5 * x * (1.0 + jnp.tanh(0.7978845608028654 * (x + 0.044715 * (x * x * x))))


def _nt_dot(a, b):
    return lax.dot_general(a, b, (((1,), (1,)), ((), ())), preferred_element_type=F32)


def _params(*sem):
    return pltpu.CompilerParams(dimension_semantics=sem, vmem_limit_bytes=VMEM_LIMIT)


def _mods_kernel(c_ref, w_ref, b_ref, o_ref):
    s = _silu(c_ref[...])
    o_ref[...] = jnp.dot(s.astype(BF16), w_ref[...], preferred_element_type=F32) + b_ref[...]


def _mods(c, w_ada_bf, b_ada):
    n = c.shape[0]
    return pl.pallas_call(
        _mods_kernel,
        grid=(6,),
        in_specs=[pl.BlockSpec((n, D_MODEL), lambda j: (0, 0)),
                  pl.BlockSpec((D_MODEL, D_MODEL), lambda j: (0, j)),
                  pl.BlockSpec((1, D_MODEL), lambda j: (0, j))],
        out_specs=pl.BlockSpec((n, D_MODEL), lambda j: (0, j)),
        out_shape=jax.ShapeDtypeStruct((n, 6 * D_MODEL), F32),
        compiler_params=_params("arbitrary"),
        name="mods",
    )(c, w_ada_bf, b_ada)


def _mod_spec(per_row, tm, rows_per_batch, k):
    if per_row:
        return pl.BlockSpec((tm, D_MODEL), lambda i, *_: (i, k))
    tiles = rows_per_batch // tm
    return pl.BlockSpec((None, 1, D_MODEL), lambda i, *_: (i // tiles, 0, k))


def _inproj_kernel(x_ref, sc_ref, sh_ref, n1_ref, w_ref, qn_ref, kn_ref, seg_ref, o_ref, h_scr):
    j = pl.program_id(1)

    @pl.when(j == 0)
    def _():
        x = x_ref[...]
        ms = jnp.mean(x * x, axis=-1, keepdims=True)
        xn = x * lax.rsqrt(ms + EPS) * n1_ref[...]
        h_scr[...] = (xn * (1.0 + sc_ref[...]) + sh_ref[...]).astype(BF16)

    acc = jnp.dot(h_scr[...], w_ref[...], preferred_element_type=F32)

    def head_norm(w_row, scale):
        sq = (acc * acc).astype(BF16)
        seg_w = seg_ref.shape[0]
        ss = jnp.concatenate([jnp.dot(sq[:, c:c + seg_w], seg_ref[...], preferred_element_type=F32)
                              for c in range(0, ATT_WIDTH, seg_w)], axis=-1)
        return acc * lax.rsqrt(ss * (1.0 / ATT_HEAD_DIM) + EPS) * w_row * scale

    @pl.when(j == 0)
    def _():
        o_ref[...] = head_norm(qn_ref[...], ATT_HEAD_DIM ** -0.5)

    @pl.when(j == 1)
    def _():
        o_ref[...] = head_norm(kn_ref[...], 1.0)

    @pl.when(j >= 2)
    def _():
        o_ref[...] = acc


def _inproj(x2, mods, per_row, rows_per_batch, tm, n1, w_in_bf, qn, kn, seg):
    n = x2.shape[0]
    const = lambda i, j: (0, 0)
    return pl.pallas_call(
        _inproj_kernel,
        grid=(n // tm, IN_WIDTH // ATT_WIDTH),
        in_specs=[pl.BlockSpec((tm, D_MODEL), lambda i, j: (i, 0)),
                  _mod_spec(per_row, tm, rows_per_batch, 1),
                  _mod_spec(per_row, tm, rows_per_batch, 0),
                  pl.BlockSpec((1, D_MODEL), const),
                  pl.BlockSpec((D_MODEL, ATT_WIDTH), lambda i, j: (0, j)),
                  pl.BlockSpec((1, ATT_WIDTH), const),
                  pl.BlockSpec((1, ATT_WIDTH), const),
                  pl.BlockSpec(seg.shape, const)],
        out_specs=pl.BlockSpec((tm, ATT_WIDTH), lambda i, j: (i, j)),
        out_shape=jax.ShapeDtypeStruct((n, IN_WIDTH), F32),
        scratch_shapes=[pltpu.VMEM((tm, D_MODEL), BF16)],
        compiler_params=_params("arbitrary", "arbitrary"),
        name="inproj",
    )(x2, mods, mods, n1, w_in_bf, qn, kn, seg)


ATT_TILE = ATT_GROUPS[-1][1] * SPAN
PAIR = 2 * ATT_HEAD_DIM


def _class_rows(ref, start, n, dil):
    if dil == 1:
        return ref[pl.ds(start, n), :]
    return ref[pl.ds(start, n, stride=dil), :]


def _attn_fused_kernel(*refs):
    in_refs, (o_ref, og_scr, lg_scr) = refs[:5 * N_GROUPS], refs[5 * N_GROUPS:]
    tile = pl.program_id(1)
    lane = lax.broadcasted_iota(jnp.int32, (SPAN, PAIR), 1)
    lo = lane < ATT_HEAD_DIM
    qi = lax.broadcasted_iota(jnp.int32, (SPAN, 2 * SPAN), 0)
    kr = lax.broadcasted_iota(jnp.int32, (SPAN, 2 * SPAN), 1)
    delta = qi + SPAN - kr
    band = (delta >= 0) & (delta <= SPAN)
    cur_keys = kr >= SPAN

    def block(g, dil, r, jq, q_ref, k_ref, kp_ref, v_ref, vp_ref):
        q = _class_rows(q_ref, r + dil * SPAN * jq, SPAN, dil).astype(BF16)
        if jq == 0:
            k = jnp.concatenate([_class_rows(kp_ref, r, SPAN, dil), _class_rows(k_ref, r, SPAN, dil)], axis=0)
            v = jnp.concatenate([_class_rows(vp_ref, r, SPAN, dil), _class_rows(v_ref, r, SPAN, dil)], axis=0)
            valid = band & (cur_keys | (tile > 0))
        else:
            k = _class_rows(k_ref, r + dil * SPAN * (jq - 1), 2 * SPAN, dil)
            v = _class_rows(v_ref, r + dil * SPAN * (jq - 1), 2 * SPAN, dil)
            valid = band
        k = k.astype(BF16)
        v = v.astype(BF16)
        o_pair, lse_pair = None, None
        for first in (True, False):
            mine = lo if first else jnp.logical_not(lo)
            qh = jnp.where(mine, q, jnp.zeros_like(q))
            s = jnp.where(valid, _nt_dot(qh, k), NEG_INF)
            m = jnp.max(s, axis=-1, keepdims=True)
            p = jnp.exp(s - m)
            l = jnp.sum(p, axis=-1, keepdims=True)
            oh = jnp.dot(p.astype(BF16), v, preferred_element_type=F32) / l
            lse = m + jnp.log(l)
            o_pair = oh if first else jnp.where(lo, o_pair, oh)
            lse_pair = jnp.broadcast_to(lse, (SPAN, PAIR)) if first else jnp.where(lo, lse_pair, lse)
        start = r + dil * SPAN * jq
        if dil == 1:
            og_scr[g, pl.ds(start, SPAN), :] = o_pair
            lg_scr[g, pl.ds(start, SPAN), :] = lse_pair
        else:
            og_scr[g, pl.ds(start, SPAN, stride=dil), :] = o_pair
            lg_scr[g, pl.ds(start, SPAN, stride=dil), :] = lse_pair

    for g, (win, dil) in enumerate(ATT_GROUPS):
        grefs = in_refs[5 * g:5 * g + 5]
        nblk = ATT_TILE // (dil * SPAN)
        if dil == 1:
            for jq in range(nblk):
                block(g, dil, 0, jq, *grefs)
        else:
            def residue(r, carry, g=g, dil=dil, nblk=nblk, grefs=grefs):
                for jq in range(nblk):
                    block(g, dil, r, jq, *grefs)
                return carry
            lax.fori_loop(0, dil, residue, 0, unroll=max(1, 4 // nblk))

    lses = [lg_scr[g] for g in range(N_GROUPS)]
    mx = functools.reduce(jnp.maximum, lses)
    es = [jnp.exp(x - mx) for x in lses]
    num = es[0] * og_scr[0] + es[1] * og_scr[1] + es[2] * og_scr[2]
    o_ref[...] = (num / (es[0] + es[1] + es[2])).astype(o_ref.dtype)


def _attn_prompt(proj3):
    b, s, _ = proj3.shape
    npair = ATT_OUT // PAIR
    in_specs = []
    for g, (win, dil) in enumerate(ATT_GROUPS):
        prev_rows = dil * SPAN
        per_tile = ATT_TILE // prev_rows
        for cb in (CB_Q, CB_K, CB_V):
            col = lambda hp, cb=cb, g=g: (cb + g) * (COL // PAIR) + hp
            in_specs.append(pl.BlockSpec((None, ATT_TILE, PAIR), lambda bi, t, hp, col=col: (bi, t, col(hp))))
            if cb != CB_Q:
                in_specs.append(pl.BlockSpec(
                    (None, prev_rows, PAIR),
                    lambda bi, t, hp, col=col, per_tile=per_tile: (bi, jnp.maximum(t * per_tile - 1, 0), col(hp))))
    att = pl.pallas_call(
        _attn_fused_kernel,
        grid=(b, s // ATT_TILE, npair),
        in_specs=in_specs,
        out_specs=pl.BlockSpec((None, ATT_TILE, PAIR), lambda bi, t, hp: (bi, t, hp)),
        out_shape=jax.ShapeDtypeStruct((b, s, ATT_OUT), BF16),
        scratch_shapes=[pltpu.VMEM((N_GROUPS, ATT_TILE, PAIR), F32), pltpu.VMEM((N_GROUPS, ATT_TILE, PAIR), F32)],
        compiler_params=_params("arbitrary", "arbitrary", "arbitrary"),
        name="attn_prompt",
    )(*([proj3] * len(in_specs)))
    return att.reshape(b * s, ATT_OUT)


def _lower_bound(logits):
    mx = jnp.max(logits, axis=0, keepdims=True)
    e = jnp.exp(logits - mx)
    return e[0:1] / jnp.sum(e, axis=0, keepdims=True)


HG_SEQS = 1


def _hgrn_kernel(qh_ref, fh_ref, ih_ref, gh_ref, lbl_ref, hn_ref, tri_ref, o_ref, st_ref,
                 st_scr, q_scr, k_scr, b_scr, *, tb):
    t = pl.program_id(2)
    nh = COL // HG_DK
    c_rows = HG_CHUNK
    rows_all = HG_SEQS * tb

    @pl.when(t == 0)
    def _():
        st_scr[...] = jnp.zeros_like(st_scr)

    lb = _lower_bound(lbl_ref[...])
    f = lb + (1.0 - lb) * jax.nn.sigmoid(fh_ref[...].reshape(rows_all, COL))
    k_scr[...] = 1.0 - f
    q_scr[...] = _silu(qh_ref[...].reshape(rows_all, COL))
    logf = jnp.log(f)
    hi = logf.astype(BF16)
    rest = logf - hi.astype(F32)
    mid = rest.astype(BF16)
    low = (rest - mid.astype(F32)).astype(BF16)
    for r in range(rows_all // 128):
        rows = slice(r * 128, (r + 1) * 128)
        pieces = jnp.concatenate([hi[rows, :], mid[rows, :], low[rows, :]], axis=0)
        b_scr[rows, :] = jnp.dot(tri_ref[...], pieces, preferred_element_type=F32)
    rowid8 = lax.broadcasted_iota(jnp.int32, (8, 1), 0)

    def chunk(c, carry):
        r0 = pl.multiple_of(c * c_rows, c_rows)
        rows = pl.ds(r0, c_rows)
        for sq in range(HG_SEQS):
            srows = pl.ds(pl.multiple_of(sq * tb + r0, c_rows), c_rows)
            for h in range(nh):
                cols = slice(h * HG_DK, (h + 1) * HG_DK)
                b = b_scr[srows, cols]
                qc = q_scr[srows, cols]
                kc = k_scr[srows, cols]
                vc = ih_ref[sq, rows, cols]
                st = st_scr[sq * nh + h]
                o_state = _nt_dot((qc * jnp.exp(b)).astype(BF16), st.astype(BF16))
                prods, where = [], []
                for g8 in range(c_rows // 8):
                    tr = slice(g8 * 8, (g8 + 1) * 8)
                    bg, qg = b[tr], qc[tr]
                    for s in range(g8 * 8 + 8):
                        d = bg - b[s:s + 1]
                        if s >= g8 * 8:
                            d = jnp.where(rowid8 >= s - g8 * 8, d, NEG_INF)
                        prods.append(qg * kc[s:s + 1] * jnp.exp(d))
                        where.append((g8, s))
                x = jnp.concatenate(prods, axis=0).astype(BF16)
                sums = jnp.dot(x, jnp.ones((HG_DK, HG_DV), BF16), preferred_element_type=F32)
                parts = [o_state[g8 * 8:(g8 + 1) * 8] for g8 in range(c_rows // 8)]
                for i, (g8, s) in enumerate(where):
                    parts[g8] = parts[g8] + sums[i * 8:(i + 1) * 8] * vc[s:s + 1]
                o = jnp.concatenate(parts, axis=0)
                bl = b[c_rows - 1:c_rows]
                kt = kc * jnp.exp(bl - b)
                upd = lax.dot_general(vc.astype(BF16), kt.astype(BF16), (((0,), (0,)), ((), ())),
                                      preferred_element_type=F32)
                st_scr[sq * nh + h] = st * jnp.exp(bl) + upd
                ms = jnp.mean(o * o, axis=-1, keepdims=True)
                on = o * lax.rsqrt(ms + EPS) * hn_ref[...]
                o_ref[sq, rows, cols] = on * _silu(gh_ref[sq, rows, cols])
        return carry

    lax.fori_loop(0, tb // c_rows, chunk, 0, unroll=2)

    @pl.when(t == pl.num_programs(2) - 1)
    def _():
        for sq in range(HG_SEQS):
            for h in range(nh):
                st_ref[sq, h] = st_scr[sq * nh + h].T


def _hgrn_prompt(proj3, lb_logits, hn, tri, tb):
    b, s, _ = proj3.shape
    assert b % HG_SEQS == 0
    nhb = HG_HEADS * HG_DK // COL
    nh = COL // HG_DK

    def spec(cb):
        return pl.BlockSpec((HG_SEQS, tb, COL), lambda bi, hb, t: (bi, t, cb + hb))

    o, st = pl.pallas_call(
        functools.partial(_hgrn_kernel, tb=tb),
        grid=(b // HG_SEQS, nhb, s // tb),
        in_specs=[spec(CB_QH), spec(CB_FH), spec(CB_IH), spec(CB_GH),
                  pl.BlockSpec((lb_logits.shape[0], COL), lambda bi, hb, t: (0, hb)),
                  pl.BlockSpec((1, HG_DV), lambda bi, hb, t: (0, 0)),
                  pl.BlockSpec(tri.shape, lambda bi, hb, t: (0, 0))],
        out_specs=[pl.BlockSpec((HG_SEQS, tb, COL), lambda bi, hb, t: (bi, t, hb)),
                   pl.BlockSpec((HG_SEQS, nh, HG_DK, HG_DV), lambda bi, hb, t: (bi, hb, 0, 0))],
        out_shape=[jax.ShapeDtypeStruct((b, s, HG_HEADS * HG_DV), F32),
                   jax.ShapeDtypeStruct((b, HG_HEADS, HG_DK, HG_DV), F32)],
        scratch_shapes=[pltpu.VMEM((HG_SEQS * nh, HG_DV, HG_DK), F32)]
                       + [pltpu.VMEM((HG_SEQS * tb, COL), F32)] * 3,
        compiler_params=_params("arbitrary", "arbitrary", "arbitrary"),
        name="hgrn_prompt",
    )(proj3, proj3, proj3, proj3, lb_logits, hn, tri)
    return o.reshape(b * s, HG_HEADS * HG_DV), st


def _mix_kernel(att_ref, hg_ref, ga0, ga1, gb0, gb1, x_ref, g1_ref, sc2_ref, sh2_ref, n2_ref,
                wa_ref, wb_ref, wo_ref, x1_ref, h2_ref):
    att = att_ref[...]
    ga = jnp.concatenate([ga0[...], ga1[...]], axis=-1)
    gb = jnp.concatenate([gb0[...], gb1[...]], axis=-1)
    ya = jnp.dot(att.astype(BF16), wa_ref[...], preferred_element_type=F32)
    yb = jnp.dot(hg_ref[...].astype(BF16), wb_ref[...], preferred_element_type=F32)
    y = jax.nn.sigmoid(ga) * ya + jax.nn.sigmoid(gb) * yb
    x1 = x_ref[...] + g1_ref[...] * jnp.dot(y.astype(BF16), wo_ref[...], preferred_element_type=F32)
    x1_ref[...] = x1
    ms = jnp.mean(x1 * x1, axis=-1, keepdims=True)
    xn = x1 * lax.rsqrt(ms + EPS) * n2_ref[...]
    h2_ref[...] = (xn * (1.0 + sc2_ref[...]) + sh2_ref[...]).astype(BF16)


def _mix(att, hg, proj2, x2, mods, per_row, rows_per_batch, tm, n2, wa, wb, wo):
    n = x2.shape[0]
    row = lambda w: pl.BlockSpec((tm, w), lambda i: (i, 0))
    colblk = lambda cb: pl.BlockSpec((tm, COL), lambda i: (i, cb))
    full = lambda a: pl.BlockSpec(a.shape, lambda i: (0, 0))
    in_specs = [row(ATT_OUT), row(D_MODEL), colblk(CB_GA), colblk(CB_GA + 1), colblk(CB_GB), colblk(CB_GB + 1),
                row(D_MODEL), _mod_spec(per_row, tm, rows_per_batch, 2), _mod_spec(per_row, tm, rows_per_batch, 4),
                _mod_spec(per_row, tm, rows_per_batch, 3), full(n2), full(wa), full(wb), full(wo)]
    return pl.pallas_call(
        _mix_kernel,
        grid=(n // tm,),
        in_specs=in_specs,
        out_specs=[row(D_MODEL), row(D_MODEL)],
        out_shape=[jax.ShapeDtypeStruct((n, D_MODEL), F32), jax.ShapeDtypeStruct((n, D_MODEL), BF16)],
        compiler_params=_params("arbitrary"),
        name="mix",
    )(att, hg, proj2, proj2, proj2, proj2, x2, mods, mods, mods, n2, wa, wb, wo)


TOK_LANES = 128
CAND_ROWS = ((0, 16), (1, 16), (2, 8), (3, 8), (4, 8), (5, 8), (6, 8), (7, 8))


def _top16_rows(s, ridx, sentinel):
    slabs = [(s[r:r + 8], ridx[r:r + 8]) for r in range(0, s.shape[0], 8)]
    vals, idxs = [], []
    for _ in range(PEER_TOPK):
        level = slabs
        while len(level) > 1:
            nxt = []
            for i in range(0, len(level) - 1, 2):
                (va, ia), (vb, ib) = level[i], level[i + 1]
                keep_a = va >= vb
                nxt.append((jnp.where(keep_a, va, vb), jnp.where(keep_a, ia, ib)))
            level = nxt + ([level[-1]] if len(level) % 2 else [])
        v8, i8 = level[0]
        m = jnp.max(v8, axis=0, keepdims=True)
        am = jnp.min(jnp.where(v8 == m, i8, sentinel), axis=0, keepdims=True)
        vals.append(m)
        idxs.append(am)
        slabs = [(jnp.where(ix == am, NEG_INF, sv), ix) for sv, ix in slabs]
    return jnp.concatenate(vals, axis=0), jnp.concatenate(idxs, axis=0)


def _route_kernel(h2_ref, wq_ref, sk_ref, a_ref, b_ref, g_ref, q_scr, at_scr, bt_scr, gt_scr):
    tm = h2_ref.shape[0]
    tt = TOK_LANES
    q = jnp.dot(h2_ref[...], wq_ref[...], preferred_element_type=F32)
    for hp in range(2 * PEER_HEADS):
        q_scr[hp] = q[:, hp * PEER_HALF:(hp + 1) * PEER_HALF].astype(BF16)
    kidx = lax.broadcasted_iota(jnp.int32, (PEER_KEYS, tt), 0)
    sub16 = lax.broadcasted_iota(jnp.int32, (PEER_TOPK, tt), 0)
    sub8 = lax.broadcasted_iota(jnp.int32, (8, tt), 0)
    cflat = jnp.concatenate([p * PEER_TOPK + (sub16 if nq == 16 else sub8) for p, nq in CAND_ROWS]
                            + [(sub8 + 8) * PEER_TOPK], axis=0)

    def group(gi, carry):
        rows = pl.ds(pl.multiple_of(gi * tt, tt), tt)

        def sub_key_top(h):
            v1, i1 = _top16_rows(_nt_dot(sk_ref[0], q_scr[2 * h, rows, :]), kidx, PEER_KEYS)
            v2, i2 = _top16_rows(_nt_dot(sk_ref[1], q_scr[2 * h + 1, rows, :]), kidx, PEER_KEYS)
            return v1, i1, v2, i2

        def select(h, tops):
            v1, i1, v2, i2 = tops
            cand = jnp.concatenate([v1[p:p + 1] + v2[0:nq] for p, nq in CAND_ROWS] + [v1[8:16] + v2[0:1]], axis=0)
            tv, tp = _top16_rows(cand, cflat, PEER_TOPK * PEER_TOPK)
            e = jnp.exp(tv - tv[0:1])
            g = e / jnp.sum(e, axis=0, keepdims=True)
            pr = tp >> 4
            qr = tp & 15
            a_sel = jnp.zeros((PEER_TOPK, tt), jnp.int32)
            b_sel = jnp.zeros((PEER_TOPK, tt), jnp.int32)
            for p in range(PEER_TOPK):
                a_sel = jnp.where(pr == p, i1[p:p + 1], a_sel)
                b_sel = jnp.where(qr == p, i2[p:p + 1], b_sel)
            first = h * PEER_TOPK if isinstance(h, int) else pl.multiple_of(h * PEER_TOPK, PEER_TOPK)
            slot = pl.ds(first, PEER_TOPK)
            at_scr[slot, :] = a_sel
            bt_scr[slot, :] = b_sel
            gt_scr[slot, :] = g

        def head(h, tops):
            select(h, tops)
            return sub_key_top(h + 1)

        select(PEER_HEADS - 1, lax.fori_loop(0, PEER_HEADS - 1, head, sub_key_top(0)))
        a_ref[rows, :] = at_scr[...].T
        b_ref[rows, :] = bt_scr[...].T
        g_ref[rows, :] = gt_scr[...].T
        return carry

    lax.fori_loop(0, tm // tt, group, 0)


def _route(h2, wq_bf, sk_bf, tm):
    n = h2.shape[0]
    assert tm % TOK_LANES == 0 and n % tm == 0
    row = pl.BlockSpec((tm, 128), lambda i: (i, 0))
    slots = PEER_HEADS * PEER_TOPK
    return pl.pallas_call(
        _route_kernel,
        grid=(n // tm,),
        in_specs=[pl.BlockSpec((tm, D_MODEL), lambda i: (i, 0)),
                  pl.BlockSpec(wq_bf.shape, lambda i: (0, 0)),
                  pl.BlockSpec(sk_bf.shape, lambda i: (0, 0, 0))],
        out_specs=[row, row, row],
        out_shape=[jax.ShapeDtypeStruct((n, 128), jnp.int32), jax.ShapeDtypeStruct((n, 128), jnp.int32),
                   jax.ShapeDtypeStruct((n, 128), F32)],
        scratch_shapes=[pltpu.VMEM((2 * PEER_HEADS, tm, PEER_HALF), BF16),
                        pltpu.VMEM((slots, TOK_LANES), jnp.int32), pltpu.VMEM((slots, TOK_LANES), jnp.int32),
                        pltpu.VMEM((slots, TOK_LANES), F32)],
        compiler_params=_params("arbitrary"),
        name="peer_route",
    )(h2, wq_bf, sk_bf)


def _peer_u_kernel(h2_ref, u_ref, a_ref, b_ref, g_ref, w_ref, act_scr, *, ac):
    c = pl.program_id(1)

    @pl.when(c == 0)
    def _():
        act_scr[...] = jnp.zeros_like(act_scr)

    hc = _nt_dot(h2_ref[...], u_ref[...])
    a_idx = a_ref[...]
    b_idx = b_ref[...]
    act = act_scr[...]
    for i in range(ac):
        gathered = jnp.take_along_axis(hc[:, i * 128:(i + 1) * 128], b_idx, axis=1)
        act = jnp.where(a_idx == c * ac + i, gathered, act)
    act_scr[...] = act

    @pl.when(c == pl.num_programs(1) - 1)
    def _():
        w_ref[...] = g_ref[...] * _gelu_tanh(act)


def _peer_u(h2, u_bf, a_idx, b_idx, gate, tm, ac):
    n = h2.shape[0]
    row = pl.BlockSpec((tm, 128), lambda i, c: (i, 0))
    return pl.pallas_call(
        functools.partial(_peer_u_kernel, ac=ac),
        grid=(n // tm, PEER_KEYS // ac),
        in_specs=[pl.BlockSpec((tm, D_MODEL), lambda i, c: (i, 0)),
                  pl.BlockSpec((ac * 128, D_MODEL), lambda i, c: (c, 0)),
                  row, row, row],
        out_specs=row,
        out_shape=jax.ShapeDtypeStruct((n, 128), F32),
        scratch_shapes=[pltpu.VMEM((tm, 128), F32)],
        compiler_params=_params("arbitrary", "arbitrary"),
        name="peer_u",
    )(h2, u_bf, a_idx, b_idx, gate)


def _peer_v_kernel(a_ref, b_ref, w_ref, v_ref, x1_ref, g2_ref, o_ref, w3_scr, acc_scr, *, ac, tm):
    c = pl.program_id(1)

    @pl.when(c == 0)
    def _():
        sub = lax.broadcasted_iota(jnp.int32, (128, 128), 0)

        def build(n, carry):
            ar = a_ref[pl.ds(n, 1), :]
            br = b_ref[pl.ds(n, 1), :]
            wr = w_ref[pl.ds(n, 1), :]
            at = jnp.where(sub == ar, 1.0, 0.0).astype(BF16)
            rt = jnp.where(sub == br, wr, 0.0).astype(BF16)
            w3_scr[pl.ds(pl.multiple_of(n * W_PITCH, 8), 128), :] = _nt_dot(at, rt)
            return carry

        lax.fori_loop(0, tm, build, 0, unroll=32)

    part = None
    for i in range(0, ac, 2):
        lhs = jnp.concatenate([w3_scr[pl.ds(c * ac + i, tm, stride=W_PITCH), :],
                               w3_scr[pl.ds(c * ac + i + 1, tm, stride=W_PITCH), :]], axis=-1)
        d = jnp.dot(lhs.astype(BF16), v_ref[i * 128:(i + 2) * 128, :], preferred_element_type=F32)
        part = d if part is None else part + d

    @pl.when(c == 0)
    def _():
        acc_scr[...] = part

    @pl.when(c > 0)
    def _():
        acc_scr[...] += part

    @pl.when(c == pl.num_programs(1) - 1)
    def _():
        o_ref[...] = x1_ref[...] + g2_ref[...] * acc_scr[...]


def _peer_v(a_idx, b_idx, wts, v_bf, x1, mods, per_row, rows_per_batch, tm, ac):
    n = x1.shape[0]
    row = pl.BlockSpec((tm, 128), lambda i, c: (i, 0))
    wide = pl.BlockSpec((tm, D_MODEL), lambda i, c: (i, 0))
    return pl.pallas_call(
        functools.partial(_peer_v_kernel, ac=ac, tm=tm),
        grid=(n // tm, PEER_KEYS // ac),
        in_specs=[row, row, row,
                  pl.BlockSpec((ac * 128, D_MODEL), lambda i, c: (c, 0)),
                  wide, _mod_spec(per_row, tm, rows_per_batch, 5)],
        out_specs=wide,
        out_shape=jax.ShapeDtypeStruct((n, D_MODEL), F32),
        scratch_shapes=[pltpu.VMEM((tm * W_PITCH, 128), F32), pltpu.VMEM((tm, D_MODEL), F32)],
        compiler_params=_params("arbitrary", "arbitrary"),
        name="peer_v",
    )(a_idx, b_idx, wts, v_bf, x1, mods)


def _decode_attn_kernel(q_ref, k_ref, v_ref, c0_ref, c1_ref, c2_ref, o_ref):
    caches = (c0_ref, c1_ref, c2_ref)
    lses, outs = [], []
    for g, (win, dil) in enumerate(ATT_GROUPS):
        length = caches[g].shape[-1]
        pos = lax.broadcasted_iota(jnp.int32, (1, length), 1)
        tap = pos % dil == 0
        o_cols, lse_cols = [], []
        for h in range(ATT_HEADS):
            q = q_ref[g][:, h:h + 1]
            kn = k_ref[g][:, h:h + 1]
            vn = v_ref[g][:, h:h + 1]
            s = jnp.sum(caches[g][0, h] * q, axis=0, keepdims=True)
            s = jnp.where(tap, s, NEG_INF)
            s0 = jnp.sum(kn * q, axis=0, keepdims=True)
            m = jnp.maximum(jnp.max(s, axis=1, keepdims=True), s0)
            p = jnp.exp(s - m)
            p0 = jnp.exp(s0 - m)
            l = jnp.sum(p, axis=1, keepdims=True) + p0
            o_cols.append((jnp.sum(caches[g][1, h] * p, axis=1, keepdims=True) + p0 * vn) / l)
            lse_cols.append(m + jnp.log(l))
        outs.append(jnp.concatenate(o_cols, axis=1))
        lses.append(jnp.concatenate(lse_cols, axis=1))
    mx = functools.reduce(jnp.maximum, lses)
    es = [jnp.exp(x - mx) for x in lses]
    o_ref[...] = (es[0] * outs[0] + es[1] * outs[1] + es[2] * outs[2]) / (es[0] + es[1] + es[2])


def _decode_attn(proj_s, caches):
    n = proj_s.shape[0]

    def cols_t(cb):
        x = proj_s[:, cb * COL:cb * COL + ATT_WIDTH].reshape(n, N_GROUPS, ATT_HEADS, ATT_HEAD_DIM)
        return x.transpose(0, 1, 3, 2)

    views, specs = [], []
    for cache in caches:
        views.append(cache.transpose(0, 2, 3, 4, 1))
        specs.append(pl.BlockSpec((None,) + views[-1].shape[1:], lambda i: (i, 0, 0, 0, 0)))
    qkv_spec = pl.BlockSpec((None, N_GROUPS, ATT_HEAD_DIM, ATT_HEADS), lambda i: (i, 0, 0, 0))
    att = pl.pallas_call(
        _decode_attn_kernel,
        grid=(n,),
        in_specs=[qkv_spec, qkv_spec, qkv_spec] + specs,
        out_specs=pl.BlockSpec((None, ATT_HEAD_DIM, ATT_HEADS), lambda i: (i, 0, 0)),
        out_shape=jax.ShapeDtypeStruct((n, ATT_HEAD_DIM, ATT_HEADS), F32),
        compiler_params=_params("arbitrary"),
        name="decode_attn",
    )(cols_t(CB_Q), cols_t(CB_K), cols_t(CB_V), *views)
    return att.transpose(0, 2, 1).reshape(n, ATT_OUT)


def _decode_hgrn_kernel(q0, q1, f0, f1, i0, i1, g0, g1, lbl_ref, hn_ref, st_ref, o_ref, sto_ref, *, bt):
    nh = COL // HG_DK
    lb = _lower_bound(lbl_ref[...])
    for h in range(HG_HEADS):
        qr, fr, ir, gr = ((q0, f0, i0, g0), (q1, f1, i1, g1))[h // nh]
        cols = slice((h % nh) * HG_DK, (h % nh + 1) * HG_DK)
        lbh = lb[:, h * HG_DK:(h + 1) * HG_DK]
        f = lbh + (1.0 - lbh) * jax.nn.sigmoid(fr[:, cols])
        ft = f.T
        kt = 1.0 - ft
        qt = _silu(qr[:, cols]).T
        for i in range(bt):
            v = ir[i:i + 1, cols]
            s_new = ft[:, i:i + 1] * st_ref[i, h] + kt[:, i:i + 1] * v
            sto_ref[i, h] = s_new
            o = jnp.sum(qt[:, i:i + 1] * s_new, axis=0, keepdims=True)
            ms = jnp.mean(o * o, axis=-1, keepdims=True)
            o_ref[i:i + 1, h * HG_DV:(h + 1) * HG_DV] = (o * lax.rsqrt(ms + EPS) * hn_ref[...]
                                                         * _silu(gr[i:i + 1, cols]))


def _decode_hgrn(proj_s, state, lb_logits, hn, bt):
    n = proj_s.shape[0]
    st_spec = pl.BlockSpec((bt, HG_HEADS, HG_DK, HG_DV), lambda i: (i, 0, 0, 0))
    half = lambda cb: pl.BlockSpec((bt, COL), lambda i: (i, cb))
    return pl.pallas_call(
        functools.partial(_decode_hgrn_kernel, bt=bt),
        grid=(n // bt,),
        in_specs=[half(CB_QH), half(CB_QH + 1), half(CB_FH), half(CB_FH + 1), half(CB_IH), half(CB_IH + 1),
                  half(CB_GH), half(CB_GH + 1),
                  pl.BlockSpec(lb_logits.shape, lambda i: (0, 0)),
                  pl.BlockSpec((1, HG_DV), lambda i: (0, 0)),
                  st_spec],
        out_specs=[pl.BlockSpec((bt, HG_HEADS * HG_DV), lambda i: (i, 0)), st_spec],
        out_shape=[jax.ShapeDtypeStruct((n, HG_HEADS * HG_DV), F32),
                   jax.ShapeDtypeStruct(state.shape, F32)],
        compiler_params=_params("arbitrary"),
        name="decode_hgrn",
    )(proj_s, proj_s, proj_s, proj_s, proj_s, proj_s, proj_s, proj_s, lb_logits, hn, state)


def _block_diag_ones(n, seg):
    i = jnp.arange(n)
    return (i[:, None] // seg == i[None, :] // seg)


def _peer(h2, x1, mods, per_row, rows_per_batch, wq_bf, sk_bf, u_bf, v_bf):
    n = h2.shape[0]
    tm_u, ac_u = min(512, n), 32
    tm_v, ac_v = min(256, n), 32
    a_idx, b_idx, gate = _route(h2, wq_bf, sk_bf, min(512, n))
    wts = _peer_u(h2, u_bf, a_idx, b_idx, gate, tm_u, ac_u)
    return _peer_v(a_idx, b_idx, wts, v_bf, x1, mods, per_row, rows_per_batch, tm_v, ac_v)


def _kv_rows_kernel(k_ref, v_ref, o_ref):
    for j, ref in enumerate((k_ref, v_ref)):
        t = ref[...].T
        for h in range(ATT_HEADS):
            o_ref[j, h] = t[h * ATT_HEAD_DIM:(h + 1) * ATT_HEAD_DIM, :]


def _kv_rows(proj3, g, rows):
    b, s, _ = proj3.shape
    tr = min(256, rows)
    assert rows % tr == 0 and (s - rows) % tr == 0
    first = (s - rows) // tr
    out = pl.pallas_call(
        _kv_rows_kernel,
        grid=(b, rows // tr),
        in_specs=[pl.BlockSpec((None, tr, COL), lambda bi, t: (bi, first + t, CB_K + g)),
                  pl.BlockSpec((None, tr, COL), lambda bi, t: (bi, first + t, CB_V + g))],
        out_specs=pl.BlockSpec((None, 2, ATT_HEADS, ATT_HEAD_DIM, tr), lambda bi, t: (bi, 0, 0, 0, t)),
        out_shape=jax.ShapeDtypeStruct((b, 2, ATT_HEADS, ATT_HEAD_DIM, rows), F32),
        compiler_params=_params("arbitrary", "arbitrary"),
        name=f"kv_rows_g{g}",
    )(proj3, proj3)
    return out.transpose(0, 4, 1, 2, 3)


def kernel(x_prompt, x_sample, cache_kv_w128, cache_kv_w512, cache_kv_w2048, state_hgrn, c_prompt, c_sample, w_ada, b_ada, norm1_w, norm2_w, w_in, q_norm_w, k_norm_w, hg_lb_logits, hg_norm_w, w_br_a, w_br_b, w_o, w_peer_q, peer_subkeys, peer_u, peer_v):
    bsz, seq, _ = x_prompt.shape
    dec, dec_t, _ = x_sample.shape
    assert w_ada.shape[0] == 1 and dec_t == 1 and seq % (ATT_GROUPS[-1][1] * SPAN) == 0
    for (win, dil), cache in zip(ATT_GROUPS, (cache_kv_w128, cache_kv_w512, cache_kv_w2048)):
        assert win == dil * SPAN and cache.shape[2] == win

    w_ada_bf = w_ada[0].astype(BF16)
    w_in_bf = w_in[0].astype(BF16)
    wa, wb, wo = w_br_a[0].astype(BF16), w_br_b[0].astype(BF16), w_o[0].astype(BF16)
    wq_bf = w_peer_q[0].astype(BF16)
    sk_bf = peer_subkeys[0].astype(BF16)
    u_bf = peer_u[0].astype(BF16)
    v_bf = peer_v[0].astype(BF16)
    n1 = norm1_w[0].reshape(1, D_MODEL)
    n2 = norm2_w[0].reshape(1, D_MODEL)
    qn = jnp.tile(q_norm_w[0], N_GROUPS * ATT_HEADS).reshape(1, ATT_WIDTH)
    kn = jnp.tile(k_norm_w[0], N_GROUPS * ATT_HEADS).reshape(1, ATT_WIDTH)
    hn = hg_norm_w[0].reshape(1, HG_DV)
    seg = _block_diag_ones(MXU_TILE, ATT_HEAD_DIM).astype(BF16)
    tri = _block_diag_ones(128, HG_CHUNK) & (jnp.arange(128)[:, None] >= jnp.arange(128)[None, :])
    tri = jnp.tile(tri.astype(BF16), (1, 3))

    mods = _mods(jnp.concatenate([c_prompt, c_sample], axis=0), w_ada_bf, b_ada)
    mods_p = mods[:bsz].reshape(bsz, 1, 6 * D_MODEL)
    mods_s = mods[bsz:]

    n_p = bsz * seq
    tm_p = 512
    xp2 = x_prompt.reshape(n_p, D_MODEL)
    proj_p = _inproj(xp2, mods_p, False, seq, 1024, n1, w_in_bf, qn, kn, seg)
    proj_p3 = proj_p.reshape(bsz, seq, IN_WIDTH)
    att_p = _attn_prompt(proj_p3)
    hg_p, st_p = _hgrn_prompt(proj_p3, hg_lb_logits, hn, tri, 512)
    x1_p, h2_p = _mix(att_p,hg_p, proj_p, xp2, mods_p, False, seq, tm_p, n2, wa, wb, wo)
    y_p = _peer(h2_p, x1_p, mods_p, False, seq, wq_bf, sk_bf, u_bf, v_bf)

    xs2 = x_sample.reshape(dec, D_MODEL)
    proj_s = _inproj(xs2, mods_s, True, 1, dec, n1, w_in_bf, qn, kn, seg)
    att_s = _decode_attn(proj_s, (cache_kv_w128[0], cache_kv_w512[0], cache_kv_w2048[0]))
    hg_s, st_s = _decode_hgrn(proj_s, state_hgrn[0], hg_lb_logits, hn, 8)
    x1_s, h2_s = _mix(att_s,hg_s, proj_s, xs2, mods_s, True, 1, dec, n2, wa, wb, wo)
    y_s = _peer(h2_s, x1_s, mods_s, True, 1, wq_bf, sk_bf, u_bf, v_bf)

    kv_p = [_kv_rows(proj_p3, g, min(win, seq))[None] for g, (win, _) in enumerate(ATT_GROUPS)]
    proj_s3 = proj_s.reshape(1, dec, IN_WIDTH)
    kv_s = [_kv_rows(proj_s3, g, dec).reshape(1, dec, 1, 2, ATT_HEADS, ATT_HEAD_DIM) for g in range(N_GROUPS)]
    return (y_p.reshape(bsz, seq, D_MODEL), y_s.reshape(dec, 1, D_MODEL), kv_p[0], kv_p[1], kv_p[2], st_p[None],
            kv_s[0], kv_s[1], kv_s[2], st_s[None])
```
